```python
import math
import jax
import jax.numpy as jnp
from jax import lax
import numpy as np

D_MODEL = 4096
BATCH = 4
SEQ = 2048
DEPTH = 4
DEC_BATCH = 128
DEC_SEQ = 4
PAST_LEN = 16384
PAGE_SIZE = 128

W_A = D_MODEL // 4
W_B = 3 * D_MODEL // 8
W_C = 3 * D_MODEL // 8
MIX_W = W_A + W_B + W_C
BS_A = 64
NB_A = W_A // BS_A
CONV_W = 4
LRU_C = 8.0
HEAD_B = 64
H_B = W_B // HEAD_B
LORA_W = 128
LORA_A = 128
GN_EPS = 64e-5
DK_C = 128
H_C = W_C // DK_C
DV_C = W_C // H_C
CHUNK_C = 64
TINY = 1e-30
PLE_DIM = 256
RMS_EPS = 1e-6
IN_W = 2 * W_A + 4 * W_B + 4 * W_C

kernel_name = 'hymba_style_rglru_rwkv7_hgrn2_decode_step'


def _rms(x, eps=RMS_EPS):
    xf = x.astype(jnp.float32)
    return xf * lax.rsqrt(jnp.mean(xf * xf, axis=-1, keepdims=True) + eps)


def rmsnorm(x, g):
    return (_rms(x) * g.astype(jnp.float32)).astype(x.dtype)


def _shift(seq, prev):
    return jnp.concatenate([prev[:, None].astype(seq.dtype), seq[:, :-1]], axis=1)


def _rglru(xa, gate, h0, conv_buf, conv_w, conv_b, wr, br, wi, bi, lam):
    B, T, _ = xa.shape
    xp = jnp.concatenate([conv_buf.astype(xa.dtype), xa], axis=1)
    u = conv_b
    for j in range(CONV_W):
        u = u + xp[:, j:j + T] * conv_w[j]
    new_buf = xp[:, T:]
    ub = u.reshape(B, T, NB_A, BS_A)
    r = jax.nn.sigmoid(jnp.einsum('btnc,ncd->btnd', ub, wr).reshape(B, T, W_A) + br)
    i = jax.nn.sigmoid(jnp.einsum('btnc,ncd->btnd', ub, wi).reshape(B, T, W_A) + bi)
    log_a = -LRU_C * r.astype(jnp.float32) * jax.nn.softplus(-lam.astype(jnp.float32))
    a = jnp.exp(log_a)
    b = jnp.sqrt(-jnp.expm1(2.0 * log_a)) * (i * u).astype(jnp.float32)

    def comb(c1, c2):
        a1, b1 = c1
        a2, b2 = c2
        return a1 * a2, a2 * b1 + b2

    a_cum, b_cum = lax.associative_scan(comb, (a, b), axis=1)
    h = a_cum * h0.astype(jnp.float32)[:, None] + b_cum
    out = h.astype(xa.dtype) * jax.nn.silu(gate)
    return out, h[:, -1].astype(xa.dtype), new_buf


def _rwkv7_scan(r, w, k, v, kk, a, S0):
    xs = tuple(jnp.moveaxis(t, 1, 0) for t in (r, w, k, v, kk, a))

    def step(S, inp):
        r_t, w_t, k_t, v_t, kk_t, a_t = inp
        sa = jnp.einsum('bhvk,bhk->bhv', S, kk_t)
        S = (S * w_t[:, :, None, :] - sa[..., None] * (kk_t * a_t)[:, :, None, :]
             + v_t[..., None] * k_t[:, :, None, :])
        y = jnp.einsum('bhvk,bhk->bhv', S, r_t)
        return S, y

    S, ys = lax.scan(step, S0, xs)
    return jnp.moveaxis(ys, 0, 1), S


def _rwkv7(z, gate, xn, S0, xprev, zprev, mu_z, mu_w, mu_a, w0, w1, w2,
           a0, a1, a2, k_k, k_a, r_k, lnx_w, lnx_b):
    B, T, _ = z.shape
    zm = z + (_shift(z, zprev) - z) * mu_z
    r, k, v = jnp.split(zm, 3, axis=-1)
    dx = _shift(xn, xprev) - xn
    xw = xn + dx * mu_w
    xa = xn + dx * mu_a
    wl = -jax.nn.softplus(-(w0 + jnp.tanh(xw @ w1) @ w2).astype(jnp.float32)) - 0.5
    decay = jnp.exp(-jnp.exp(wl))
    a = jax.nn.sigmoid((a0 + (xa @ a1) @ a2).astype(jnp.float32))
    rf, kf, vf = (t.astype(jnp.float32) for t in (r, k, v))
    hs = lambda t: t.reshape(B, T, H_B, HEAD_B)
    kk = hs(kf * k_k)
    kk = kk / jnp.maximum(jnp.sqrt(jnp.sum(kk * kk, axis=-1, keepdims=True)), 1e-12)
    kf = kf * (1.0 + (a - 1.0) * k_a)
    y, S = _rwkv7_scan(hs(rf), hs(decay), hs(kf), hs(vf), kk, hs(a), S0.astype(jnp.float32))
    mean = jnp.mean(y, axis=-1, keepdims=True)
    var = jnp.mean(jnp.square(y - mean), axis=-1, keepdims=True)
    yn = ((y - mean) * lax.rsqrt(var + GN_EPS)).reshape(B, T, W_B) * lnx_w + lnx_b
    bonus = (jnp.sum(hs(rf) * hs(kf) * r_k, axis=-1, keepdims=True) * hs(vf)).reshape(B, T, W_B)
    out = (yn + bonus).astype(z.dtype) * jax.nn.silu(gate)
    return out, S.astype(z.dtype), xn[:, -1], z[:, -1]


def _hgrn2_chunked(q, logg, k, v, S0):
    B, T, H, DK = q.shape
    C = math.gcd(T, CHUNK_C)
    n = T // C
    to_chunks = lambda t: t.reshape(B, n, C, H, t.shape[-1]).transpose(1, 0, 3, 2, 4)
    mask = jnp.tril(jnp.ones((C, C), dtype=bool))

    def step(S, inp):
        qc, gc, kc, vc = inp
        b = jnp.cumsum(gc, axis=2)
        o_inter = jnp.einsum('bhtk,bhkv->bhtv', qc * jnp.exp(b), S)
        diff = b[:, :, :, None, :] - b[:, :, None, :, :]
        dec = jnp.where(mask[:, :, None], jnp.exp(jnp.minimum(diff, 0.0)), 0.0)
        att = jnp.einsum('bhtk,bhtsk,bhsk->bhts', qc, dec, kc)
        o = o_inter + jnp.einsum('bhts,bhsv->bhtv', att, vc)
        bl = b[:, :, -1:, :]
        S = jnp.exp(bl[:, :, 0])[..., None] * S + jnp.einsum('bhsk,bhsv->bhkv', kc * jnp.exp(bl - b), vc)
        return S, o

    S, o = lax.scan(step, S0, (to_chunks(q), to_chunks(logg), to_chunks(k), to_chunks(v)))
    o = o.transpose(1, 0, 3, 2, 4).reshape(B, T, H, v.shape[-1])
    return o, S


def _hgrn2(q, f, i, gate, S0, lb, norm_g):
    B, T, _ = q.shape
    ff = f.astype(jnp.float32)
    g_f = lb + (1.0 - lb) * jax.nn.sigmoid(ff)
    logg = jnp.log(jnp.maximum(g_f, TINY))
    kf = (1.0 - lb) * jax.nn.sigmoid(-ff)
    hs = lambda t, d: t.astype(jnp.float32).reshape(B, T, H_C, d)
    o, S = _hgrn2_chunked(hs(q, DK_C), hs(logg, DK_C), hs(kf, DK_C), hs(i, DV_C), S0.astype(jnp.float32))
    o = (_rms(o).reshape(B, T, W_C) * norm_g.astype(jnp.float32)).astype(q.dtype)
    return o * jax.nn.silu(gate), S.astype(q.dtype)


def _layer_stack(x, p, st_a_h, st_a_conv, st_b_S, st_b_x, st_b_z, st_c_S, W):
    lb_soft = jax.nn.softmax(W['hgrn_lb'].astype(jnp.float32), axis=0)
    lb_all = jnp.cumsum(lb_soft, axis=0) - lb_soft[0]
    idx = [W_A, 2 * W_A, 2 * W_A + 3 * W_B, 2 * W_A + 4 * W_B,
           2 * W_A + 4 * W_B + W_C, 2 * W_A + 4 * W_B + 2 * W_C, 2 * W_A + 4 * W_B + 3 * W_C]
    h = x
    na_h, na_c, nb_S, nb_x, nb_z, nc_S = [], [], [], [], [], []
    for l in range(DEPTH):
        xn = rmsnorm(h, W['g_pre'][l])
        z = xn @ W['w_in'][l]
        xa, ga, zb, gb, qc, fc, ic, gc = jnp.split(z, idx, axis=-1)
        oa, ha, ca = _rglru(xa, ga, st_a_h[l], st_a_conv[l], W['conv_a_w'][l], W['conv_a_b'][l],
                            W['lru_wr'][l], W['lru_br'][l], W['lru_wi'][l], W['lru_bi'][l],
                            W['lru_lambda'][l])
        ob, Sb, xb_last, zb_last = _rwkv7(
            zb, gb, xn, st_b_S[l], st_b_x[l], st_b_z[l], W['rwkv_mu_z'][l], W['rwkv_mu_w'][l],
            W['rwkv_mu_a'][l], W['rwkv_w0'][l], W['rwkv_w1'][l], W['rwkv_w2'][l], W['rwkv_a0'][l],
            W['rwkv_a1'][l], W['rwkv_a2'][l], W['rwkv_kk'][l], W['rwkv_ka'][l], W['rwkv_rk'][l],
            W['rwkv_lnx_w'][l], W['rwkv_lnx_b'][l])
        oc, Sc = _hgrn2(qc, fc, ic, gc, st_c_S[l], lb_all[l], W['hgrn_norm_g'][l])
        mix = jnp.concatenate([oa, ob, oc], axis=-1) @ W['w_out'][l]
        h = h + rmsnorm(mix, W['g_post'][l])
        u = p[l].astype(h.dtype) @ W['ple_proj'][l]
        g = jax.nn.sigmoid(_rms(h).astype(h.dtype) @ W['ple_gate'][l])
        h = h + rmsnorm(u * g, W['g_ple'][l])
        na_h.append(ha); na_c.append(ca); nb_S.append(Sb)
        nb_x.append(xb_last); nb_z.append(zb_last); nc_S.append(Sc)
    return h, (jnp.stack(na_h), jnp.stack(na_c), jnp.stack(nb_S),
               jnp.stack(nb_x), jnp.stack(nb_z), jnp.stack(nc_S))


def _zero_state(batch, dtype):
    return (jnp.zeros((DEPTH, batch, W_A), dtype),
            jnp.zeros((DEPTH, batch, CONV_W - 1, W_A), dtype),
            jnp.zeros((DEPTH, batch, H_B, HEAD_B, HEAD_B), dtype),
            jnp.zeros((DEPTH, batch, D_MODEL), dtype),
            jnp.zeros((DEPTH, batch, 3 * W_B), dtype),
            jnp.zeros((DEPTH, batch, H_C, DK_C, DV_C), dtype))


def setup_inputs(seed: int = 0) -> dict:
    key = jax.random.key(seed)
    ks = iter(jax.random.split(key, 48))
    f32 = jnp.float32
    nrm = lambda shape, s: jax.random.normal(next(ks), shape, f32) * s
    uni = lambda shape, lo, hi: jax.random.uniform(next(ks), shape, f32, lo, hi)
    a0 = uni((DEPTH, W_A), 0.9, 0.999)
    return {
        'x_prompt': nrm((BATCH, SEQ, D_MODEL), 1.0),
        'x_sample': nrm((DEC_BATCH, DEC_SEQ, D_MODEL), 1.0),
        'p_prompt': nrm((DEPTH, BATCH, SEQ, PLE_DIM), 1.0),
        'p_sample': nrm((DEPTH, DEC_BATCH, DEC_SEQ, PLE_DIM), 1.0),
        'state_a_h': nrm((DEPTH, DEC_BATCH, W_A), 0.5),
        'state_a_conv': nrm((DEPTH, DEC_BATCH, CONV_W - 1, W_A), 1.0),
        'state_b_S': nrm((DEPTH, DEC_BATCH, H_B, HEAD_B, HEAD_B), 0.5),
        'state_b_xprev': nrm((DEPTH, DEC_BATCH, D_MODEL), 1.0),
        'state_b_zprev': nrm((DEPTH, DEC_BATCH, 3 * W_B), 1.0),
        'state_c_S': nrm((DEPTH, DEC_BATCH, H_C, DK_C, DV_C), 0.5),
        'g_pre': 1.0 + nrm((DEPTH, D_MODEL), 0.02),
        'g_post': 1.0 + nrm((DEPTH, D_MODEL), 0.02),
        'w_in': nrm((DEPTH, D_MODEL, IN_W), D_MODEL ** -0.5),
        'w_out': nrm((DEPTH, MIX_W, D_MODEL), MIX_W ** -0.5),
        'conv_a_w': nrm((DEPTH, CONV_W, W_A), CONV_W ** -0.5),
        'conv_a_b': nrm((DEPTH, W_A), 0.01),
        'lru_wr': nrm((DEPTH, NB_A, BS_A, BS_A), BS_A ** -0.5),
        'lru_br': nrm((DEPTH, W_A), 0.01),
        'lru_wi': nrm((DEPTH, NB_A, BS_A, BS_A), BS_A ** -0.5),
        'lru_bi': nrm((DEPTH, W_A), 0.01),
        'lru_lambda': jnp.log(a0) - jnp.log1p(-a0),
        'rwkv_mu_z': uni((DEPTH, 3 * W_B), 0.0, 1.0),
        'rwkv_mu_w': uni((DEPTH, D_MODEL), 0.0, 1.0),
        'rwkv_mu_a': uni((DEPTH, D_MODEL), 0.0, 1.0),
        'rwkv_w0': uni((DEPTH, W_B), -5.0, 0.5),
        'rwkv_w1': nrm((DEPTH, D_MODEL, LORA_W), D_MODEL ** -0.5),
        'rwkv_w2': nrm((DEPTH, LORA_W, W_B), 0.3 * LORA_W ** -0.5),
        'rwkv_a0': nrm((DEPTH, W_B), 0.1),
        'rwkv_a1': nrm((DEPTH, D_MODEL, LORA_A), D_MODEL ** -0.5),
        'rwkv_a2': nrm((DEPTH, LORA_A, W_B), 0.3 * LORA_A ** -0.5),
        'rwkv_kk': 0.85 + nrm((DEPTH, W_B), 0.05),
        'rwkv_ka': 1.0 + nrm((DEPTH, W_B), 0.05),
        'rwkv_rk': nrm((DEPTH, H_B, HEAD_B), 0.1),
        'rwkv_lnx_w': 1.0 + nrm((DEPTH, W_B), 0.02),
        'rwkv_lnx_b': nrm((DEPTH, W_B), 0.01),
        'hgrn_lb': nrm((DEPTH, W_C), 1.0),
        'hgrn_norm_g': 1.0 + nrm((DEPTH, W_C), 0.02),
        'ple_proj': nrm((DEPTH, PLE_DIM, D_MODEL), PLE_DIM ** -0.5),
        'ple_gate': nrm((DEPTH, D_MODEL, D_MODEL), D_MODEL ** -0.5),
        'g_ple': 1.0 + nrm((DEPTH, D_MODEL), 0.02),
    }


def reference(x_prompt, x_sample, p_prompt, p_sample, state_a_h, state_a_conv, state_b_S,
              state_b_xprev, state_b_zprev, state_c_S, g_pre, g_post, w_in, w_out,
              conv_a_w, conv_a_b, lru_wr, lru_br, lru_wi, lru_bi, lru_lambda,
              rwkv_mu_z, rwkv_mu_w, rwkv_mu_a, rwkv_w0, rwkv_w1, rwkv_w2, rwkv_a0, rwkv_a1,
              rwkv_a2, rwkv_kk, rwkv_ka, rwkv_rk, rwkv_lnx_w, rwkv_lnx_b, hgrn_lb, hgrn_norm_g,
              ple_proj, ple_gate, g_ple):
    W = dict(g_pre=g_pre, g_post=g_post, w_in=w_in, w_out=w_out, conv_a_w=conv_a_w,
             conv_a_b=conv_a_b, lru_wr=lru_wr, lru_br=lru_br, lru_wi=lru_wi, lru_bi=lru_bi,
             lru_lambda=lru_lambda, rwkv_mu_z=rwkv_mu_z, rwkv_mu_w=rwkv_mu_w,
             rwkv_mu_a=rwkv_mu_a, rwkv_w0=rwkv_w0, rwkv_w1=rwkv_w1, rwkv_w2=rwkv_w2,
             rwkv_a0=rwkv_a0, rwkv_a1=rwkv_a1, rwkv_a2=rwkv_a2, rwkv_kk=rwkv_kk,
             rwkv_ka=rwkv_ka, rwkv_rk=rwkv_rk, rwkv_lnx_w=rwkv_lnx_w, rwkv_lnx_b=rwkv_lnx_b,
             hgrn_lb=hgrn_lb, hgrn_norm_g=hgrn_norm_g, ple_proj=ple_proj, ple_gate=ple_gate,
             g_ple=g_ple)
    z0 = _zero_state(x_prompt.shape[0], x_prompt.dtype)
    y_prompt, (pa_h, pa_c, pb_S, pb_x, pb_z, pc_S) = _layer_stack(
        x_prompt, p_prompt, z0[0], z0[1], z0[2], z0[3], z0[4], z0[5], W)
    y_sample, (sa_h, sa_c, sb_S, sb_x, sb_z, sc_S) = _layer_stack(
        x_sample, p_sample, state_a_h, state_a_conv, state_b_S, state_b_xprev,
        state_b_zprev, state_c_S, W)
    return (y_prompt, y_sample, pa_h, pa_c, pb_S, pb_x, pb_z, pc_S,
            sa_h, sa_c, sb_S, sb_x, sb_z, sc_S)
```

```python
import functools
import math

import jax
import jax.numpy as jnp
from jax import lax
from jax.experimental import pallas as pl
from jax.experimental.pallas import tpu as pltpu

F32 = jnp.float32
BF16 = jnp.bfloat16

RMS_EPS = 1e-6
GN_EPS = 64e-5
LRU_C = 8.0
TINY = 1e-30
KK_EPS = 1e-12
CONV_W = 4

LANES = 128
SUBLANES = 8
VMEM_LIMIT = 52 * 1024 * 1024

NT = (((1,), (1,)), ((), ()))
TN = (((0,), (0,)), ((), ()))


def _cparams(n_axes):
    return pltpu.CompilerParams(dimension_semantics=("arbitrary",) * n_axes,
                                vmem_limit_bytes=VMEM_LIMIT)


def _softplus(x):
    return jnp.maximum(x, 0.0) + jnp.log1p(jnp.exp(-jnp.abs(x)))


def _silu(x):
    return x * jax.nn.sigmoid(x)


def _bdot(a, b, dims=None):
    a = a.astype(BF16)
    b = b.astype(BF16)
    if dims is None:
        return jnp.dot(a, b, preferred_element_type=F32)
    return lax.dot_general(a, b, dims, preferred_element_type=F32)


def _cumsum_time(x, axis):
    n = x.shape[axis]
    idx = lax.broadcasted_iota(jnp.int32, x.shape, axis)
    d = 1
    while d < n:
        x = x + jnp.where(idx >= d, pltpu.roll(x, d, axis), 0.0)
        d *= 2
    return x


def _mm_body(x_ref, w_ref, o_ref, xb_ref):
    @pl.when(pl.program_id(1) == 0)
    def _():
        xb_ref[...] = x_ref[...].astype(BF16)

    o_ref[...] = jnp.dot(xb_ref[...], w_ref[...], preferred_element_type=F32)


def _matmul(x, w, tm=512, tn=512):
    m, k = x.shape
    n = w.shape[1]
    tm = min(tm, m)
    tn = min(tn, n)
    return pl.pallas_call(
        _mm_body,
        grid=(m // tm, n // tn),
        in_specs=[pl.BlockSpec((tm, k), lambda i, j: (i, 0)),
                  pl.BlockSpec((k, tn), lambda i, j: (0, j))],
        out_specs=pl.BlockSpec((tm, tn), lambda i, j: (i, j)),
        out_shape=jax.ShapeDtypeStruct((m, n), F32),
        scratch_shapes=[pltpu.VMEM((tm, k), BF16)],
        compiler_params=_cparams(2),
        name="matmul",
    )(x, w)


def _prenorm_body(h_ref, g_ref, xprev_ref, muw_ref, mua_ref, w1_ref, a1_ref, w2_ref, a2_ref,
                  w0_ref, a0_ref, xn_ref, lw_ref, a_ref, xlast_ref, carry_ref, *, last_row):
    j = pl.program_id(1)

    @pl.when(j == 0)
    def _():
        carry_ref[...] = xprev_ref[...]

    h = h_ref[...]
    sb, tt, d = h.shape
    xn = h * lax.rsqrt(jnp.mean(h * h, axis=-1, keepdims=True) + RMS_EPS) * g_ref[...]
    xn_ref[...] = xn
    t_idx = lax.broadcasted_iota(jnp.int32, (sb, tt, d), 1)
    prev = jnp.where(t_idx == 0, carry_ref[...], pltpu.roll(xn, 1, 1))
    carry_ref[...] = xn[:, tt - 1:tt, :]

    @pl.when(j == pl.num_programs(1) - 1)
    def _():
        xlast_ref[...] = xn[:, last_row:last_row + 1, :]

    dx = prev - xn
    xw = (xn + dx * muw_ref[...]).reshape(sb * tt, d)
    xa = (xn + dx * mua_ref[...]).reshape(sb * tt, d)
    yw = w0_ref[...] + _bdot(jnp.tanh(_bdot(xw, w1_ref[...])), w2_ref[...])
    wl = -_softplus(-yw) - 0.5
    lw_ref[...] = (-jnp.exp(wl)).reshape(lw_ref.shape)
    ya = a0_ref[...] + _bdot(_bdot(xa, a1_ref[...]), a2_ref[...])
    a_ref[...] = jax.nn.sigmoid(ya).reshape(a_ref.shape)


def _prenorm(h3, g, xprev, muw, mua, w1, a1, w2, a2, w0, a0, *, sb, tt, last_row):
    b, t, d = h3.shape
    wb = w2.shape[1]
    lw_dim = w1.shape[1]
    row3 = lambda i, j: (i, j, 0)
    par3 = lambda i, j: (0, 0, 0)
    par2 = lambda i, j: (0, 0)
    seq3 = lambda i, j: (i, 0, 0)
    return pl.pallas_call(
        functools.partial(_prenorm_body, last_row=last_row),
        grid=(b // sb, t // tt),
        in_specs=[pl.BlockSpec((sb, tt, d), row3),
                  pl.BlockSpec((1, 1, d), par3),
                  pl.BlockSpec((sb, 1, d), seq3),
                  pl.BlockSpec((1, 1, d), par3),
                  pl.BlockSpec((1, 1, d), par3),
                  pl.BlockSpec((d, lw_dim), par2),
                  pl.BlockSpec((d, lw_dim), par2),
                  pl.BlockSpec((lw_dim, wb), par2),
                  pl.BlockSpec((lw_dim, wb), par2),
                  pl.BlockSpec((1, wb), par2),
                  pl.BlockSpec((1, wb), par2)],
        out_specs=[pl.BlockSpec((sb, tt, d), row3),
                   pl.BlockSpec((sb, tt, wb), row3),
                   pl.BlockSpec((sb, tt, wb), row3),
                   pl.BlockSpec((sb, 1, d), seq3)],
        out_shape=[jax.ShapeDtypeStruct((b, t, d), F32),
                   jax.ShapeDtypeStruct((b, t, wb), F32),
                   jax.ShapeDtypeStruct((b, t, wb), F32),
                   jax.ShapeDtypeStruct((b, 1, d), F32)],
        scratch_shapes=[pltpu.VMEM((sb, 1, d), F32)],
        compiler_params=_cparams(2),
        name="prenorm_lora",
    )(h3, g, xprev, muw, mua, w1, a1, w2, a2, w0, a0)


def _rglru_body(x_ref, gt_ref, h0_ref, cb_ref, cw_ref, cbias_ref, wr_ref, br_ref, wi_ref, bi_ref,
                lam_ref, o_ref, hout_ref, tail_ref, hc_ref, tl_ref, *, nv_last):
    j = pl.program_id(1)

    @pl.when(j == 0)
    def _():
        hc_ref[...] = h0_ref[...]
        tl_ref[...] = cb_ref[...]

    x = x_ref[...]
    sb, tt, w = x.shape
    ext = jnp.concatenate([tl_ref[...], x], axis=1)
    tl_ref[...] = ext[:, tt:tt + SUBLANES, :]
    u = cbias_ref[...]
    for tap in range(CONV_W):
        sh = CONV_W - 1 - tap
        xs = x if sh == 0 else pltpu.roll(ext, sh, 1)[:, SUBLANES:, :]
        u = u + xs * cw_ref[:, tap:tap + 1, :]
    u2 = u.reshape(sb * tt, w)
    rp, ip = [], []
    for m in range(w // LANES):
        um = u2[:, m * LANES:(m + 1) * LANES]
        rp.append(_bdot(um, wr_ref[m]))
        ip.append(_bdot(um, wi_ref[m]))
    r = jax.nn.sigmoid(jnp.concatenate(rp, axis=1) + br_ref[...])
    i = jax.nn.sigmoid(jnp.concatenate(ip, axis=1) + bi_ref[...])
    log_a = -LRU_C * r * _softplus(-lam_ref[...])
    th = jnp.tanh(log_a)
    bcoef = jnp.sqrt(-2.0 * th / (1.0 - th)) * (i * u2)
    a_cum = jnp.exp(log_a).reshape(sb, tt, w)
    b_cum = bcoef.reshape(sb, tt, w)
    t_idx = lax.broadcasted_iota(jnp.int32, (sb, tt, w), 1)
    d = 1
    while d < tt:
        keep = t_idx >= d
        a_prev = jnp.where(keep, pltpu.roll(a_cum, d, 1), 1.0)
        b_prev = jnp.where(keep, pltpu.roll(b_cum, d, 1), 0.0)
        b_cum = a_cum * b_prev + b_cum
        a_cum = a_cum * a_prev
        d *= 2
    hs = a_cum * hc_ref[...] + b_cum
    hc_ref[...] = hs[:, tt - 1:tt, :]
    o_ref[...] = hs * _silu(gt_ref[...])

    @pl.when(j == pl.num_programs(1) - 1)
    def _():
        hout_ref[...] = hs[:, nv_last - 1:nv_last, :]
        if nv_last == tt:
            tail_ref[...] = ext[:, tt:tt + SUBLANES, :]
        else:
            tail_ref[...] = pltpu.roll(ext, tt + SUBLANES - nv_last, 1)[:, 0:SUBLANES, :]


def _rglru(z3, h0, cb8, cw, cbias, wr_bd, br, wi_bd, bi, lam, *, wa, sb, tt, nv_last):
    b, t, _ = z3.shape
    nt = wa // LANES
    par3 = lambda i, j: (0, 0, 0)
    seq3 = lambda i, j: (i, 0, 0)
    return pl.pallas_call(
        functools.partial(_rglru_body, nv_last=nv_last),
        grid=(b // sb, t // tt),
        in_specs=[pl.BlockSpec((sb, tt, wa), lambda i, j: (i, j, 0)),
                  pl.BlockSpec((sb, tt, wa), lambda i, j: (i, j, 1)),
                  pl.BlockSpec((sb, 1, wa), seq3),
                  pl.BlockSpec((sb, SUBLANES, wa), seq3),
                  pl.BlockSpec((1, CONV_W, wa), par3),
                  pl.BlockSpec((1, 1, wa), par3),
                  pl.BlockSpec((nt, LANES, LANES), par3),
                  pl.BlockSpec((1, wa), lambda i, j: (0, 0)),
                  pl.BlockSpec((nt, LANES, LANES), par3),
                  pl.BlockSpec((1, wa), lambda i, j: (0, 0)),
                  pl.BlockSpec((1, wa), lambda i, j: (0, 0))],
        out_specs=[pl.BlockSpec((sb, tt, wa), lambda i, j: (i, j, 0)),
                   pl.BlockSpec((sb, 1, wa), seq3),
                   pl.BlockSpec((sb, SUBLANES, wa), seq3)],
        out_shape=[jax.ShapeDtypeStruct((b, t, wa), F32),
                   jax.ShapeDtypeStruct((b, 1, wa), F32),
                   jax.ShapeDtypeStruct((b, SUBLANES, wa), F32)],
        scratch_shapes=[pltpu.VMEM((sb, 1, wa), F32), pltpu.VMEM((sb, SUBLANES, wa), F32)],
        compiler_params=_cparams(2),
        name="rglru",
    )(z3, z3, h0, cb8, cw, cbias, wr_bd, br, wi_bd, bi, lam)


def _rwkv_seq(zr, zk, zv, zg, lw, a, cr, ck, cv, s0, prm, n_valid, hd):
    mur, muk, muv, kkp, kap, rkp, lnw, lnb = prm
    c, hbw = zr.shape
    row = lax.broadcasted_iota(jnp.int32, (c, hbw), 0)
    first = row == 0

    def mix(z, carry, mu):
        return z + (jnp.where(first, carry, pltpu.roll(z, 1, 0)) - z) * mu

    r = mix(zr, cr, mur)
    k = mix(zk, ck, muk)
    v = mix(zv, cv, muv)
    kk_raw = k * kkp
    kmod = k * (1.0 + (a - 1.0) * kap)
    if n_valid < c:
        valid = row < n_valid
        lw = jnp.where(valid, lw, 0.0)
        kmod_s = jnp.where(valid, kmod, 0.0)
        v_s = jnp.where(valid, v, 0.0)
        a_s = jnp.where(valid, a, 0.0)
    else:
        kmod_s, v_s, a_s = kmod, v, a
    cl = _cumsum_time(lw, 0)
    e_inc = jnp.exp(cl)
    e_exc = jnp.exp(cl - lw)
    e_neg = jnp.exp(-cl)
    cl_last = cl[c - 1:c, :]
    e_end = jnp.exp(cl_last - cl)
    e_tot = jnp.exp(cl_last)

    ti = lax.broadcasted_iota(jnp.int32, (c, c), 0)
    si = lax.broadcasted_iota(jnp.int32, (c, c), 1)
    strict = ti > si
    lower = ti >= si

    outs, s_new = [], []
    for hh in range(hbw // hd):
        sl = slice(hh * hd, (hh + 1) * hd)
        kkh = kk_raw[:, sl]
        nrm = jnp.sqrt(jnp.sum(kkh * kkh, axis=-1, keepdims=True))
        kkh = kkh / jnp.maximum(nrm, KK_EPS)
        bh = kkh * a_s[:, sl]
        kmh = kmod_s[:, sl]
        vh = v_s[:, sl]
        rh = r[:, sl]
        lhs = jnp.concatenate([kkh * e_exc[:, sl], rh * e_inc[:, sl]], axis=0)
        b_n = bh * e_neg[:, sl]
        k_n = kmh * e_neg[:, sl]
        ab = _bdot(lhs, b_n, NT)
        ak = _bdot(lhs, k_n, NT)
        a_kb = jnp.where(strict, ab[:c], 0.0)
        a_kk = jnp.where(strict, ak[:c], 0.0)
        a_rb = jnp.where(lower, ab[c:], 0.0)
        a_rk = jnp.where(lower, ak[c:], 0.0)
        su = _bdot(lhs, s0[hh], NT)
        x = su[:c] + _bdot(a_kk, vh)
        lp = a_kb
        x = x - _bdot(lp, x)
        p = 2
        while p < c:
            lp = _bdot(lp, lp)
            x = x + _bdot(lp, x)
            p *= 2
        y = su[c:] + _bdot(a_rk, vh) - _bdot(a_rb, x)
        s_new.append(s0[hh] * e_tot[:, sl] + _bdot(vh, kmh * e_end[:, sl], TN)
                     - _bdot(x, bh * e_end[:, sl], TN))
        mean = jnp.mean(y, axis=-1, keepdims=True)
        var = jnp.mean(jnp.square(y - mean), axis=-1, keepdims=True)
        yn = (y - mean) * lax.rsqrt(var + GN_EPS) * lnw[:, sl] + lnb[:, sl]
        bonus = jnp.sum(rh * kmod[:, sl] * rkp[:, sl], axis=-1, keepdims=True) * v[:, sl]
        outs.append((yn + bonus) * _silu(zg[:, sl]))
    return jnp.concatenate(outs, axis=1), s_new


def _rwkv_body(zr_ref, zk_ref, zv_ref, zg_ref, lw_ref, a_ref, pr_ref, pk_ref, pv_ref, s0_ref,
               mur_ref, muk_ref, muv_ref, kk_ref, ka_ref, rk_ref, lnw_ref, lnb_ref,
               o_ref, sout_ref, s_scr, cr_scr, ck_scr, cv_scr, *, n_valid, hd):
    j = pl.program_id(2)

    @pl.when(j == 0)
    def _():
        s_scr[...] = s0_ref[...]
        cr_scr[...] = pr_ref[...]
        ck_scr[...] = pk_ref[...]
        cv_scr[...] = pv_ref[...]

    sb, c, hbw = zr_ref.shape
    nh = hbw // hd
    prm = (mur_ref[...], muk_ref[...], muv_ref[...], kk_ref[...], ka_ref[...], rk_ref[...],
           lnw_ref[...], lnb_ref[...])

    def one(s, carry):
        zr, zk, zv = zr_ref[s], zk_ref[s], zv_ref[s]
        out, s_new = _rwkv_seq(zr, zk, zv, zg_ref[s], lw_ref[s], a_ref[s],
                               cr_scr[s], ck_scr[s], cv_scr[s],
                               [s_scr[s, hh] for hh in range(nh)], prm, n_valid, hd)
        o_ref[s] = out
        for hh in range(nh):
            s_scr[s, hh] = s_new[hh]
        cr_scr[s] = zr[c - 1:c, :]
        ck_scr[s] = zk[c - 1:c, :]
        cv_scr[s] = zv[c - 1:c, :]
        return carry

    if sb == 1:
        one(0, 0)
    else:
        lax.fori_loop(0, sb, one, 0)

    @pl.when(j == pl.num_programs(2) - 1)
    def _():
        sout_ref[...] = s_scr[...]


def _rwkv(z3, lw, a, zprev, s0, muz, kkp, kap, rkp, lnw, lnb, *, wa, wb, hbw, sb, c, n_valid):
    b, t, _ = z3.shape
    hd = s0.shape[-1]
    nb = wb // hbw
    nh = hbw // hd
    off = 2 * wa // hbw
    zspec = lambda o: pl.BlockSpec((sb, c, hbw), lambda i, h, j, o=o: (i, j, o + h))
    pspec = lambda o: pl.BlockSpec((sb, 1, hbw), lambda i, h, j, o=o: (i, 0, o + h))
    mspec = lambda o: pl.BlockSpec((1, hbw), lambda i, h, j, o=o: (0, o + h))
    act = pl.BlockSpec((sb, c, hbw), lambda i, h, j: (i, j, h))
    st = pl.BlockSpec((sb, nh, hd, hd), lambda i, h, j: (i, h, 0, 0))
    return pl.pallas_call(
        functools.partial(_rwkv_body, n_valid=n_valid, hd=hd),
        grid=(b // sb, nb, t // c),
        in_specs=[zspec(off), zspec(off + nb), zspec(off + 2 * nb), zspec(off + 3 * nb), act, act,
                  pspec(0), pspec(nb), pspec(2 * nb), st,
                  mspec(0), mspec(nb), mspec(2 * nb), mspec(0), mspec(0), mspec(0), mspec(0), mspec(0)],
        out_specs=[act, st],
        out_shape=[jax.ShapeDtypeStruct((b, t, wb), F32), jax.ShapeDtypeStruct(s0.shape, F32)],
        scratch_shapes=[pltpu.VMEM((sb, nh, hd, hd), F32), pltpu.VMEM((sb, 1, hbw), F32),
                        pltpu.VMEM((sb, 1, hbw), F32), pltpu.VMEM((sb, 1, hbw), F32)],
        compiler_params=_cparams(3),
        name="rwkv7",
    )(z3, z3, z3, z3, lw, a, zprev, zprev, zprev, s0, muz, muz, muz, kkp, kap, rkp, lnw, lnb)


def _hgrn_head(q, f, v, g, s0, lb, ng, n_valid, sub):
    c, dk = q.shape
    sig = jax.nn.sigmoid(f)
    logg = jnp.log(jnp.maximum(lb + (1.0 - lb) * sig, TINY))
    kf = (1.0 - lb) * jax.nn.sigmoid(-f)
    if n_valid < c:
        valid = lax.broadcasted_iota(jnp.int32, (c, dk), 0) < n_valid
        logg = jnp.where(valid, logg, 0.0)
        kf = jnp.where(valid, kf, 0.0)
    bc = _cumsum_time(logg, 0)
    o_inter = _bdot(q * jnp.exp(bc), s0)
    t_sub = lax.broadcasted_iota(jnp.int32, (sub, 1), 0)
    blocks = []
    for bi in range(c // sub):
        lo = bi * sub
        qb, bb, kb, vb = q[lo:lo + sub], bc[lo:lo + sub], kf[lo:lo + sub], v[lo:lo + sub]
        ob = o_inter[lo:lo + sub]
        if bi > 0:
            bs = bc[lo - 1:lo]
            att = _bdot(qb * jnp.exp(bb - bs), kf[:lo] * jnp.exp(bs - bc[:lo]), NT)
            ob = ob + _bdot(att, v[:lo])
        for s2 in range(sub):
            dec = jnp.exp(jnp.minimum(bb - bb[s2:s2 + 1], 0.0))
            col = jnp.sum(qb * dec * kb[s2:s2 + 1], axis=-1, keepdims=True)
            ob = ob + jnp.where(t_sub >= s2, col, 0.0) * vb[s2:s2 + 1]
        blocks.append(ob)
    o = jnp.concatenate(blocks, axis=0) if len(blocks) > 1 else blocks[0]
    b_last = bc[c - 1:c]
    e_col = jnp.broadcast_to(jnp.exp(b_last), (v.shape[1], dk)).T
    s_new = e_col * s0 + _bdot(kf * jnp.exp(b_last - bc), v, TN)
    on = o * lax.rsqrt(jnp.mean(o * o, axis=-1, keepdims=True) + RMS_EPS) * ng
    return on * _silu(g), s_new


def _hgrn_body(q_ref, f_ref, i_ref, g_ref, s0_ref, lb_ref, ng_ref, o_ref, sout_ref, s_scr,
               *, n_valid, dk, sub):
    j = pl.program_id(2)

    @pl.when(j == 0)
    def _():
        s_scr[...] = s0_ref[...]

    sb, c, hcw = q_ref.shape
    nh = hcw // dk
    lb = lb_ref[...]
    ng = ng_ref[...]

    def one(s, carry):
        q, f, v, g = q_ref[s], f_ref[s], i_ref[s], g_ref[s]
        outs = []
        for hh in range(nh):
            sl = slice(hh * dk, (hh + 1) * dk)
            o, s_new = _hgrn_head(q[:, sl], f[:, sl], v[:, sl], g[:, sl], s_scr[s, hh],
                                  lb[:, sl], ng[:, sl], n_valid, sub)
            s_scr[s, hh] = s_new
            outs.append(o)
        o_ref[s] = jnp.concatenate(outs, axis=1) if nh > 1 else outs[0]
        return carry

    if sb == 1:
        one(0, 0)
    else:
        lax.fori_loop(0, sb, one, 0)

    @pl.when(j == pl.num_programs(2) - 1)
    def _():
        sout_ref[...] = s_scr[...]


def _hgrn(z3, s0, lb, ng, *, col0, wc, hcw, sb, c, n_valid):
    b, t, _ = z3.shape
    dk, dv = s0.shape[-2], s0.shape[-1]
    nb = wc // hcw
    nh = hcw // dk
    off = col0 // hcw
    sub = min(16, c)
    zspec = lambda o: pl.BlockSpec((sb, c, hcw), lambda i, h, j, o=o: (i, j, o + h))
    st = pl.BlockSpec((sb, nh, dk, dv), lambda i, h, j: (i, h, 0, 0))
    par = pl.BlockSpec((1, hcw), lambda i, h, j: (0, h))
    act = pl.BlockSpec((sb, c, hcw), lambda i, h, j: (i, j, h))
    return pl.pallas_call(
        functools.partial(_hgrn_body, n_valid=n_valid, dk=dk, sub=sub),
        grid=(b // sb, nb, t // c),
        in_specs=[zspec(off), zspec(off + nb), zspec(off + 2 * nb), zspec(off + 3 * nb), st, par, par],
        out_specs=[act, st],
        out_shape=[jax.ShapeDtypeStruct((b, t, wc), F32), jax.ShapeDtypeStruct(s0.shape, F32)],
        scratch_shapes=[pltpu.VMEM((sb, nh, dk, dv), F32)],
        compiler_params=_cparams(3),
        name="hgrn2",
    )(z3, z3, z3, z3, s0, lb, ng)


def _post1_body(h_ref, m_ref, g_ref, h1_ref, hn_ref):
    m = m_ref[...]
    h1 = h_ref[...] + m * lax.rsqrt(jnp.mean(m * m, axis=-1, keepdims=True) + RMS_EPS) * g_ref[...]
    h1_ref[...] = h1
    hn_ref[...] = h1 * lax.rsqrt(jnp.mean(h1 * h1, axis=-1, keepdims=True) + RMS_EPS)


def _post2_body(h_ref, u_ref, p_ref, g_ref, o_ref):
    x = u_ref[...] * jax.nn.sigmoid(p_ref[...])
    o_ref[...] = h_ref[...] + x * lax.rsqrt(jnp.mean(x * x, axis=-1, keepdims=True) + RMS_EPS) * g_ref[...]


def _rowwise(body, name, arrays, gain, n_out, tr=256):
    n, d = arrays[0].shape
    tr = min(tr, n)
    row = pl.BlockSpec((tr, d), lambda i: (i, 0))
    return pl.pallas_call(
        body,
        grid=(n // tr,),
        in_specs=[row] * len(arrays) + [pl.BlockSpec((1, d), lambda i: (0, 0))],
        out_specs=[row] * n_out if n_out > 1 else row,
        out_shape=[jax.ShapeDtypeStruct((n, d), F32)] * n_out if n_out > 1
        else jax.ShapeDtypeStruct((n, d), F32),
        compiler_params=_cparams(1),
        name=name,
    )(*arrays, gain)


def _block_diag_tiles(w):
    nb, bs, _ = w.shape
    per = LANES // bs
    w = w.reshape(nb // per, per, bs, bs)
    eye = jnp.eye(per, dtype=w.dtype)
    return jnp.einsum('tpab,pq->tpaqb', w, eye).reshape(nb // per, LANES, LANES)


def _layer_stack(x, p, st, W, *, n_valid, chunk, sb_rows, sb_seq):
    st_a_h, st_a_conv, st_b_s, st_b_x, st_b_z, st_c_s = st
    b, t, d = x.shape
    depth = W['w_in'].shape[0]
    wa = st_a_h.shape[-1]
    hd = st_b_s.shape[-1]
    wb = st_b_s.shape[-3] * hd
    dk = st_c_s.shape[-2]
    wc = st_c_s.shape[-3] * dk
    n = b * t
    tt = min(t, 128)
    last_row = (n_valid - 1) % tt
    nv_last = last_row + 1
    hbw = math.gcd(math.gcd(2 * wa, wb), 512)
    hcw = math.gcd(math.gcd(2 * wa + 4 * wb, wc), 512)

    lb_soft = jax.nn.softmax(W['hgrn_lb'].astype(F32), axis=0)
    lb_all = jnp.cumsum(lb_soft, axis=0) - lb_soft[0]

    h = x
    outs = [[] for _ in range(6)]
    for l in range(depth):
        xn, lw, a, xlast = _prenorm(
            h, W['g_pre'][l].reshape(1, 1, d), st_b_x[l][:, None, :],
            W['rwkv_mu_w'][l].reshape(1, 1, d), W['rwkv_mu_a'][l].reshape(1, 1, d),
            W['rwkv_w1'][l], W['rwkv_a1'][l], W['rwkv_w2'][l], W['rwkv_a2'][l],
            W['rwkv_w0'][l][None], W['rwkv_a0'][l][None], sb=sb_rows, tt=tt, last_row=last_row)
        z = _matmul(xn.reshape(n, d), W['w_in'][l])
        z3 = z.reshape(b, t, -1)
        cb8 = jnp.pad(st_a_conv[l], ((0, 0), (SUBLANES - (CONV_W - 1), 0), (0, 0)))
        oa, ha, tail = _rglru(
            z3, st_a_h[l][:, None, :], cb8, W['conv_a_w'][l][None], W['conv_a_b'][l].reshape(1, 1, wa),
            W['lru_wr_bd'][l], W['lru_br'][l][None], W['lru_wi_bd'][l], W['lru_bi'][l][None],
            W['lru_lambda'][l][None], wa=wa, sb=sb_rows, tt=tt, nv_last=nv_last)
        ob, sb_new = _rwkv(
            z3, lw, a, st_b_z[l][:, None, :], st_b_s[l], W['rwkv_mu_z'][l][None], W['rwkv_kk'][l][None],
            W['rwkv_ka'][l][None], W['rwkv_rk'][l].reshape(1, wb), W['rwkv_lnx_w'][l][None],
            W['rwkv_lnx_b'][l][None], wa=wa, wb=wb, hbw=hbw, sb=sb_seq, c=chunk, n_valid=n_valid if chunk > n_valid else chunk)
        oc, sc_new = _hgrn(
            z3, st_c_s[l], lb_all[l][None], W['hgrn_norm_g'][l][None],
            col0=2 * wa + 4 * wb, wc=wc, hcw=hcw, sb=sb_seq, c=chunk,
            n_valid=n_valid if chunk > n_valid else chunk)
        mix = jnp.concatenate([oa, ob, oc], axis=-1).reshape(n, -1)
        mm = _matmul(mix, W['w_out'][l])
        h1, hn = _rowwise(_post1_body, "post_mix", [h.reshape(n, d), mm], W['g_post'][l][None], 2)
        gp = _matmul(hn, W['ple_gate'][l])
        u = _matmul(p[l].reshape(n, -1), W['ple_proj'][l])
        h = _rowwise(_post2_body, "post_ple", [h1, u, gp], W['g_ple'][l][None], 1).reshape(b, t, d)
        outs[0].append(ha[:, 0])
        outs[1].append(tail[:, SUBLANES - (CONV_W - 1):])
        outs[2].append(sb_new)
        outs[3].append(xlast[:, 0])
        outs[4].append(z3[:, n_valid - 1, 2 * wa:2 * wa + 3 * wb])
        outs[5].append(sc_new)
    return h, tuple(jnp.stack(o) for o in outs)


def kernel(x_prompt, x_sample, p_prompt, p_sample, state_a_h, state_a_conv, state_b_S,
           state_b_xprev, state_b_zprev, state_c_S, g_pre, g_post, w_in, w_out,
           conv_a_w, conv_a_b, lru_wr, lru_br, lru_wi, lru_bi, lru_lambda,
           rwkv_mu_z, rwkv_mu_w, rwkv_mu_a, rwkv_w0, rwkv_w1, rwkv_w2, rwkv_a0, rwkv_a1,
           rwkv_a2, rwkv_kk, rwkv_ka, rwkv_rk, rwkv_lnx_w, rwkv_lnx_b, hgrn_lb, hgrn_norm_g,
           ple_proj, ple_gate, g_ple):
    depth = w_in.shape[0]
    W = dict(g_pre=g_pre, g_post=g_post, conv_a_w=conv_a_w, conv_a_b=conv_a_b,
             lru_br=lru_br, lru_bi=lru_bi, lru_lambda=lru_lambda, rwkv_mu_z=rwkv_mu_z,
             rwkv_mu_w=rwkv_mu_w, rwkv_mu_a=rwkv_mu_a, rwkv_w0=rwkv_w0, rwkv_a0=rwkv_a0,
             rwkv_kk=rwkv_kk, rwkv_ka=rwkv_ka, rwkv_rk=rwkv_rk, rwkv_lnx_w=rwkv_lnx_w,
             rwkv_lnx_b=rwkv_lnx_b, hgrn_lb=hgrn_lb, hgrn_norm_g=hgrn_norm_g, g_ple=g_ple)
    for name, w in (('w_in', w_in), ('w_out', w_out), ('rwkv_w1', rwkv_w1), ('rwkv_w2', rwkv_w2),
                    ('rwkv_a1', rwkv_a1), ('rwkv_a2', rwkv_a2), ('ple_proj', ple_proj),
                    ('ple_gate', ple_gate)):
        W[name] = w.astype(BF16)
    W['lru_wr_bd'] = jnp.stack([_block_diag_tiles(lru_wr[l]) for l in range(depth)]).astype(BF16)
    W['lru_wi_bd'] = jnp.stack([_block_diag_tiles(lru_wi[l]) for l in range(depth)]).astype(BF16)

    bp, tp, d = x_prompt.shape
    bs, ts, _ = x_sample.shape
    dt = x_prompt.dtype
    zero = lambda s: jnp.zeros((depth, bp) + s.shape[2:], dt)
    st_p = tuple(zero(s) for s in (state_a_h, state_a_conv, state_b_S, state_b_xprev,
                                   state_b_zprev, state_c_S))
    y_p, out_p = _layer_stack(x_prompt, p_prompt, st_p, W, n_valid=tp, chunk=64, sb_rows=1, sb_seq=1)

    tpad = -(-ts // SUBLANES) * SUBLANES
    pad_t = lambda v, ax: jnp.pad(v, [(0, tpad - ts) if i == ax else (0, 0) for i in range(v.ndim)])
    st_s = (state_a_h, state_a_conv, state_b_S, state_b_xprev, state_b_zprev, state_c_S)
    y_s, out_s = _layer_stack(pad_t(x_sample, 1), pad_t(p_sample, 2), st_s, W, n_valid=ts,
                              chunk=tpad, sb_rows=16, sb_seq=16)
    return (y_p, y_s[:, :ts]) + out_p + out_s
```

```python
import functools
import math
from typing import NamedTuple

import jax
import jax.numpy as jnp
from jax import lax
from jax.experimental import pallas as pl
from jax.experimental.pallas import tpu as pltpu

F32 = jnp.float32
BF16 = jnp.bfloat16

RMS_EPS = 1e-6
GN_EPS = 64e-5
LRU_C = 8.0
TINY = 1e-30
KK_EPS = 1e-12
CONV_W = 4

LANES = 128
SUBLANES = 8
BF16_ROWS = 16
VMEM_LIMIT = 52 * 1024 * 1024

NT = (((1,), (1,)), ((), ()))
TN = (((0,), (0,)), ((), ()))
BNT = (((2,), (2,)), ((0,), (0,)))
BNN = (((2,), (1,)), ((0,), (0,)))
BTN = (((1,), (1,)), ((0,), (0,)))


class Plan(NamedTuple):
    n_valid: int
    tt: int
    sb_rows: int
    chunk: int
    sb_rwkv: int
    sb_hgrn: int
    act_dtype: object


def _make_plan(b, t, n_valid):
    tt = min(t, 128)
    chunk = min(t, 64)
    rows = 128
    sb = max(1, min(b, rows // tt))
    packed_ok = chunk % BF16_ROWS == 0 and tt % BF16_ROWS == 0
    return Plan(n_valid=n_valid, tt=tt, sb_rows=sb, chunk=chunk,
                sb_rwkv=math.gcd(b, max(4, rows // chunk)), sb_hgrn=sb if chunk < 64 else 1,
                act_dtype=BF16 if packed_ok else F32)


def _cparams(n_axes):
    return pltpu.CompilerParams(dimension_semantics=("arbitrary",) * n_axes,
                                vmem_limit_bytes=VMEM_LIMIT)


def _softplus(x):
    return jnp.maximum(x, 0.0) + jnp.log1p(jnp.exp(-jnp.abs(x)))


def _silu(x):
    return x * jax.nn.sigmoid(x)


def _bdot(a, b, dims=None):
    a = a.astype(BF16)
    b = b.astype(BF16)
    if dims is None:
        return jnp.dot(a, b, preferred_element_type=F32)
    return lax.dot_general(a, b, dims, preferred_element_type=F32)


def _cumsum_time(x, axis):
    n = x.shape[axis]
    idx = lax.broadcasted_iota(jnp.int32, x.shape, axis)
    d = 1
    while d < n:
        x = x + jnp.where(idx >= d, pltpu.roll(x, d, axis), 0.0)
        d *= 2
    return x


_ANY = pl.BlockSpec(memory_space=pl.ANY)


def _mm_body(x_ref, w_ref, o_ref):
    o_ref[...] = jnp.dot(x_ref[...], w_ref[...], preferred_element_type=F32)


def _mm_cast_body(x_ref, w_ref, o_ref, xb_ref):
    @pl.when(pl.program_id(1) == 0)
    def _():
        xb_ref[...] = x_ref[...].astype(BF16)

    o_ref[...] = jnp.dot(xb_ref[...], w_ref[...], preferred_element_type=F32)


def _matmul(x, w_all, l, tn=512):
    m, k = x.shape
    n = w_all.shape[2]
    packed = x.dtype == BF16
    tm = min(1024 if packed else 512, m)
    tn = min(tn, n)
    return pl.pallas_call(
        _mm_body if packed else _mm_cast_body,
        grid=(m // tm, n // tn),
        in_specs=[pl.BlockSpec((tm, k), lambda i, j: (i, 0)),
                  pl.BlockSpec((None, k, tn), lambda i, j: (l, 0, j))],
        out_specs=pl.BlockSpec((tm, tn), lambda i, j: (i, j)),
        out_shape=jax.ShapeDtypeStruct((m, n), F32),
        scratch_shapes=[] if packed else [pltpu.VMEM((tm, k), BF16)],
        compiler_params=_cparams(2),
        name="matmul",
    )(x, w_all)


def _prenorm_body(h_ref, g_ref, xprev_ref, muw_ref, mua_ref, w1_ref, a1_ref, w2_ref, a2_ref,
                  w0_ref, a0_ref, xn_ref, lw_ref, a_ref, xlast_ref, carry_ref, *, last_row):
    j = pl.program_id(1)

    @pl.when(j == 0)
    def _():
        carry_ref[...] = xprev_ref[...]

    h = h_ref[...]
    sb, tt, d = h.shape
    xn = h * lax.rsqrt(jnp.mean(h * h, axis=-1, keepdims=True) + RMS_EPS) * g_ref[...]
    xn_ref[...] = xn.astype(xn_ref.dtype)
    t_idx = lax.broadcasted_iota(jnp.int32, (sb, tt, d), 1)
    prev = jnp.where(t_idx == 0, carry_ref[...], pltpu.roll(xn, 1, 1))
    carry_ref[...] = xn[:, tt - 1:tt, :]

    @pl.when(j == pl.num_programs(1) - 1)
    def _():
        xlast_ref[...] = xn[:, last_row:last_row + 1, :]

    dx = prev - xn
    xw = (xn + dx * muw_ref[...]).reshape(sb * tt, d)
    xa = (xn + dx * mua_ref[...]).reshape(sb * tt, d)
    yw = w0_ref[...] + _bdot(jnp.tanh(_bdot(xw, w1_ref[...])), w2_ref[...])
    wl = -_softplus(-yw) - 0.5
    lw_ref[...] = (-jnp.exp(wl)).reshape(lw_ref.shape)
    ya = a0_ref[...] + _bdot(_bdot(xa, a1_ref[...]), a2_ref[...])
    a_ref[...] = jax.nn.sigmoid(ya).reshape(a_ref.shape)


def _prenorm(h3, g, xprev, muw, mua, w1, a1, w2, a2, w0, a0, plan):
    b, t, d = h3.shape
    sb, tt = plan.sb_rows, plan.tt
    wb = w2.shape[1]
    lw_dim = w1.shape[1]
    row3 = lambda i, j: (i, j, 0)
    par3 = lambda i, j: (0, 0, 0)
    par2 = lambda i, j: (0, 0)
    seq3 = lambda i, j: (i, 0, 0)
    return pl.pallas_call(
        functools.partial(_prenorm_body, last_row=(plan.n_valid - 1) % tt),
        grid=(b // sb, t // tt),
        in_specs=[pl.BlockSpec((sb, tt, d), row3),
                  pl.BlockSpec((1, 1, d), par3),
                  pl.BlockSpec((sb, 1, d), seq3),
                  pl.BlockSpec((1, 1, d), par3),
                  pl.BlockSpec((1, 1, d), par3),
                  pl.BlockSpec((d, lw_dim), par2),
                  pl.BlockSpec((d, lw_dim), par2),
                  pl.BlockSpec((lw_dim, wb), par2),
                  pl.BlockSpec((lw_dim, wb), par2),
                  pl.BlockSpec((1, wb), par2),
                  pl.BlockSpec((1, wb), par2)],
        out_specs=[pl.BlockSpec((sb, tt, d), row3),
                   pl.BlockSpec((sb, tt, wb), row3),
                   pl.BlockSpec((sb, tt, wb), row3),
                   pl.BlockSpec((sb, 1, d), seq3)],
        out_shape=[jax.ShapeDtypeStruct((b, t, d), plan.act_dtype),
                   jax.ShapeDtypeStruct((b, t, wb), F32),
                   jax.ShapeDtypeStruct((b, t, wb), F32),
                   jax.ShapeDtypeStruct((b, 1, d), F32)],
        scratch_shapes=[pltpu.VMEM((sb, 1, d), F32)],
        compiler_params=_cparams(2),
        name="prenorm_lora",
    )(h3, g, xprev, muw, mua, w1, a1, w2, a2, w0, a0)


def _rglru_body(x_ref, gt_ref, h0_ref, cb_ref, cw_ref, cbias_ref, wr_ref, br_ref, wi_ref, bi_ref,
                lam_ref, o_ref, hout_ref, tail_ref, hc_ref, tl_ref, *, nv_last):
    j = pl.program_id(1)

    @pl.when(j == 0)
    def _():
        hc_ref[...] = h0_ref[...]
        tl_ref[...] = cb_ref[...]

    x = x_ref[...]
    sb, tt, w = x.shape
    ext = jnp.concatenate([tl_ref[...], x], axis=1)
    tl_ref[...] = ext[:, tt:tt + SUBLANES, :]
    u = cbias_ref[...]
    for tap in range(CONV_W):
        sh = CONV_W - 1 - tap
        xs = x if sh == 0 else pltpu.roll(ext, sh, 1)[:, SUBLANES:, :]
        u = u + xs * cw_ref[:, tap:tap + 1, :]
    u2 = u.reshape(sb * tt, w)
    rp, ip = [], []
    for m in range(w // LANES):
        um = u2[:, m * LANES:(m + 1) * LANES]
        rp.append(_bdot(um, wr_ref[m]))
        ip.append(_bdot(um, wi_ref[m]))
    r = jax.nn.sigmoid(jnp.concatenate(rp, axis=1) + br_ref[...])
    i = jax.nn.sigmoid(jnp.concatenate(ip, axis=1) + bi_ref[...])
    log_a = -LRU_C * r * _softplus(-lam_ref[...])
    th = jnp.tanh(log_a)
    bcoef = jnp.sqrt(-2.0 * th / (1.0 - th)) * (i * u2)
    a_cum = jnp.exp(log_a).reshape(sb, tt, w)
    b_cum = bcoef.reshape(sb, tt, w)
    t_idx = lax.broadcasted_iota(jnp.int32, (sb, tt, w), 1)
    d = 1
    while d < tt:
        keep = t_idx >= d
        a_prev = jnp.where(keep, pltpu.roll(a_cum, d, 1), 1.0)
        b_prev = jnp.where(keep, pltpu.roll(b_cum, d, 1), 0.0)
        b_cum = a_cum * b_prev + b_cum
        a_cum = a_cum * a_prev
        d *= 2
    hs = a_cum * hc_ref[...] + b_cum
    hc_ref[...] = hs[:, tt - 1:tt, :]
    o_ref[...] = (hs * _silu(gt_ref[...])).astype(o_ref.dtype)

    @pl.when(j == pl.num_programs(1) - 1)
    def _():
        hout_ref[...] = hs[:, nv_last - 1:nv_last, :]
        if nv_last == tt:
            tail_ref[...] = ext[:, tt:tt + SUBLANES, :]
        else:
            tail_ref[...] = pltpu.roll(ext, tt + SUBLANES - nv_last, 1)[:, 0:SUBLANES, :]


def _rglru(z3, h0, cb8, cw, cbias, wr_bd, br, wi_bd, bi, lam, *, wa, mix_w, plan):
    b, t, _ = z3.shape
    sb, tt = plan.sb_rows, plan.tt
    nt = wa // LANES
    par3 = lambda i, j: (0, 0, 0)
    seq3 = lambda i, j: (i, 0, 0)
    return pl.pallas_call(
        functools.partial(_rglru_body, nv_last=(plan.n_valid - 1) % tt + 1),
        grid=(b // sb, t // tt),
        in_specs=[pl.BlockSpec((sb, tt, wa), lambda i, j: (i, j, 0)),
                  pl.BlockSpec((sb, tt, wa), lambda i, j: (i, j, 1)),
                  pl.BlockSpec((sb, 1, wa), seq3),
                  pl.BlockSpec((sb, SUBLANES, wa), seq3),
                  pl.BlockSpec((1, CONV_W, wa), par3),
                  pl.BlockSpec((1, 1, wa), par3),
                  pl.BlockSpec((nt, LANES, LANES), par3),
                  pl.BlockSpec((1, wa), lambda i, j: (0, 0)),
                  pl.BlockSpec((nt, LANES, LANES), par3),
                  pl.BlockSpec((1, wa), lambda i, j: (0, 0)),
                  pl.BlockSpec((1, wa), lambda i, j: (0, 0))],
        out_specs=[pl.BlockSpec((sb, tt, wa), lambda i, j: (i, j, 0)),
                   pl.BlockSpec((sb, 1, wa), seq3),
                   pl.BlockSpec((sb, SUBLANES, wa), seq3)],
        out_shape=[jax.ShapeDtypeStruct((b, t, mix_w), plan.act_dtype),
                   jax.ShapeDtypeStruct((b, 1, wa), F32),
                   jax.ShapeDtypeStruct((b, SUBLANES, wa), F32)],
        scratch_shapes=[pltpu.VMEM((sb, 1, wa), F32), pltpu.VMEM((sb, SUBLANES, wa), F32)],
        compiler_params=_cparams(2),
        name="rglru",
    )(z3, z3, h0, cb8, cw, cbias, wr_bd, br, wi_bd, bi, lam)


def _split_heads(x, hd):
    sb, c, w = x.shape
    nh = w // hd
    st = jnp.stack([x[:, :, h * hd:(h + 1) * hd] for h in range(nh)], axis=1)
    return st.reshape(sb * nh, c, hd)


def _merge_heads(x, sb):
    n, c, hd = x.shape
    nh = n // sb
    x4 = x.reshape(sb, nh, c, hd)
    return jnp.concatenate([x4[:, h] for h in range(nh)], axis=-1)


_RWKV_INPUTS = 18


def _rwkv_body(*refs, n_valid, hd, n_alias):
    (zr_ref, zk_ref, zv_ref, zg_ref, lw_ref, a_ref, pr_ref, pk_ref, pv_ref, s0_ref,
     mur_ref, muk_ref, muv_ref, kk_ref, ka_ref, rk_ref, lnw_ref, lnb_ref) = refs[:_RWKV_INPUTS]
    o_ref, sout_ref, s_scr, cr_scr, ck_scr, cv_scr = refs[_RWKV_INPUTS + n_alias:]
    j = pl.program_id(2)

    @pl.when(j == 0)
    def _():
        s_scr[...] = s0_ref[...]
        cr_scr[...] = pr_ref[...]
        ck_scr[...] = pk_ref[...]
        cv_scr[...] = pv_ref[...]

    sb, c, hbw = zr_ref.shape
    nh = hbw // hd
    n = sb * nh
    row = lax.broadcasted_iota(jnp.int32, (sb, c, hbw), 1)
    first = row == 0

    def mix(z_ref, carry_ref, mu_ref):
        z = z_ref[...]
        zm = z + (jnp.where(first, carry_ref[...], pltpu.roll(z, 1, 1)) - z) * mu_ref[...]
        carry_ref[...] = z[:, c - 1:c, :]
        return zm

    r = mix(zr_ref, cr_scr, mur_ref)
    k = mix(zk_ref, ck_scr, muk_ref)
    v = mix(zv_ref, cv_scr, muv_ref)
    a = a_ref[...]
    lw = lw_ref[...]
    kk_raw = k * kk_ref[...]
    kmod = k * (1.0 + (a - 1.0) * ka_ref[...])
    if n_valid < c:
        valid = row < n_valid
        lw = jnp.where(valid, lw, 0.0)
        kmod = jnp.where(valid, kmod, 0.0)
        v = jnp.where(valid, v, 0.0)
        a = jnp.where(valid, a, 0.0)
    cl = _cumsum_time(lw, 1)

    per_head = lambda x: _split_heads(x, hd)
    par_head = lambda ref: jnp.concatenate([_split_heads(ref[...][None], hd)] * sb, axis=0)
    kkh, ah, kmh, vh, rh, clh, lwh = map(per_head, (kk_raw, a, kmod, v, r, cl, lw))
    kkh = kkh / jnp.maximum(jnp.sqrt(jnp.sum(kkh * kkh, axis=-1, keepdims=True)), KK_EPS)
    bh = kkh * ah
    cl_last = clh[:, c - 1:c, :]
    e_neg = jnp.exp(-clh)
    e_end = jnp.exp(cl_last - clh)
    lhs = jnp.concatenate([kkh * jnp.exp(clh - lwh), rh * jnp.exp(clh)], axis=1)
    s0 = s_scr[...].reshape(n, hd, hd)

    ti = lax.broadcasted_iota(jnp.int32, (c, c), 0)
    si = lax.broadcasted_iota(jnp.int32, (c, c), 1)
    strict = (ti > si)[None]
    lower = (ti >= si)[None]
    ab = _bdot(lhs, bh * e_neg, BNT)
    ak = _bdot(lhs, kmh * e_neg, BNT)
    su = _bdot(lhs, s0, BNT)
    x = su[:, :c] + _bdot(jnp.where(strict, ak[:, :c], 0.0), vh, BNN)
    lp = jnp.where(strict, ab[:, :c], 0.0)
    x = x - _bdot(lp, x, BNN)
    p = 2
    while p < c:
        lp = _bdot(lp, lp, BNN)
        x = x + _bdot(lp, x, BNN)
        p *= 2
    y = (su[:, c:] + _bdot(jnp.where(lower, ak[:, c:], 0.0), vh, BNN)
         - _bdot(jnp.where(lower, ab[:, c:], 0.0), x, BNN))
    s_new = s0 * jnp.exp(cl_last) + _bdot(vh, kmh * e_end, BTN) - _bdot(x, bh * e_end, BTN)
    s_scr[...] = s_new.reshape(s_scr.shape)

    mean = jnp.mean(y, axis=-1, keepdims=True)
    var = jnp.mean(jnp.square(y - mean), axis=-1, keepdims=True)
    yn = (y - mean) * lax.rsqrt(var + GN_EPS) * par_head(lnw_ref) + par_head(lnb_ref)
    bonus = jnp.sum(rh * kmh * par_head(rk_ref), axis=-1, keepdims=True) * vh
    o_ref[...] = (_merge_heads(yn + bonus, sb) * _silu(zg_ref[...])).astype(o_ref.dtype)

    @pl.when(j == pl.num_programs(2) - 1)
    def _():
        sout_ref[...] = s_new.reshape(sout_ref.shape)


def _rwkv(z3, lw, a, zprev, s0_all, l_in, mix, s_acc, l, depth, muz, kkp, kap, rkp, lnw, lnb,
          *, wa, wb, hbw, plan):
    b, t, _ = z3.shape
    sb, c = plan.sb_rwkv, plan.chunk
    hd = s0_all.shape[-1]
    nb = wb // hbw
    nh = hbw // hd
    off = 2 * wa // hbw
    zspec = lambda o: pl.BlockSpec((sb, c, hbw), lambda i, h, j, o=o: (i, j, o + h))
    pspec = lambda o: pl.BlockSpec((sb, 1, hbw), lambda i, h, j, o=o: (i, 0, o + h))
    mspec = lambda o: pl.BlockSpec((1, hbw), lambda i, h, j, o=o: (0, o + h))
    act = pl.BlockSpec((sb, c, hbw), lambda i, h, j: (i, j, h))
    st = lambda ll: pl.BlockSpec((None, sb, nh, hd, hd), lambda i, h, j, ll=ll: (ll, i, h, 0, 0))
    aliased = [mix] + ([] if s_acc is None else [s_acc])
    n_in = _RWKV_INPUTS
    aliases = {n_in: 0} if s_acc is None else {n_in: 0, n_in + 1: 1}
    out = pl.pallas_call(
        functools.partial(_rwkv_body, n_valid=min(plan.n_valid, c), hd=hd, n_alias=len(aliased)),
        grid=(b // sb, nb, t // c),
        in_specs=[zspec(off), zspec(off + nb), zspec(off + 2 * nb), zspec(off + 3 * nb), act, act,
                  pspec(0), pspec(nb), pspec(2 * nb), st(l_in),
                  mspec(0), mspec(nb), mspec(2 * nb), mspec(0), mspec(0), mspec(0), mspec(0), mspec(0)]
        + [_ANY] * len(aliased),
        out_specs=[pl.BlockSpec((sb, c, hbw), lambda i, h, j: (i, j, wa // hbw + h)), st(l)],
        out_shape=[jax.ShapeDtypeStruct(mix.shape, mix.dtype),
                   jax.ShapeDtypeStruct((depth,) + s0_all.shape[1:], F32)],
        input_output_aliases=aliases,
        scratch_shapes=[pltpu.VMEM((sb, nh, hd, hd), F32), pltpu.VMEM((sb, 1, hbw), F32),
                        pltpu.VMEM((sb, 1, hbw), F32), pltpu.VMEM((sb, 1, hbw), F32)],
        compiler_params=_cparams(3),
        name="rwkv7",
    )(z3, z3, z3, z3, lw, a, zprev, zprev, zprev, s0_all, muz, muz, muz, kkp, kap, rkp, lnw, lnb,
      *aliased)
    return out


def _hgrn_head(q, f, v, g, s0, lb, ng, n_valid, sub):
    c, dk = q.shape
    sig = jax.nn.sigmoid(f)
    logg = jnp.log(jnp.maximum(lb + (1.0 - lb) * sig, TINY))
    kf = (1.0 - lb) * jax.nn.sigmoid(-f)
    if n_valid < c:
        valid = lax.broadcasted_iota(jnp.int32, (c, dk), 0) < n_valid
        logg = jnp.where(valid, logg, 0.0)
        kf = jnp.where(valid, kf, 0.0)
    bc = _cumsum_time(logg, 0)
    o_inter = _bdot(q * jnp.exp(bc), s0)
    t_sub = lax.broadcasted_iota(jnp.int32, (sub, 1), 0)
    blocks = []
    for bi in range(c // sub):
        lo = bi * sub
        qb, bb, kb, vb = q[lo:lo + sub], bc[lo:lo + sub], kf[lo:lo + sub], v[lo:lo + sub]
        ob = o_inter[lo:lo + sub]
        if bi > 0:
            bs = bc[lo - 1:lo]
            att = _bdot(qb * jnp.exp(bb - bs), kf[:lo] * jnp.exp(bs - bc[:lo]), NT)
            ob = ob + _bdot(att, v[:lo])
        for s2 in range(sub):
            dec = jnp.exp(jnp.minimum(bb - bb[s2:s2 + 1], 0.0))
            col = jnp.sum(qb * dec * kb[s2:s2 + 1], axis=-1, keepdims=True)
            ob = ob + jnp.where(t_sub >= s2, col, 0.0) * vb[s2:s2 + 1]
        blocks.append(ob)
    o = jnp.concatenate(blocks, axis=0) if len(blocks) > 1 else blocks[0]
    b_last = bc[c - 1:c]
    e_col = jnp.broadcast_to(jnp.exp(b_last), (v.shape[1], dk)).T
    s_new = e_col * s0 + _bdot(kf * jnp.exp(b_last - bc), v, TN)
    on = o * lax.rsqrt(jnp.mean(o * o, axis=-1, keepdims=True) + RMS_EPS) * ng
    return on * _silu(g), s_new


_HGRN_INPUTS = 7


def _hgrn_body(*refs, n_valid, dk, sub, n_alias):
    q_ref, f_ref, i_ref, g_ref, s0_ref, lb_ref, ng_ref = refs[:_HGRN_INPUTS]
    o_ref, sout_ref, s_scr = refs[_HGRN_INPUTS + n_alias:]
    j = pl.program_id(2)

    @pl.when(j == 0)
    def _():
        s_scr[...] = s0_ref[...]

    sb, c, hcw = q_ref.shape
    nh = hcw // dk
    lb = lb_ref[...]
    ng = ng_ref[...]

    def one(s, carry):
        q, f, v, g = q_ref[s], f_ref[s], i_ref[s], g_ref[s]
        outs = []
        for hh in range(nh):
            sl = slice(hh * dk, (hh + 1) * dk)
            o, s_new = _hgrn_head(q[:, sl], f[:, sl], v[:, sl], g[:, sl], s_scr[s, hh],
                                  lb[:, sl], ng[:, sl], n_valid, sub)
            s_scr[s, hh] = s_new
            outs.append(o)
        o_ref[s] = (jnp.concatenate(outs, axis=1) if nh > 1 else outs[0]).astype(o_ref.dtype)
        return carry

    if sb == 1:
        one(0, 0)
    else:
        lax.fori_loop(0, sb, one, 0)

    @pl.when(j == pl.num_programs(2) - 1)
    def _():
        sout_ref[...] = s_scr[...]


def _hgrn(z3, s0_all, l_in, mix, s_acc, l, depth, lb, ng, *, col0, out_col0, wc, hcw, plan):
    b, t, _ = z3.shape
    sb, c = plan.sb_hgrn, plan.chunk
    dk, dv = s0_all.shape[-2], s0_all.shape[-1]
    nb = wc // hcw
    nh = hcw // dk
    off = col0 // hcw
    zspec = lambda o: pl.BlockSpec((sb, c, hcw), lambda i, h, j, o=o: (i, j, o + h))
    st = lambda ll: pl.BlockSpec((None, sb, nh, dk, dv), lambda i, h, j, ll=ll: (ll, i, h, 0, 0))
    par = pl.BlockSpec((1, hcw), lambda i, h, j: (0, h))
    aliased = [mix] + ([] if s_acc is None else [s_acc])
    n_in = _HGRN_INPUTS
    aliases = {n_in: 0} if s_acc is None else {n_in: 0, n_in + 1: 1}
    return pl.pallas_call(
        functools.partial(_hgrn_body, n_valid=min(plan.n_valid, c), dk=dk, sub=min(16, c),
                          n_alias=len(aliased)),
        grid=(b // sb, nb, t // c),
        in_specs=[zspec(off), zspec(off + nb), zspec(off + 2 * nb), zspec(off + 3 * nb), st(l_in),
                  par, par] + [_ANY] * len(aliased),
        out_specs=[pl.BlockSpec((sb, c, hcw), lambda i, h, j: (i, j, out_col0 // hcw + h)), st(l)],
        out_shape=[jax.ShapeDtypeStruct(mix.shape, mix.dtype),
                   jax.ShapeDtypeStruct((depth,) + s0_all.shape[1:], F32)],
        input_output_aliases=aliases,
        scratch_shapes=[pltpu.VMEM((sb, nh, dk, dv), F32)],
        compiler_params=_cparams(3),
        name="hgrn2",
    )(z3, z3, z3, z3, s0_all, lb, ng, *aliased)


def _post1_body(h_ref, m_ref, g_ref, h1_ref, hn_ref):
    m = m_ref[...]
    h1 = h_ref[...] + m * lax.rsqrt(jnp.mean(m * m, axis=-1, keepdims=True) + RMS_EPS) * g_ref[...]
    h1_ref[...] = h1
    hn_ref[...] = (h1 * lax.rsqrt(jnp.mean(h1 * h1, axis=-1, keepdims=True) + RMS_EPS)).astype(hn_ref.dtype)


def _post2_body(h_ref, u_ref, p_ref, g_ref, o_ref):
    x = u_ref[...] * jax.nn.sigmoid(p_ref[...])
    o_ref[...] = h_ref[...] + x * lax.rsqrt(jnp.mean(x * x, axis=-1, keepdims=True) + RMS_EPS) * g_ref[...]


def _rowwise(body, name, arrays, gain, out_dtypes, tr=256):
    n, d = arrays[0].shape
    tr = min(tr, n)
    row = pl.BlockSpec((tr, d), lambda i: (i, 0))
    return pl.pallas_call(
        body,
        grid=(n // tr,),
        in_specs=[row] * len(arrays) + [pl.BlockSpec((1, d), lambda i: (0, 0))],
        out_specs=[row] * len(out_dtypes),
        out_shape=[jax.ShapeDtypeStruct((n, d), dt) for dt in out_dtypes],
        compiler_params=_cparams(1),
        name=name,
    )(*arrays, gain)


def _block_diag_tiles(w):
    nb, bs, _ = w.shape
    per = LANES // bs
    w = w.reshape(nb // per, per, bs, bs)
    eye = jnp.eye(per, dtype=w.dtype)
    return jnp.einsum('tpab,pq->tpaqb', w, eye).reshape(nb // per, LANES, LANES)


def _layer_stack(x, p, st, W, plan, zero_state):
    st_a_h, st_a_conv, st_b_s, st_b_x, st_b_z, st_c_s = st
    b, t, d = x.shape
    depth = W['w_in'].shape[0]
    wa = st_a_h.shape[-1]
    hd = st_b_s.shape[-1]
    wb = st_b_s.shape[-3] * hd
    dk = st_c_s.shape[-2]
    wc = st_c_s.shape[-3] * dk
    mix_w = wa + wb + wc
    n = b * t
    nv = plan.n_valid
    hbw = math.gcd(math.gcd(2 * wa, wb), math.gcd(wa, 512))
    hcw = math.gcd(math.gcd(2 * wa + 4 * wb, wc), math.gcd(wa + wb, 512))

    lb_soft = jax.nn.softmax(W['hgrn_lb'].astype(F32), axis=0)
    lb_all = jnp.cumsum(lb_soft, axis=0) - lb_soft[0]
    p_act = p.astype(BF16)

    h = x
    small = [[] for _ in range(4)]
    sb_acc = sc_acc = None
    for l in range(depth):
        ls = 0 if zero_state else l
        xn, lw, a, xlast = _prenorm(
            h, W['g_pre'][l].reshape(1, 1, d), st_b_x[ls][:, None, :],
            W['rwkv_mu_w'][l].reshape(1, 1, d), W['rwkv_mu_a'][l].reshape(1, 1, d),
            W['rwkv_w1'][l], W['rwkv_a1'][l], W['rwkv_w2'][l], W['rwkv_a2'][l],
            W['rwkv_w0'][l][None], W['rwkv_a0'][l][None], plan)
        z3 = _matmul(xn.reshape(n, d), W['w_in'], l).reshape(b, t, -1)
        cb8 = jnp.pad(st_a_conv[ls], ((0, 0), (SUBLANES - (CONV_W - 1), 0), (0, 0)))
        mix, ha, tail = _rglru(
            z3, st_a_h[ls][:, None, :], cb8, W['conv_a_w'][l][None], W['conv_a_b'][l].reshape(1, 1, wa),
            W['lru_wr_bd'][l], W['lru_br'][l][None], W['lru_wi_bd'][l], W['lru_bi'][l][None],
            W['lru_lambda'][l][None], wa=wa, mix_w=mix_w, plan=plan)
        mix, sb_acc = _rwkv(
            z3, lw, a, st_b_z[ls][:, None, :], st_b_s, ls, mix, sb_acc, l, depth,
            W['rwkv_mu_z'][l][None], W['rwkv_kk'][l][None], W['rwkv_ka'][l][None],
            W['rwkv_rk'][l].reshape(1, wb), W['rwkv_lnx_w'][l][None], W['rwkv_lnx_b'][l][None],
            wa=wa, wb=wb, hbw=hbw, plan=plan)
        mix, sc_acc = _hgrn(
            z3, st_c_s, ls, mix, sc_acc, l, depth, lb_all[l][None], W['hgrn_norm_g'][l][None],
            col0=2 * wa + 4 * wb, out_col0=wa + wb, wc=wc, hcw=hcw, plan=plan)
        mm = _matmul(mix.reshape(n, mix_w), W['w_out'], l)
        h1, hn = _rowwise(_post1_body, "post_mix", [h.reshape(n, d), mm], W['g_post'][l][None],
                          [F32, BF16])
        gp = _matmul(hn, W['ple_gate'], l)
        u = _matmul(p_act[l].reshape(n, -1), W['ple_proj'], l)
        h = _rowwise(_post2_body, "post_ple", [h1, u, gp], W['g_ple'][l][None], [F32])[0].reshape(b, t, d)
        small[0].append(ha[:, 0])
        small[1].append(tail[:, SUBLANES - (CONV_W - 1):])
        small[2].append(xlast[:, 0])
        small[3].append(z3[:, nv - 1, 2 * wa:2 * wa + 3 * wb])
    na_h, na_c, nb_x, nb_z = (jnp.stack(o) for o in small)
    return h, (na_h, na_c, sb_acc, nb_x, nb_z, sc_acc)


def kernel(x_prompt, x_sample, p_prompt, p_sample, state_a_h, state_a_conv, state_b_S,
           state_b_xprev, state_b_zprev, state_c_S, g_pre, g_post, w_in, w_out,
           conv_a_w, conv_a_b, lru_wr, lru_br, lru_wi, lru_bi, lru_lambda,
           rwkv_mu_z, rwkv_mu_w, rwkv_mu_a, rwkv_w0, rwkv_w1, rwkv_w2, rwkv_a0, rwkv_a1,
           rwkv_a2, rwkv_kk, rwkv_ka, rwkv_rk, rwkv_lnx_w, rwkv_lnx_b, hgrn_lb, hgrn_norm_g,
           ple_proj, ple_gate, g_ple):
    depth = w_in.shape[0]
    W = dict(g_pre=g_pre, g_post=g_post, conv_a_w=conv_a_w, conv_a_b=conv_a_b,
             lru_br=lru_br, lru_bi=lru_bi, lru_lambda=lru_lambda, rwkv_mu_z=rwkv_mu_z,
             rwkv_mu_w=rwkv_mu_w, rwkv_mu_a=rwkv_mu_a, rwkv_w0=rwkv_w0, rwkv_a0=rwkv_a0,
             rwkv_kk=rwkv_kk, rwkv_ka=rwkv_ka, rwkv_rk=rwkv_rk, rwkv_lnx_w=rwkv_lnx_w,
             rwkv_lnx_b=rwkv_lnx_b, hgrn_lb=hgrn_lb, hgrn_norm_g=hgrn_norm_g, g_ple=g_ple)
    for name, w in (('w_in', w_in), ('w_out', w_out), ('rwkv_w1', rwkv_w1), ('rwkv_w2', rwkv_w2),
                    ('rwkv_a1', rwkv_a1), ('rwkv_a2', rwkv_a2), ('ple_proj', ple_proj),
                    ('ple_gate', ple_gate)):
        W[name] = w.astype(BF16)
    W['lru_wr_bd'] = jnp.stack([_block_diag_tiles(lru_wr[l]) for l in range(depth)]).astype(BF16)
    W['lru_wi_bd'] = jnp.stack([_block_diag_tiles(lru_wi[l]) for l in range(depth)]).astype(BF16)

    bp, tp, d = x_prompt.shape
    bs, ts, _ = x_sample.shape
    dt = x_prompt.dtype
    st_s = (state_a_h, state_a_conv, state_b_S, state_b_xprev, state_b_zprev, state_c_S)
    st_p = tuple(jnp.zeros((1, bp) + s.shape[2:], dt) for s in st_s)
    y_p, out_p = _layer_stack(x_prompt, p_prompt, st_p, W, _make_plan(bp, tp, tp), True)

    tpad = -(-ts // SUBLANES) * SUBLANES
    pad_t = lambda v, ax: jnp.pad(v, [(0, tpad - ts) if i == ax else (0, 0) for i in range(v.ndim)])
    y_s, out_s = _layer_stack(pad_t(x_sample, 1), pad_t(p_sample, 2), st_s, W,
                              _make_plan(bs, tpad, ts), False)
    return (y_p, y_s[:, :ts]) + out_p + out_s
```

```python
import functools
import math
from typing import NamedTuple

import jax
import jax.numpy as jnp
from jax import lax
from jax.experimental import pallas as pl
from jax.experimental.pallas import tpu as pltpu

F32 = jnp.float32
BF16 = jnp.bfloat16

RMS_EPS = 1e-6
GN_EPS = 64e-5
LRU_C = 8.0
TINY = 1e-30
KK_EPS = 1e-12
CONV_W = 4

LANES = 128
SUBLANES = 8
BF16_ROWS = 16
VMEM_LIMIT = 52 * 1024 * 1024

NT = (((1,), (1,)), ((), ()))
TN = (((0,), (0,)), ((), ()))
BNT = (((2,), (2,)), ((0,), (0,)))
BNN = (((2,), (1,)), ((0,), (0,)))
BTN = (((1,), (1,)), ((0,), (0,)))


class Plan(NamedTuple):
    n_valid: int
    tt: int
    sb_rows: int
    chunk: int
    sb_rwkv: int
    sb_hgrn: int
    flat_acts: bool


def _make_plan(b, t, n_valid):
    tt = min(t, 128)
    chunk = min(t, 64)
    rows = 128
    sb = max(1, min(b, rows // tt))
    flat = chunk == t
    assert flat or (chunk % BF16_ROWS == 0 and tt % BF16_ROWS == 0)
    return Plan(n_valid=n_valid, tt=tt, sb_rows=sb, chunk=chunk,
                sb_rwkv=math.gcd(b, max(4, rows // chunk)), sb_hgrn=sb if chunk < 64 else 1,
                flat_acts=flat)


def _act_shape(plan, b, t, width):
    return jax.ShapeDtypeStruct((b * t, width) if plan.flat_acts else (b, t, width), BF16)


def _act_spec(plan, t, sb, rows, width, col, grid_rank):
    if grid_rank == 2:
        if plan.flat_acts:
            return pl.BlockSpec((sb * rows, width), lambda i, j: (i * (t // rows) + j, col(0)))
        return pl.BlockSpec((sb, rows, width), lambda i, j: (i, j, col(0)))
    if plan.flat_acts:
        return pl.BlockSpec((sb * rows, width), lambda i, h, j: (i * (t // rows) + j, col(h)))
    return pl.BlockSpec((sb, rows, width), lambda i, h, j: (i, j, col(h)))


def _cparams(n_axes):
    return pltpu.CompilerParams(dimension_semantics=("arbitrary",) * n_axes,
                                vmem_limit_bytes=VMEM_LIMIT)


def _softplus(x):
    return jnp.maximum(x, 0.0) + jnp.log1p(jnp.exp(-jnp.abs(x)))


def _silu(x):
    return x * jax.nn.sigmoid(x)


def _bdot(a, b, dims=None):
    a = a.astype(BF16)
    b = b.astype(BF16)
    if dims is None:
        return jnp.dot(a, b, preferred_element_type=F32)
    return lax.dot_general(a, b, dims, preferred_element_type=F32)


def _cumsum_time(x, axis):
    n = x.shape[axis]
    idx = lax.broadcasted_iota(jnp.int32, x.shape, axis)
    d = 1
    while d < n:
        x = x + jnp.where(idx >= d, pltpu.roll(x, d, axis), 0.0)
        d *= 2
    return x


_ANY = pl.BlockSpec(memory_space=pl.ANY)


def _mm_body(x_ref, w_ref, o_ref):
    o_ref[...] = jnp.dot(x_ref[...], w_ref[...], preferred_element_type=F32)


def _mm_cast_body(x_ref, w_ref, o_ref, xb_ref):
    @pl.when(pl.program_id(1) == 0)
    def _():
        xb_ref[...] = x_ref[...].astype(BF16)

    o_ref[...] = jnp.dot(xb_ref[...], w_ref[...], preferred_element_type=F32)


def _matmul(x, w_all, l, tn=512):
    m, k = x.shape
    n = w_all.shape[2]
    packed = x.dtype == BF16
    tm = min(1024 if packed else 512, m)
    tn = min(tn, n)
    return pl.pallas_call(
        _mm_body if packed else _mm_cast_body,
        grid=(m // tm, n // tn),
        in_specs=[pl.BlockSpec((tm, k), lambda i, j: (i, 0)),
                  pl.BlockSpec((None, k, tn), lambda i, j: (l, 0, j))],
        out_specs=pl.BlockSpec((tm, tn), lambda i, j: (i, j)),
        out_shape=jax.ShapeDtypeStruct((m, n), F32),
        scratch_shapes=[] if packed else [pltpu.VMEM((tm, k), BF16)],
        compiler_params=_cparams(2),
        name="matmul",
    )(x, w_all)


def _prenorm_body(h_ref, g_ref, xprev_ref, muw_ref, mua_ref, w1_ref, a1_ref, w2_ref, a2_ref,
                  w0_ref, a0_ref, xn_ref, lw_ref, a_ref, xlast_ref, carry_ref, *, last_row):
    j = pl.program_id(1)

    @pl.when(j == 0)
    def _():
        carry_ref[...] = xprev_ref[...]

    h = h_ref[...]
    sb, tt, d = h.shape
    xn = h * lax.rsqrt(jnp.mean(h * h, axis=-1, keepdims=True) + RMS_EPS) * g_ref[...]
    xn_ref[...] = xn.reshape(xn_ref.shape).astype(xn_ref.dtype)
    t_idx = lax.broadcasted_iota(jnp.int32, (sb, tt, d), 1)
    prev = jnp.where(t_idx == 0, carry_ref[...], pltpu.roll(xn, 1, 1))
    carry_ref[...] = xn[:, tt - 1:tt, :]

    @pl.when(j == pl.num_programs(1) - 1)
    def _():
        xlast_ref[...] = xn[:, last_row:last_row + 1, :]

    dx = prev - xn
    xw = (xn + dx * muw_ref[...]).reshape(sb * tt, d)
    xa = (xn + dx * mua_ref[...]).reshape(sb * tt, d)
    yw = w0_ref[...] + _bdot(jnp.tanh(_bdot(xw, w1_ref[...])), w2_ref[...])
    wl = -_softplus(-yw) - 0.5
    lw_ref[...] = (-jnp.exp(wl)).reshape(lw_ref.shape)
    ya = a0_ref[...] + _bdot(_bdot(xa, a1_ref[...]), a2_ref[...])
    a_ref[...] = jax.nn.sigmoid(ya).reshape(a_ref.shape)


def _prenorm(h3, g, xprev, muw, mua, w1, a1, w2, a2, w0, a0, plan):
    b, t, d = h3.shape
    sb, tt = plan.sb_rows, plan.tt
    wb = w2.shape[1]
    lw_dim = w1.shape[1]
    row3 = lambda i, j: (i, j, 0)
    par3 = lambda i, j: (0, 0, 0)
    par2 = lambda i, j: (0, 0)
    seq3 = lambda i, j: (i, 0, 0)
    return pl.pallas_call(
        functools.partial(_prenorm_body, last_row=(plan.n_valid - 1) % tt),
        grid=(b // sb, t // tt),
        in_specs=[pl.BlockSpec((sb, tt, d), row3),
                  pl.BlockSpec((1, 1, d), par3),
                  pl.BlockSpec((sb, 1, d), seq3),
                  pl.BlockSpec((1, 1, d), par3),
                  pl.BlockSpec((1, 1, d), par3),
                  pl.BlockSpec((d, lw_dim), par2),
                  pl.BlockSpec((d, lw_dim), par2),
                  pl.BlockSpec((lw_dim, wb), par2),
                  pl.BlockSpec((lw_dim, wb), par2),
                  pl.BlockSpec((1, wb), par2),
                  pl.BlockSpec((1, wb), par2)],
        out_specs=[_act_spec(plan, t, sb, tt, d, lambda h: 0, 2),
                   pl.BlockSpec((sb, tt, wb), row3),
                   pl.BlockSpec((sb, tt, wb), row3),
                   pl.BlockSpec((sb, 1, d), seq3)],
        out_shape=[_act_shape(plan, b, t, d),
                   jax.ShapeDtypeStruct((b, t, wb), F32),
                   jax.ShapeDtypeStruct((b, t, wb), F32),
                   jax.ShapeDtypeStruct((b, 1, d), F32)],
        scratch_shapes=[pltpu.VMEM((sb, 1, d), F32)],
        compiler_params=_cparams(2),
        name="prenorm_lora",
    )(h3, g, xprev, muw, mua, w1, a1, w2, a2, w0, a0)


def _rglru_body(x_ref, gt_ref, h0_ref, cb_ref, cw_ref, cbias_ref, wr_ref, br_ref, wi_ref, bi_ref,
                lam_ref, o_ref, hout_ref, tail_ref, hc_ref, tl_ref, *, nv_last):
    j = pl.program_id(1)

    @pl.when(j == 0)
    def _():
        hc_ref[...] = h0_ref[...]
        tl_ref[...] = cb_ref[...]

    x = x_ref[...]
    sb, tt, w = x.shape
    ext = jnp.concatenate([tl_ref[...], x], axis=1)
    tl_ref[...] = ext[:, tt:tt + SUBLANES, :]
    u = cbias_ref[...]
    for tap in range(CONV_W):
        sh = CONV_W - 1 - tap
        xs = x if sh == 0 else pltpu.roll(ext, sh, 1)[:, SUBLANES:, :]
        u = u + xs * cw_ref[:, tap:tap + 1, :]
    u2 = u.reshape(sb * tt, w)
    rp, ip = [], []
    for m in range(w // LANES):
        um = u2[:, m * LANES:(m + 1) * LANES]
        rp.append(_bdot(um, wr_ref[m]))
        ip.append(_bdot(um, wi_ref[m]))
    r = jax.nn.sigmoid(jnp.concatenate(rp, axis=1) + br_ref[...])
    i = jax.nn.sigmoid(jnp.concatenate(ip, axis=1) + bi_ref[...])
    log_a = -LRU_C * r * _softplus(-lam_ref[...])
    th = jnp.tanh(log_a)
    bcoef = jnp.sqrt(-2.0 * th / (1.0 - th)) * (i * u2)
    a_cum = jnp.exp(log_a).reshape(sb, tt, w)
    b_cum = bcoef.reshape(sb, tt, w)
    t_idx = lax.broadcasted_iota(jnp.int32, (sb, tt, w), 1)
    d = 1
    while d < tt:
        keep = t_idx >= d
        a_prev = jnp.where(keep, pltpu.roll(a_cum, d, 1), 1.0)
        b_prev = jnp.where(keep, pltpu.roll(b_cum, d, 1), 0.0)
        b_cum = a_cum * b_prev + b_cum
        a_cum = a_cum * a_prev
        d *= 2
    hs = a_cum * hc_ref[...] + b_cum
    hc_ref[...] = hs[:, tt - 1:tt, :]
    o_ref[...] = (hs * _silu(gt_ref[...])).reshape(o_ref.shape).astype(o_ref.dtype)

    @pl.when(j == pl.num_programs(1) - 1)
    def _():
        hout_ref[...] = hs[:, nv_last - 1:nv_last, :]
        if nv_last == tt:
            tail_ref[...] = ext[:, tt:tt + SUBLANES, :]
        else:
            tail_ref[...] = pltpu.roll(ext, tt + SUBLANES - nv_last, 1)[:, 0:SUBLANES, :]


def _rglru(z3, h0, cb8, cw, cbias, wr_bd, br, wi_bd, bi, lam, *, wa, mix_w, plan):
    b, t, _ = z3.shape
    sb, tt = plan.sb_rows, plan.tt
    nt = wa // LANES
    par3 = lambda i, j: (0, 0, 0)
    seq3 = lambda i, j: (i, 0, 0)
    return pl.pallas_call(
        functools.partial(_rglru_body, nv_last=(plan.n_valid - 1) % tt + 1),
        grid=(b // sb, t // tt),
        in_specs=[pl.BlockSpec((sb, tt, wa), lambda i, j: (i, j, 0)),
                  pl.BlockSpec((sb, tt, wa), lambda i, j: (i, j, 1)),
                  pl.BlockSpec((sb, 1, wa), seq3),
                  pl.BlockSpec((sb, SUBLANES, wa), seq3),
                  pl.BlockSpec((1, CONV_W, wa), par3),
                  pl.BlockSpec((1, 1, wa), par3),
                  pl.BlockSpec((nt, LANES, LANES), par3),
                  pl.BlockSpec((1, wa), lambda i, j: (0, 0)),
                  pl.BlockSpec((nt, LANES, LANES), par3),
                  pl.BlockSpec((1, wa), lambda i, j: (0, 0)),
                  pl.BlockSpec((1, wa), lambda i, j: (0, 0))],
        out_specs=[_act_spec(plan, t, sb, tt, wa, lambda h: 0, 2),
                   pl.BlockSpec((sb, 1, wa), seq3),
                   pl.BlockSpec((sb, SUBLANES, wa), seq3)],
        out_shape=[_act_shape(plan, b, t, mix_w),
                   jax.ShapeDtypeStruct((b, 1, wa), F32),
                   jax.ShapeDtypeStruct((b, SUBLANES, wa), F32)],
        scratch_shapes=[pltpu.VMEM((sb, 1, wa), F32), pltpu.VMEM((sb, SUBLANES, wa), F32)],
        compiler_params=_cparams(2),
        name="rglru",
    )(z3, z3, h0, cb8, cw, cbias, wr_bd, br, wi_bd, bi, lam)


def _split_heads(x, hd):
    sb, c, w = x.shape
    nh = w // hd
    st = jnp.stack([x[:, :, h * hd:(h + 1) * hd] for h in range(nh)], axis=1)
    return st.reshape(sb * nh, c, hd)


def _merge_heads(x, sb):
    n, c, hd = x.shape
    nh = n // sb
    x4 = x.reshape(sb, nh, c, hd)
    return jnp.concatenate([x4[:, h] for h in range(nh)], axis=-1)


_RWKV_INPUTS = 18


def _rwkv_body(*refs, n_valid, hd, n_alias):
    (zr_ref, zk_ref, zv_ref, zg_ref, lw_ref, a_ref, pr_ref, pk_ref, pv_ref, s0_ref,
     mur_ref, muk_ref, muv_ref, kk_ref, ka_ref, rk_ref, lnw_ref, lnb_ref) = refs[:_RWKV_INPUTS]
    o_ref, sout_ref, s_scr, cr_scr, ck_scr, cv_scr = refs[_RWKV_INPUTS + n_alias:]
    j = pl.program_id(2)

    @pl.when(j == 0)
    def _():
        s_scr[...] = s0_ref[...]
        cr_scr[...] = pr_ref[...]
        ck_scr[...] = pk_ref[...]
        cv_scr[...] = pv_ref[...]

    sb, c, hbw = zr_ref.shape
    nh = hbw // hd
    n = sb * nh
    row = lax.broadcasted_iota(jnp.int32, (sb, c, hbw), 1)
    first = row == 0

    def mix(z_ref, carry_ref, mu_ref):
        z = z_ref[...]
        zm = z + (jnp.where(first, carry_ref[...], pltpu.roll(z, 1, 1)) - z) * mu_ref[...]
        carry_ref[...] = z[:, c - 1:c, :]
        return zm

    r = mix(zr_ref, cr_scr, mur_ref)
    k = mix(zk_ref, ck_scr, muk_ref)
    v = mix(zv_ref, cv_scr, muv_ref)
    a = a_ref[...]
    lw = lw_ref[...]
    kk_raw = k * kk_ref[...]
    kmod = k * (1.0 + (a - 1.0) * ka_ref[...])
    if n_valid < c:
        valid = row < n_valid
        lw = jnp.where(valid, lw, 0.0)
        kmod = jnp.where(valid, kmod, 0.0)
        v = jnp.where(valid, v, 0.0)
        a = jnp.where(valid, a, 0.0)
    cl = _cumsum_time(lw, 1)

    per_head = lambda x: _split_heads(x, hd)
    par_head = lambda ref: jnp.concatenate([_split_heads(ref[...][None], hd)] * sb, axis=0)
    kkh, ah, kmh, vh, rh, clh, lwh = map(per_head, (kk_raw, a, kmod, v, r, cl, lw))
    kkh = kkh / jnp.maximum(jnp.sqrt(jnp.sum(kkh * kkh, axis=-1, keepdims=True)), KK_EPS)
    bh = kkh * ah
    cl_last = clh[:, c - 1:c, :]
    e_neg = jnp.exp(-clh)
    e_end = jnp.exp(cl_last - clh)
    lhs = jnp.concatenate([kkh * jnp.exp(clh - lwh), rh * jnp.exp(clh)], axis=1)
    s0 = s_scr[...].reshape(n, hd, hd)

    ti = lax.broadcasted_iota(jnp.int32, (c, c), 0)
    si = lax.broadcasted_iota(jnp.int32, (c, c), 1)
    strict = (ti > si)[None]
    lower = (ti >= si)[None]
    ab = _bdot(lhs, bh * e_neg, BNT)
    ak = _bdot(lhs, kmh * e_neg, BNT)
    su = _bdot(lhs, s0, BNT)
    x = su[:, :c] + _bdot(jnp.where(strict, ak[:, :c], 0.0), vh, BNN)
    lp = jnp.where(strict, ab[:, :c], 0.0)
    x = x - _bdot(lp, x, BNN)
    p = 2
    while p < c:
        lp = _bdot(lp, lp, BNN)
        x = x + _bdot(lp, x, BNN)
        p *= 2
    y = (su[:, c:] + _bdot(jnp.where(lower, ak[:, c:], 0.0), vh, BNN)
         - _bdot(jnp.where(lower, ab[:, c:], 0.0), x, BNN))
    s_new = s0 * jnp.exp(cl_last) + _bdot(vh, kmh * e_end, BTN) - _bdot(x, bh * e_end, BTN)
    s_scr[...] = s_new.reshape(s_scr.shape)

    mean = jnp.mean(y, axis=-1, keepdims=True)
    var = jnp.mean(jnp.square(y - mean), axis=-1, keepdims=True)
    yn = (y - mean) * lax.rsqrt(var + GN_EPS) * par_head(lnw_ref) + par_head(lnb_ref)
    bonus = jnp.sum(rh * kmh * par_head(rk_ref), axis=-1, keepdims=True) * vh
    out = _merge_heads(yn + bonus, sb) * _silu(zg_ref[...])
    o_ref[...] = out.reshape(o_ref.shape).astype(o_ref.dtype)

    @pl.when(j == pl.num_programs(2) - 1)
    def _():
        sout_ref[...] = s_new.reshape(sout_ref.shape)


def _rwkv(z3, lw, a, zprev, s0_all, l_in, mix, s_acc, l, depth, muz, kkp, kap, rkp, lnw, lnb,
          *, wa, wb, hbw, plan):
    b, t, _ = z3.shape
    sb, c = plan.sb_rwkv, plan.chunk
    hd = s0_all.shape[-1]
    nb = wb // hbw
    nh = hbw // hd
    off = 2 * wa // hbw
    zspec = lambda o: pl.BlockSpec((sb, c, hbw), lambda i, h, j, o=o: (i, j, o + h))
    pspec = lambda o: pl.BlockSpec((sb, 1, hbw), lambda i, h, j, o=o: (i, 0, o + h))
    mspec = lambda o: pl.BlockSpec((1, hbw), lambda i, h, j, o=o: (0, o + h))
    act = pl.BlockSpec((sb, c, hbw), lambda i, h, j: (i, j, h))
    st = lambda ll: pl.BlockSpec((None, sb, nh, hd, hd), lambda i, h, j, ll=ll: (ll, i, h, 0, 0))
    aliased = [mix] + ([] if s_acc is None else [s_acc])
    n_in = _RWKV_INPUTS
    aliases = {n_in: 0} if s_acc is None else {n_in: 0, n_in + 1: 1}
    out = pl.pallas_call(
        functools.partial(_rwkv_body, n_valid=min(plan.n_valid, c), hd=hd, n_alias=len(aliased)),
        grid=(b // sb, nb, t // c),
        in_specs=[zspec(off), zspec(off + nb), zspec(off + 2 * nb), zspec(off + 3 * nb), act, act,
                  pspec(0), pspec(nb), pspec(2 * nb), st(l_in),
                  mspec(0), mspec(nb), mspec(2 * nb), mspec(0), mspec(0), mspec(0), mspec(0), mspec(0)]
        + [_ANY] * len(aliased),
        out_specs=[_act_spec(plan, t, sb, c, hbw, lambda h: wa // hbw + h, 3), st(l)],
        out_shape=[jax.ShapeDtypeStruct(mix.shape, mix.dtype),
                   jax.ShapeDtypeStruct((depth,) + s0_all.shape[1:], F32)],
        input_output_aliases=aliases,
        scratch_shapes=[pltpu.VMEM((sb, nh, hd, hd), F32), pltpu.VMEM((sb, 1, hbw), F32),
                        pltpu.VMEM((sb, 1, hbw), F32), pltpu.VMEM((sb, 1, hbw), F32)],
        compiler_params=_cparams(3),
        name="rwkv7",
    )(z3, z3, z3, z3, lw, a, zprev, zprev, zprev, s0_all, muz, muz, muz, kkp, kap, rkp, lnw, lnb,
      *aliased)
    return out


def _hgrn_head(q, f, v, g, s0, lb, ng, n_valid, sub):
    c, dk = q.shape
    sig = jax.nn.sigmoid(f)
    logg = jnp.log(jnp.maximum(lb + (1.0 - lb) * sig, TINY))
    kf = (1.0 - lb) * jax.nn.sigmoid(-f)
    if n_valid < c:
        valid = lax.broadcasted_iota(jnp.int32, (c, dk), 0) < n_valid
        logg = jnp.where(valid, logg, 0.0)
        kf = jnp.where(valid, kf, 0.0)
    bc = _cumsum_time(logg, 0)
    o_inter = _bdot(q * jnp.exp(bc), s0)
    t_sub = lax.broadcasted_iota(jnp.int32, (sub, 1), 0)
    blocks = []
    for bi in range(c // sub):
        lo = bi * sub
        qb, bb, kb, vb = q[lo:lo + sub], bc[lo:lo + sub], kf[lo:lo + sub], v[lo:lo + sub]
        ob = o_inter[lo:lo + sub]
        if bi > 0:
            bs = bc[lo - 1:lo]
            att = _bdot(qb * jnp.exp(bb - bs), kf[:lo] * jnp.exp(bs - bc[:lo]), NT)
            ob = ob + _bdot(att, v[:lo])
        for s2 in range(sub):
            dec = jnp.exp(jnp.minimum(bb - bb[s2:s2 + 1], 0.0))
            col = jnp.sum(qb * dec * kb[s2:s2 + 1], axis=-1, keepdims=True)
            ob = ob + jnp.where(t_sub >= s2, col, 0.0) * vb[s2:s2 + 1]
        blocks.append(ob)
    o = jnp.concatenate(blocks, axis=0) if len(blocks) > 1 else blocks[0]
    b_last = bc[c - 1:c]
    e_col = jnp.broadcast_to(jnp.exp(b_last), (v.shape[1], dk)).T
    s_new = e_col * s0 + _bdot(kf * jnp.exp(b_last - bc), v, TN)
    on = o * lax.rsqrt(jnp.mean(o * o, axis=-1, keepdims=True) + RMS_EPS) * ng
    return on * _silu(g), s_new


_HGRN_INPUTS = 7


def _hgrn_body(*refs, n_valid, dk, sub, n_alias):
    q_ref, f_ref, i_ref, g_ref, s0_ref, lb_ref, ng_ref = refs[:_HGRN_INPUTS]
    o_ref, sout_ref, s_scr, o_scr = refs[_HGRN_INPUTS + n_alias:]
    j = pl.program_id(2)

    @pl.when(j == 0)
    def _():
        s_scr[...] = s0_ref[...]

    sb, c, hcw = q_ref.shape
    nh = hcw // dk
    lb = lb_ref[...]
    ng = ng_ref[...]

    def one(s, carry):
        q, f, v, g = q_ref[s], f_ref[s], i_ref[s], g_ref[s]
        outs = []
        for hh in range(nh):
            sl = slice(hh * dk, (hh + 1) * dk)
            o, s_new = _hgrn_head(q[:, sl], f[:, sl], v[:, sl], g[:, sl], s_scr[s, hh],
                                  lb[:, sl], ng[:, sl], n_valid, sub)
            s_scr[s, hh] = s_new
            outs.append(o)
        o_scr[s] = jnp.concatenate(outs, axis=1) if nh > 1 else outs[0]
        return carry

    if sb == 1:
        one(0, 0)
    else:
        lax.fori_loop(0, sb, one, 0)
    o_ref[...] = o_scr[...].reshape(o_ref.shape).astype(o_ref.dtype)

    @pl.when(j == pl.num_programs(2) - 1)
    def _():
        sout_ref[...] = s_scr[...]


def _hgrn(z3, s0_all, l_in, mix, s_acc, l, depth, lb, ng, *, col0, out_col0, wc, hcw, plan):
    b, t, _ = z3.shape
    sb, c = plan.sb_hgrn, plan.chunk
    dk, dv = s0_all.shape[-2], s0_all.shape[-1]
    nb = wc // hcw
    nh = hcw // dk
    off = col0 // hcw
    zspec = lambda o: pl.BlockSpec((sb, c, hcw), lambda i, h, j, o=o: (i, j, o + h))
    st = lambda ll: pl.BlockSpec((None, sb, nh, dk, dv), lambda i, h, j, ll=ll: (ll, i, h, 0, 0))
    par = pl.BlockSpec((1, hcw), lambda i, h, j: (0, h))
    aliased = [mix] + ([] if s_acc is None else [s_acc])
    n_in = _HGRN_INPUTS
    aliases = {n_in: 0} if s_acc is None else {n_in: 0, n_in + 1: 1}
    return pl.pallas_call(
        functools.partial(_hgrn_body, n_valid=min(plan.n_valid, c), dk=dk, sub=min(SUBLANES, c),
                          n_alias=len(aliased)),
        grid=(b // sb, nb, t // c),
        in_specs=[zspec(off), zspec(off + nb), zspec(off + 2 * nb), zspec(off + 3 * nb), st(l_in),
                  par, par] + [_ANY] * len(aliased),
        out_specs=[_act_spec(plan, t, sb, c, hcw, lambda h: out_col0 // hcw + h, 3), st(l)],
        out_shape=[jax.ShapeDtypeStruct(mix.shape, mix.dtype),
                   jax.ShapeDtypeStruct((depth,) + s0_all.shape[1:], F32)],
        input_output_aliases=aliases,
        scratch_shapes=[pltpu.VMEM((sb, nh, dk, dv), F32), pltpu.VMEM((sb, c, hcw), F32)],
        compiler_params=_cparams(3),
        name="hgrn2",
    )(z3, z3, z3, z3, s0_all, lb, ng, *aliased)


def _post1_body(h_ref, m_ref, g_ref, h1_ref, hn_ref):
    m = m_ref[...]
    h1 = h_ref[...] + m * lax.rsqrt(jnp.mean(m * m, axis=-1, keepdims=True) + RMS_EPS) * g_ref[...]
    h1_ref[...] = h1
    hn_ref[...] = (h1 * lax.rsqrt(jnp.mean(h1 * h1, axis=-1, keepdims=True) + RMS_EPS)).astype(hn_ref.dtype)


def _post2_body(h_ref, p_ref, proj_ref, gate_ref, g_ref, o_ref):
    u = jnp.dot(p_ref[...], proj_ref[...], preferred_element_type=F32)
    x = u * jax.nn.sigmoid(gate_ref[...])
    o_ref[...] = h_ref[...] + x * lax.rsqrt(jnp.mean(x * x, axis=-1, keepdims=True) + RMS_EPS) * g_ref[...]


def _post_ple(h1, p2, proj_all, l, gate, gain, tr=256):
    n, d = h1.shape
    kp = p2.shape[1]
    tr = min(tr, n)
    row = pl.BlockSpec((tr, d), lambda i: (i, 0))
    return pl.pallas_call(
        _post2_body,
        grid=(n // tr,),
        in_specs=[row, pl.BlockSpec((tr, kp), lambda i: (i, 0)),
                  pl.BlockSpec((None, kp, d), lambda i: (l, 0, 0)), row,
                  pl.BlockSpec((1, d), lambda i: (0, 0))],
        out_specs=row,
        out_shape=jax.ShapeDtypeStruct((n, d), F32),
        compiler_params=_cparams(1),
        name="post_ple",
    )(h1, p2, proj_all, gate, gain)


def _rowwise(body, name, arrays, gain, out_dtypes, tr=256):
    n, d = arrays[0].shape
    tr = min(tr, n)
    row = pl.BlockSpec((tr, d), lambda i: (i, 0))
    return pl.pallas_call(
        body,
        grid=(n // tr,),
        in_specs=[row] * len(arrays) + [pl.BlockSpec((1, d), lambda i: (0, 0))],
        out_specs=[row] * len(out_dtypes),
        out_shape=[jax.ShapeDtypeStruct((n, d), dt) for dt in out_dtypes],
        compiler_params=_cparams(1),
        name=name,
    )(*arrays, gain)


def _block_diag_tiles(w):
    nb, bs, _ = w.shape
    per = LANES // bs
    w = w.reshape(nb // per, per, bs, bs)
    eye = jnp.eye(per, dtype=w.dtype)
    return jnp.einsum('tpab,pq->tpaqb', w, eye).reshape(nb // per, LANES, LANES)


def _layer_stack(x, p, st, W, plan, zero_state):
    st_a_h, st_a_conv, st_b_s, st_b_x, st_b_z, st_c_s = st
    b, t, d = x.shape
    depth = W['w_in'].shape[0]
    wa = st_a_h.shape[-1]
    hd = st_b_s.shape[-1]
    wb = st_b_s.shape[-3] * hd
    dk = st_c_s.shape[-2]
    wc = st_c_s.shape[-3] * dk
    mix_w = wa + wb + wc
    n = b * t
    nv = plan.n_valid
    hbw = math.gcd(math.gcd(2 * wa, wb), math.gcd(wa, 512))
    hcw = math.gcd(math.gcd(2 * wa + 4 * wb, wc), math.gcd(wa + wb, 512))

    lb_soft = jax.nn.softmax(W['hgrn_lb'].astype(F32), axis=0)
    lb_all = jnp.cumsum(lb_soft, axis=0) - lb_soft[0]
    p_act = p.astype(BF16)

    h = x
    small = [[] for _ in range(4)]
    sb_acc = sc_acc = None
    for l in range(depth):
        ls = 0 if zero_state else l
        xn, lw, a, xlast = _prenorm(
            h, W['g_pre'][l].reshape(1, 1, d), st_b_x[ls][:, None, :],
            W['rwkv_mu_w'][l].reshape(1, 1, d), W['rwkv_mu_a'][l].reshape(1, 1, d),
            W['rwkv_w1'][l], W['rwkv_a1'][l], W['rwkv_w2'][l], W['rwkv_a2'][l],
            W['rwkv_w0'][l][None], W['rwkv_a0'][l][None], plan)
        z3 = _matmul(xn.reshape(n, d), W['w_in'], l).reshape(b, t, -1)
        cb8 = jnp.pad(st_a_conv[ls], ((0, 0), (SUBLANES - (CONV_W - 1), 0), (0, 0)))
        mix, ha, tail = _rglru(
            z3, st_a_h[ls][:, None, :], cb8, W['conv_a_w'][l][None], W['conv_a_b'][l].reshape(1, 1, wa),
            W['lru_wr_bd'][l], W['lru_br'][l][None], W['lru_wi_bd'][l], W['lru_bi'][l][None],
            W['lru_lambda'][l][None], wa=wa, mix_w=mix_w, plan=plan)
        mix, sb_acc = _rwkv(
            z3, lw, a, st_b_z[ls][:, None, :], st_b_s, ls, mix, sb_acc, l, depth,
            W['rwkv_mu_z'][l][None], W['rwkv_kk'][l][None], W['rwkv_ka'][l][None],
            W['rwkv_rk'][l].reshape(1, wb), W['rwkv_lnx_w'][l][None], W['rwkv_lnx_b'][l][None],
            wa=wa, wb=wb, hbw=hbw, plan=plan)
        mix, sc_acc = _hgrn(
            z3, st_c_s, ls, mix, sc_acc, l, depth, lb_all[l][None], W['hgrn_norm_g'][l][None],
            col0=2 * wa + 4 * wb, out_col0=wa + wb, wc=wc, hcw=hcw, plan=plan)
        mm = _matmul(mix.reshape(n, mix_w), W['w_out'], l)
        h1, hn = _rowwise(_post1_body, "post_mix", [h.reshape(n, d), mm], W['g_post'][l][None],
                          [F32, BF16])
        gp = _matmul(hn, W['ple_gate'], l)
        h = _post_ple(h1, p_act[l].reshape(n, -1), W['ple_proj'], l, gp, W['g_ple'][l][None]).reshape(b, t, d)
        small[0].append(ha[:, 0])
        small[1].append(tail[:, SUBLANES - (CONV_W - 1):])
        small[2].append(xlast[:, 0])
        small[3].append(z3[:, nv - 1, 2 * wa:2 * wa + 3 * wb])
    na_h, na_c, nb_x, nb_z = (jnp.stack(o) for o in small)
    return h, (na_h, na_c, sb_acc, nb_x, nb_z, sc_acc)


def kernel(x_prompt, x_sample, p_prompt, p_sample, state_a_h, state_a_conv, state_b_S,
           state_b_xprev, state_b_zprev, state_c_S, g_pre, g_post, w_in, w_out,
           conv_a_w, conv_a_b, lru_wr, lru_br, lru_wi, lru_bi, lru_lambda,
           rwkv_mu_z, rwkv_mu_w, rwkv_mu_a, rwkv_w0, rwkv_w1, rwkv_w2, rwkv_a0, rwkv_a1,
           rwkv_a2, rwkv_kk, rwkv_ka, rwkv_rk, rwkv_lnx_w, rwkv_lnx_b, hgrn_lb, hgrn_norm_g,
           ple_proj, ple_gate, g_ple):
    depth = w_in.shape[0]
    W = dict(g_pre=g_pre, g_post=g_post, conv_a_w=conv_a_w, conv_a_b=conv_a_b,
             lru_br=lru_br, lru_bi=lru_bi, lru_lambda=lru_lambda, rwkv_mu_z=rwkv_mu_z,
             rwkv_mu_w=rwkv_mu_w, rwkv_mu_a=rwkv_mu_a, rwkv_w0=rwkv_w0, rwkv_a0=rwkv_a0,
             rwkv_kk=rwkv_kk, rwkv_ka=rwkv_ka, rwkv_rk=rwkv_rk, rwkv_lnx_w=rwkv_lnx_w,
             rwkv_lnx_b=rwkv_lnx_b, hgrn_lb=hgrn_lb, hgrn_norm_g=hgrn_norm_g, g_ple=g_ple)
    for name, w in (('w_in', w_in), ('w_out', w_out), ('rwkv_w1', rwkv_w1), ('rwkv_w2', rwkv_w2),
                    ('rwkv_a1', rwkv_a1), ('rwkv_a2', rwkv_a2), ('ple_proj', ple_proj),
                    ('ple_gate', ple_gate)):
        W[name] = w.astype(BF16)
    W['lru_wr_bd'] = jnp.stack([_block_diag_tiles(lru_wr[l]) for l in range(depth)]).astype(BF16)
    W['lru_wi_bd'] = jnp.stack([_block_diag_tiles(lru_wi[l]) for l in range(depth)]).astype(BF16)

    bp, tp, d = x_prompt.shape
    bs, ts, _ = x_sample.shape
    dt = x_prompt.dtype
    st_s = (state_a_h, state_a_conv, state_b_S, state_b_xprev, state_b_zprev, state_c_S)
    st_p = tuple(jnp.zeros((1, bp) + s.shape[2:], dt) for s in st_s)
    y_p, out_p = _layer_stack(x_prompt, p_prompt, st_p, W, _make_plan(bp, tp, tp), True)

    tpad = -(-ts // SUBLANES) * SUBLANES
    pad_t = lambda v, ax: jnp.pad(v, [(0, tpad - ts) if i == ax else (0, 0) for i in range(v.ndim)])
    y_s, out_s = _layer_stack(pad_t(x_sample, 1), pad_t(p_sample, 2), st_s, W,
                              _make_plan(bs, tpad, ts), False)
    return (y_p, y_s[:, :ts]) + out_p + out_s
```

```python
import functools
import math
from typing import NamedTuple

import jax
import jax.numpy as jnp
from jax import lax
from jax.experimental import pallas as pl
from jax.experimental.pallas import tpu as pltpu

F32 = jnp.float32
BF16 = jnp.bfloat16

RMS_EPS = 1e-6
GN_EPS = 64e-5
LRU_C = 8.0
TINY = 1e-30
KK_EPS = 1e-12
CONV_W = 4

LANES = 128
SUBLANES = 8
BF16_ROWS = 16
VMEM_LIMIT = 52 * 1024 * 1024

NT = (((1,), (1,)), ((), ()))
TN = (((0,), (0,)), ((), ()))
BNT = (((2,), (2,)), ((0,), (0,)))
BNN = (((2,), (1,)), ((0,), (0,)))
BTN = (((1,), (1,)), ((0,), (0,)))


class Plan(NamedTuple):
    n_valid: int
    tt: int
    sb_rows: int
    chunk: int
    sb_rwkv: int
    sb_hgrn: int
    flat_acts: bool
    rwkv_steps: bool


def _make_plan(b, t, n_valid):
    tt = min(t, 128)
    chunk = min(t, 64)
    rows = 128
    sb = max(1, min(b, rows // tt))
    flat = chunk == t
    assert flat or (chunk % BF16_ROWS == 0 and tt % BF16_ROWS == 0)
    return Plan(n_valid=n_valid, tt=tt, sb_rows=sb, chunk=chunk,
                sb_rwkv=math.gcd(b, max(4, rows // chunk)), sb_hgrn=sb if chunk < 64 else 1,
                flat_acts=flat, rwkv_steps=flat and b % LANES == 0 and t <= SUBLANES)


def _act_shape(plan, b, t, width):
    return jax.ShapeDtypeStruct((b * t, width) if plan.flat_acts else (b, t, width), BF16)


def _act_spec(plan, t, sb, rows, width, col, grid_rank):
    if grid_rank == 2:
        if plan.flat_acts:
            return pl.BlockSpec((sb * rows, width), lambda i, j: (i * (t // rows) + j, col(0)))
        return pl.BlockSpec((sb, rows, width), lambda i, j: (i, j, col(0)))
    if plan.flat_acts:
        return pl.BlockSpec((sb * rows, width), lambda i, h, j: (i * (t // rows) + j, col(h)))
    return pl.BlockSpec((sb, rows, width), lambda i, h, j: (i, j, col(h)))


def _cparams(n_axes):
    return pltpu.CompilerParams(dimension_semantics=("arbitrary",) * n_axes,
                                vmem_limit_bytes=VMEM_LIMIT)


def _softplus(x):
    return jnp.maximum(x, 0.0) + jnp.log1p(jnp.exp(-jnp.abs(x)))


def _silu(x):
    return x * jax.nn.sigmoid(x)


def _bdot(a, b, dims=None):
    a = a.astype(BF16)
    b = b.astype(BF16)
    if dims is None:
        return jnp.dot(a, b, preferred_element_type=F32)
    return lax.dot_general(a, b, dims, preferred_element_type=F32)


def _cumsum_time(x, axis):
    n = x.shape[axis]
    idx = lax.broadcasted_iota(jnp.int32, x.shape, axis)
    d = 1
    while d < n:
        x = x + jnp.where(idx >= d, pltpu.roll(x, d, axis), 0.0)
        d *= 2
    return x


_ANY = pl.BlockSpec(memory_space=pl.ANY)


def _mm_body(x_ref, w_ref, o_ref):
    o_ref[...] = jnp.dot(x_ref[...], w_ref[...], preferred_element_type=F32)


def _mm_cast_body(x_ref, w_ref, o_ref, xb_ref):
    @pl.when(pl.program_id(1) == 0)
    def _():
        xb_ref[...] = x_ref[...].astype(BF16)

    o_ref[...] = jnp.dot(xb_ref[...], w_ref[...], preferred_element_type=F32)


def _matmul(x, w_all, l, tn=512):
    m, k = x.shape
    n = w_all.shape[2]
    packed = x.dtype == BF16
    tm = min(1024 if packed else 512, m)
    tn = math.gcd(n, 2 * tn if packed else tn)
    return pl.pallas_call(
        _mm_body if packed else _mm_cast_body,
        grid=(m // tm, n // tn),
        in_specs=[pl.BlockSpec((tm, k), lambda i, j: (i, 0)),
                  pl.BlockSpec((None, k, tn), lambda i, j: (l, 0, j))],
        out_specs=pl.BlockSpec((tm, tn), lambda i, j: (i, j)),
        out_shape=jax.ShapeDtypeStruct((m, n), F32),
        scratch_shapes=[] if packed else [pltpu.VMEM((tm, k), BF16)],
        compiler_params=_cparams(2),
        name="matmul",
    )(x, w_all)


def _prenorm_body(h_ref, g_ref, xprev_ref, muw_ref, mua_ref, w1_ref, a1_ref, w2_ref, a2_ref,
                  w0_ref, a0_ref, xn_ref, lw_ref, a_ref, xlast_ref, carry_ref, *, last_row):
    j = pl.program_id(1)

    @pl.when(j == 0)
    def _():
        carry_ref[...] = xprev_ref[...]

    h = h_ref[...]
    sb, tt, d = h.shape
    xn = h * lax.rsqrt(jnp.mean(h * h, axis=-1, keepdims=True) + RMS_EPS) * g_ref[...]
    xn_ref[...] = xn.reshape(xn_ref.shape).astype(xn_ref.dtype)
    t_idx = lax.broadcasted_iota(jnp.int32, (sb, tt, d), 1)
    prev = jnp.where(t_idx == 0, carry_ref[...], pltpu.roll(xn, 1, 1))
    carry_ref[...] = xn[:, tt - 1:tt, :]

    @pl.when(j == pl.num_programs(1) - 1)
    def _():
        xlast_ref[...] = xn[:, last_row:last_row + 1, :]

    dx = prev - xn
    xw = (xn + dx * muw_ref[...]).reshape(sb * tt, d)
    xa = (xn + dx * mua_ref[...]).reshape(sb * tt, d)
    yw = w0_ref[...] + _bdot(jnp.tanh(_bdot(xw, w1_ref[...])), w2_ref[...])
    wl = -_softplus(-yw) - 0.5
    lw_ref[...] = (-jnp.exp(wl)).reshape(lw_ref.shape)
    ya = a0_ref[...] + _bdot(_bdot(xa, a1_ref[...]), a2_ref[...])
    a_ref[...] = jax.nn.sigmoid(ya).reshape(a_ref.shape)


def _prenorm(h3, g, xprev, muw, mua, w1, a1, w2, a2, w0, a0, plan):
    b, t, d = h3.shape
    sb, tt = plan.sb_rows, plan.tt
    wb = w2.shape[1]
    lw_dim = w1.shape[1]
    row3 = lambda i, j: (i, j, 0)
    par3 = lambda i, j: (0, 0, 0)
    par2 = lambda i, j: (0, 0)
    seq3 = lambda i, j: (i, 0, 0)
    return pl.pallas_call(
        functools.partial(_prenorm_body, last_row=(plan.n_valid - 1) % tt),
        grid=(b // sb, t // tt),
        in_specs=[pl.BlockSpec((sb, tt, d), row3),
                  pl.BlockSpec((1, 1, d), par3),
                  pl.BlockSpec((sb, 1, d), seq3),
                  pl.BlockSpec((1, 1, d), par3),
                  pl.BlockSpec((1, 1, d), par3),
                  pl.BlockSpec((d, lw_dim), par2),
                  pl.BlockSpec((d, lw_dim), par2),
                  pl.BlockSpec((lw_dim, wb), par2),
                  pl.BlockSpec((lw_dim, wb), par2),
                  pl.BlockSpec((1, wb), par2),
                  pl.BlockSpec((1, wb), par2)],
        out_specs=[_act_spec(plan, t, sb, tt, d, lambda h: 0, 2),
                   pl.BlockSpec((sb, tt, wb), row3),
                   pl.BlockSpec((sb, tt, wb), row3),
                   pl.BlockSpec((sb, 1, d), seq3)],
        out_shape=[_act_shape(plan, b, t, d),
                   jax.ShapeDtypeStruct((b, t, wb), F32),
                   jax.ShapeDtypeStruct((b, t, wb), F32),
                   jax.ShapeDtypeStruct((b, 1, d), F32)],
        scratch_shapes=[pltpu.VMEM((sb, 1, d), F32)],
        compiler_params=_cparams(2),
        name="prenorm_lora",
    )(h3, g, xprev, muw, mua, w1, a1, w2, a2, w0, a0)


def _rglru_body(x_ref, gt_ref, h0_ref, cb_ref, cw_ref, cbias_ref, wr_ref, br_ref, wi_ref, bi_ref,
                lam_ref, o_ref, hout_ref, tail_ref, hc_ref, tl_ref, *, nv_last):
    j = pl.program_id(1)

    @pl.when(j == 0)
    def _():
        hc_ref[...] = h0_ref[...]
        tl_ref[...] = cb_ref[...]

    x = x_ref[...]
    sb, tt, w = x.shape
    ext = jnp.concatenate([tl_ref[...], x], axis=1)
    tl_ref[...] = ext[:, tt:tt + SUBLANES, :]
    u = cbias_ref[...]
    for tap in range(CONV_W):
        sh = CONV_W - 1 - tap
        xs = x if sh == 0 else pltpu.roll(ext, sh, 1)[:, SUBLANES:, :]
        u = u + xs * cw_ref[:, tap:tap + 1, :]
    u2 = u.reshape(sb * tt, w)
    rp, ip = [], []
    for m in range(w // LANES):
        um = u2[:, m * LANES:(m + 1) * LANES]
        rp.append(_bdot(um, wr_ref[m]))
        ip.append(_bdot(um, wi_ref[m]))
    r = jax.nn.sigmoid(jnp.concatenate(rp, axis=1) + br_ref[...])
    i = jax.nn.sigmoid(jnp.concatenate(ip, axis=1) + bi_ref[...])
    log_a = -LRU_C * r * _softplus(-lam_ref[...])
    th = jnp.tanh(log_a)
    bcoef = jnp.sqrt(-2.0 * th / (1.0 - th)) * (i * u2)
    a_cum = jnp.exp(log_a).reshape(sb, tt, w)
    b_cum = bcoef.reshape(sb, tt, w)
    t_idx = lax.broadcasted_iota(jnp.int32, (sb, tt, w), 1)
    d = 1
    while d < tt:
        keep = t_idx >= d
        a_prev = jnp.where(keep, pltpu.roll(a_cum, d, 1), 1.0)
        b_prev = jnp.where(keep, pltpu.roll(b_cum, d, 1), 0.0)
        b_cum = a_cum * b_prev + b_cum
        a_cum = a_cum * a_prev
        d *= 2
    hs = a_cum * hc_ref[...] + b_cum
    hc_ref[...] = hs[:, tt - 1:tt, :]
    o_ref[...] = (hs * _silu(gt_ref[...])).reshape(o_ref.shape).astype(o_ref.dtype)

    @pl.when(j == pl.num_programs(1) - 1)
    def _():
        hout_ref[...] = hs[:, nv_last - 1:nv_last, :]
        if nv_last == tt:
            tail_ref[...] = ext[:, tt:tt + SUBLANES, :]
        else:
            tail_ref[...] = pltpu.roll(ext, tt + SUBLANES - nv_last, 1)[:, 0:SUBLANES, :]


def _rglru(z3, h0, cb8, cw, cbias, wr_bd, br, wi_bd, bi, lam, *, wa, mix_w, plan):
    b, t, _ = z3.shape
    sb, tt = plan.sb_rows, plan.tt
    nt = wa // LANES
    par3 = lambda i, j: (0, 0, 0)
    seq3 = lambda i, j: (i, 0, 0)
    return pl.pallas_call(
        functools.partial(_rglru_body, nv_last=(plan.n_valid - 1) % tt + 1),
        grid=(b // sb, t // tt),
        in_specs=[pl.BlockSpec((sb, tt, wa), lambda i, j: (i, j, 0)),
                  pl.BlockSpec((sb, tt, wa), lambda i, j: (i, j, 1)),
                  pl.BlockSpec((sb, 1, wa), seq3),
                  pl.BlockSpec((sb, SUBLANES, wa), seq3),
                  pl.BlockSpec((1, CONV_W, wa), par3),
                  pl.BlockSpec((1, 1, wa), par3),
                  pl.BlockSpec((nt, LANES, LANES), par3),
                  pl.BlockSpec((1, wa), lambda i, j: (0, 0)),
                  pl.BlockSpec((nt, LANES, LANES), par3),
                  pl.BlockSpec((1, wa), lambda i, j: (0, 0)),
                  pl.BlockSpec((1, wa), lambda i, j: (0, 0))],
        out_specs=[_act_spec(plan, t, sb, tt, wa, lambda h: 0, 2),
                   pl.BlockSpec((sb, 1, wa), seq3),
                   pl.BlockSpec((sb, SUBLANES, wa), seq3)],
        out_shape=[_act_shape(plan, b, t, mix_w),
                   jax.ShapeDtypeStruct((b, 1, wa), F32),
                   jax.ShapeDtypeStruct((b, SUBLANES, wa), F32)],
        scratch_shapes=[pltpu.VMEM((sb, 1, wa), F32), pltpu.VMEM((sb, SUBLANES, wa), F32)],
        compiler_params=_cparams(2),
        name="rglru",
    )(z3, z3, h0, cb8, cw, cbias, wr_bd, br, wi_bd, bi, lam)


def _split_heads(x, hd):
    sb, c, w = x.shape
    nh = w // hd
    st = jnp.stack([x[:, :, h * hd:(h + 1) * hd] for h in range(nh)], axis=1)
    return st.reshape(sb * nh, c, hd)


def _merge_heads(x, sb):
    n, c, hd = x.shape
    nh = n // sb
    x4 = x.reshape(sb, nh, c, hd)
    return jnp.concatenate([x4[:, h] for h in range(nh)], axis=-1)


_RWKV_INPUTS = 18


def _rwkv_body(*refs, n_valid, hd, n_alias):
    (zr_ref, zk_ref, zv_ref, zg_ref, lw_ref, a_ref, pr_ref, pk_ref, pv_ref, s0_ref,
     mur_ref, muk_ref, muv_ref, kk_ref, ka_ref, rk_ref, lnw_ref, lnb_ref) = refs[:_RWKV_INPUTS]
    o_ref, sout_ref, s_scr, cr_scr, ck_scr, cv_scr = refs[_RWKV_INPUTS + n_alias:]
    j = pl.program_id(2)

    @pl.when(j == 0)
    def _():
        s_scr[...] = s0_ref[...]
        cr_scr[...] = pr_ref[...]
        ck_scr[...] = pk_ref[...]
        cv_scr[...] = pv_ref[...]

    sb, c, hbw = zr_ref.shape
    nh = hbw // hd
    n = sb * nh
    row = lax.broadcasted_iota(jnp.int32, (sb, c, hbw), 1)
    first = row == 0

    def mix(z_ref, carry_ref, mu_ref):
        z = z_ref[...]
        zm = z + (jnp.where(first, carry_ref[...], pltpu.roll(z, 1, 1)) - z) * mu_ref[...]
        carry_ref[...] = z[:, c - 1:c, :]
        return zm

    r = mix(zr_ref, cr_scr, mur_ref)
    k = mix(zk_ref, ck_scr, muk_ref)
    v = mix(zv_ref, cv_scr, muv_ref)
    a = a_ref[...]
    lw = lw_ref[...]
    kk_raw = k * kk_ref[...]
    kmod = k * (1.0 + (a - 1.0) * ka_ref[...])
    if n_valid < c:
        valid = row < n_valid
        lw = jnp.where(valid, lw, 0.0)
        kmod = jnp.where(valid, kmod, 0.0)
        v = jnp.where(valid, v, 0.0)
        a = jnp.where(valid, a, 0.0)
    cl = _cumsum_time(lw, 1)

    per_head = lambda x: _split_heads(x, hd)
    par_head = lambda ref: jnp.concatenate([_split_heads(ref[...][None], hd)] * sb, axis=0)
    kkh, ah, kmh, vh, rh, clh, lwh = map(per_head, (kk_raw, a, kmod, v, r, cl, lw))
    kkh = kkh / jnp.maximum(jnp.sqrt(jnp.sum(kkh * kkh, axis=-1, keepdims=True)), KK_EPS)
    bh = kkh * ah
    cl_last = clh[:, c - 1:c, :]
    e_neg = jnp.exp(-clh)
    e_end = jnp.exp(cl_last - clh)
    lhs = jnp.concatenate([kkh * jnp.exp(clh - lwh), rh * jnp.exp(clh)], axis=1)
    s0 = s_scr[...].reshape(n, hd, hd)

    ti = lax.broadcasted_iota(jnp.int32, (c, c), 0)
    si = lax.broadcasted_iota(jnp.int32, (c, c), 1)
    strict = (ti > si)[None]
    lower = (ti >= si)[None]
    ab = _bdot(lhs, bh * e_neg, BNT)
    ak = _bdot(lhs, kmh * e_neg, BNT)
    su = _bdot(lhs, s0, BNT)
    x = su[:, :c] + _bdot(jnp.where(strict, ak[:, :c], 0.0), vh, BNN)
    lp = jnp.where(strict, ab[:, :c], 0.0)
    x = x - _bdot(lp, x, BNN)
    p = 2
    while p < c:
        lp = _bdot(lp, lp, BNN)
        x = x + _bdot(lp, x, BNN)
        p *= 2
    y = (su[:, c:] + _bdot(jnp.where(lower, ak[:, c:], 0.0), vh, BNN)
         - _bdot(jnp.where(lower, ab[:, c:], 0.0), x, BNN))
    s_new = s0 * jnp.exp(cl_last) + _bdot(vh, kmh * e_end, BTN) - _bdot(x, bh * e_end, BTN)
    s_scr[...] = s_new.reshape(s_scr.shape)

    mean = jnp.mean(y, axis=-1, keepdims=True)
    var = jnp.mean(jnp.square(y - mean), axis=-1, keepdims=True)
    yn = (y - mean) * lax.rsqrt(var + GN_EPS) * par_head(lnw_ref) + par_head(lnb_ref)
    bonus = jnp.sum(rh * kmh * par_head(rk_ref), axis=-1, keepdims=True) * vh
    out = _merge_heads(yn + bonus, sb) * _silu(zg_ref[...])
    o_ref[...] = out.reshape(o_ref.shape).astype(o_ref.dtype)

    @pl.when(j == pl.num_programs(2) - 1)
    def _():
        sout_ref[...] = s_new.reshape(sout_ref.shape)


def _rwkv(z3, lw, a, zprev, s0_all, l_in, mix, s_acc, l, depth, muz, kkp, kap, rkp, lnw, lnb,
          *, wa, wb, hbw, plan):
    b, t, _ = z3.shape
    sb, c = plan.sb_rwkv, plan.chunk
    hd = s0_all.shape[-1]
    nb = wb // hbw
    nh = hbw // hd
    off = 2 * wa // hbw
    zspec = lambda o: pl.BlockSpec((sb, c, hbw), lambda i, h, j, o=o: (i, j, o + h))
    pspec = lambda o: pl.BlockSpec((sb, 1, hbw), lambda i, h, j, o=o: (i, 0, o + h))
    mspec = lambda o: pl.BlockSpec((1, hbw), lambda i, h, j, o=o: (0, o + h))
    act = pl.BlockSpec((sb, c, hbw), lambda i, h, j: (i, j, h))
    st = lambda ll: pl.BlockSpec((None, sb, nh, hd, hd), lambda i, h, j, ll=ll: (ll, i, h, 0, 0))
    aliased = [mix] + ([] if s_acc is None else [s_acc])
    n_in = _RWKV_INPUTS
    aliases = {n_in: 0} if s_acc is None else {n_in: 0, n_in + 1: 1}
    out = pl.pallas_call(
        functools.partial(_rwkv_body, n_valid=min(plan.n_valid, c), hd=hd, n_alias=len(aliased)),
        grid=(b // sb, nb, t // c),
        in_specs=[zspec(off), zspec(off + nb), zspec(off + 2 * nb), zspec(off + 3 * nb), act, act,
                  pspec(0), pspec(nb), pspec(2 * nb), st(l_in),
                  mspec(0), mspec(nb), mspec(2 * nb), mspec(0), mspec(0), mspec(0), mspec(0), mspec(0)]
        + [_ANY] * len(aliased),
        out_specs=[_act_spec(plan, t, sb, c, hbw, lambda h: wa // hbw + h, 3), st(l)],
        out_shape=[jax.ShapeDtypeStruct(mix.shape, mix.dtype),
                   jax.ShapeDtypeStruct((depth,) + s0_all.shape[1:], F32)],
        input_output_aliases=aliases,
        scratch_shapes=[pltpu.VMEM((sb, nh, hd, hd), F32), pltpu.VMEM((sb, 1, hbw), F32),
                        pltpu.VMEM((sb, 1, hbw), F32), pltpu.VMEM((sb, 1, hbw), F32)],
        compiler_params=_cparams(3),
        name="rwkv7",
    )(z3, z3, z3, z3, lw, a, zprev, zprev, zprev, s0_all, muz, muz, muz, kkp, kap, rkp, lnw, lnb,
      *aliased)
    return out


def _rwkv_steps_body(*refs, n_valid, tpad, hd, n_alias):
    (zr_ref, zk_ref, zv_ref, zg_ref, lw_ref, a_ref, pr_ref, pk_ref, pv_ref, s0_ref,
     mur_ref, muk_ref, muv_ref, kk_ref, ka_ref, rk_ref, lnw_ref, lnb_ref) = refs[:_RWKV_INPUTS]
    o_ref, sout_ref, o_scr, vt_scr, y_scr = refs[_RWKV_INPUTS + n_alias:]
    bsz, cols = pr_ref.shape
    nh = cols // hd
    sout_ref[...] = s0_ref[...]
    o_scr[...] = jnp.zeros_like(o_scr)
    per_col = lambda ref: jnp.broadcast_to(ref[...], (bsz, cols)).T
    lnw_c, lnb_c, rk_c = per_col(lnw_ref), per_col(lnb_ref), per_col(rk_ref)
    step_rows = lambda ref, t: ref[pl.ds(t, bsz, stride=tpad), :]
    prev = (pr_ref[...], pk_ref[...], pv_ref[...])
    for t in range(n_valid):
        z = (step_rows(zr_ref, t), step_rows(zk_ref, t), step_rows(zv_ref, t))
        r, k, v = (zz + (pp - zz) * mu[...] for zz, pp, mu in zip(z, prev, (mur_ref, muk_ref, muv_ref)))
        prev = z
        a = step_rows(a_ref, t)
        w_t = jnp.exp(step_rows(lw_ref, t)).T
        a_t = a.T
        r_t = r.T
        v_t = v.T
        kk_t = (k * kk_ref[...]).T
        km_t = (k * (1.0 + (a - 1.0) * ka_ref[...])).T
        vt_scr[...] = v_t
        for hh in range(nh):
            sl = slice(hh * hd, (hh + 1) * hd)
            kkh = kk_t[sl]
            kkh = kkh / jnp.maximum(jnp.sqrt(jnp.sum(kkh * kkh, axis=0, keepdims=True)), KK_EPS)
            bh, wh, kmh, rh = kkh * a_t[sl], w_t[sl], km_t[sl], r_t[sl]

            def value_row(vi, carry, hh=hh, kkh=kkh, bh=bh, wh=wh, kmh=kmh, rh=rh):
                s = sout_ref[hh, vi]
                sa = jnp.sum(s * kkh, axis=0, keepdims=True)
                s = s * wh - sa * bh + vt_scr[pl.ds(hh * hd + vi, 1), :] * kmh
                sout_ref[hh, vi] = s
                y_scr[pl.ds(hh * hd + vi, 1), :] = jnp.sum(s * rh, axis=0, keepdims=True)
                return carry

            lax.fori_loop(0, hd, value_row, 0, unroll=4)
        y = y_scr[...]
        outs = []
        for hh in range(nh):
            sl = slice(hh * hd, (hh + 1) * hd)
            yh = y[sl]
            mean = jnp.mean(yh, axis=0, keepdims=True)
            var = jnp.mean(jnp.square(yh - mean), axis=0, keepdims=True)
            yn = (yh - mean) * lax.rsqrt(var + GN_EPS) * lnw_c[sl] + lnb_c[sl]
            bonus = jnp.sum(r_t[sl] * km_t[sl] * rk_c[sl], axis=0, keepdims=True) * v_t[sl]
            outs.append(yn + bonus)
        out = jnp.concatenate(outs, axis=0).T * _silu(step_rows(zg_ref, t))
        o_scr[pl.ds(t, bsz, stride=tpad), :] = out
    o_ref[...] = o_scr[...].astype(o_ref.dtype)


def _rwkv_steps(z2, lw2, a2, zprev, s0_all, l_in, mix, s_acc, l, depth, muz, kkp, kap, rkp, lnw, lnb,
                *, wa, wb, plan, tpad):
    bsz = zprev.shape[0]
    rows = z2.shape[0]
    hd = s0_all.shape[-2]
    cols = LANES
    nb = wb // cols
    nh = cols // hd
    off = 2 * wa // cols
    zspec = lambda o: pl.BlockSpec((rows, cols), lambda h, o=o: (0, o + h))
    pspec = lambda o: pl.BlockSpec((bsz, cols), lambda h, o=o: (0, o + h))
    mspec = lambda o: pl.BlockSpec((1, cols), lambda h, o=o: (0, o + h))
    st = lambda ll: pl.BlockSpec((None, nh, hd, hd, bsz), lambda h, ll=ll: (ll, h, 0, 0, 0))
    aliased = [mix] + ([] if s_acc is None else [s_acc])
    n_in = _RWKV_INPUTS
    aliases = {n_in: 0} if s_acc is None else {n_in: 0, n_in + 1: 1}
    return pl.pallas_call(
        functools.partial(_rwkv_steps_body, n_valid=plan.n_valid, tpad=tpad, hd=hd, n_alias=len(aliased)),
        grid=(nb,),
        in_specs=[zspec(off), zspec(off + nb), zspec(off + 2 * nb), zspec(off + 3 * nb),
                  zspec(0), zspec(0), pspec(0), pspec(nb), pspec(2 * nb), st(l_in),
                  mspec(0), mspec(nb), mspec(2 * nb), mspec(0), mspec(0), mspec(0), mspec(0), mspec(0)]
        + [_ANY] * len(aliased),
        out_specs=[pl.BlockSpec((rows, cols), lambda h: (0, wa // cols + h)), st(l)],
        out_shape=[jax.ShapeDtypeStruct(mix.shape, mix.dtype),
                   jax.ShapeDtypeStruct((depth,) + s0_all.shape[1:], F32)],
        input_output_aliases=aliases,
        scratch_shapes=[pltpu.VMEM((rows, cols), F32), pltpu.VMEM((cols, bsz), F32),
                        pltpu.VMEM((cols, bsz), F32)],
        compiler_params=_cparams(1),
        name="rwkv7_steps",
    )(z2, z2, z2, z2, lw2, a2, zprev, zprev, zprev, s0_all, muz, muz, muz, kkp, kap, rkp, lnw, lnb,
      *aliased)


def _hgrn_head(q, f, v, g, s0, lb, ng, n_valid, sub):
    c, dk = q.shape
    sig = jax.nn.sigmoid(f)
    logg = jnp.log(jnp.maximum(lb + (1.0 - lb) * sig, TINY))
    kf = (1.0 - lb) * jax.nn.sigmoid(-f)
    if n_valid < c:
        valid = lax.broadcasted_iota(jnp.int32, (c, dk), 0) < n_valid
        logg = jnp.where(valid, logg, 0.0)
        kf = jnp.where(valid, kf, 0.0)
    bc = _cumsum_time(logg, 0)
    o_inter = _bdot(q * jnp.exp(bc), s0)
    t_row = lax.broadcasted_iota(jnp.int32, (SUBLANES, 1), 0)
    blocks = []
    for bi in range(c // sub):
        lo = bi * sub
        qb, bb, kb, vb = q[lo:lo + sub], bc[lo:lo + sub], kf[lo:lo + sub], v[lo:lo + sub]
        ob = o_inter[lo:lo + sub]
        if bi > 0:
            bs = bc[lo - 1:lo]
            att = _bdot(qb * jnp.exp(bb - bs), kf[:lo] * jnp.exp(bs - bc[:lo]), NT)
            ob = ob + _bdot(att, v[:lo])
        for r0 in range(0, sub, SUBLANES):
            qg, bg, og = qb[r0:r0 + SUBLANES], bb[r0:r0 + SUBLANES], ob[r0:r0 + SUBLANES]
            for s2 in range(r0 + SUBLANES):
                dec = jnp.exp(bg - bb[s2:s2 + 1])
                col = jnp.sum(qg * dec * kb[s2:s2 + 1], axis=-1, keepdims=True)
                if s2 > r0:
                    col = jnp.where(t_row >= s2 - r0, col, 0.0)
                og = og + col * vb[s2:s2 + 1]
            blocks.append(og)
    o = jnp.concatenate(blocks, axis=0) if len(blocks) > 1 else blocks[0]
    b_last = bc[c - 1:c]
    e_col = jnp.broadcast_to(jnp.exp(b_last), (v.shape[1], dk)).T
    s_new = e_col * s0 + _bdot(kf * jnp.exp(b_last - bc), v, TN)
    on = o * lax.rsqrt(jnp.mean(o * o, axis=-1, keepdims=True) + RMS_EPS) * ng
    return on * _silu(g), s_new


_HGRN_INPUTS = 7


def _hgrn_body(*refs, n_valid, dk, sub, n_alias):
    q_ref, f_ref, i_ref, g_ref, s0_ref, lb_ref, ng_ref = refs[:_HGRN_INPUTS]
    o_ref, sout_ref, s_scr, o_scr = refs[_HGRN_INPUTS + n_alias:]
    j = pl.program_id(2)

    @pl.when(j == 0)
    def _():
        s_scr[...] = s0_ref[...]

    sb, c, hcw = q_ref.shape
    nh = hcw // dk
    lb = lb_ref[...]
    ng = ng_ref[...]

    def one(s, carry):
        q, f, v, g = q_ref[s], f_ref[s], i_ref[s], g_ref[s]
        outs = []
        for hh in range(nh):
            sl = slice(hh * dk, (hh + 1) * dk)
            o, s_new = _hgrn_head(q[:, sl], f[:, sl], v[:, sl], g[:, sl], s_scr[s, hh],
                                  lb[:, sl], ng[:, sl], n_valid, sub)
            s_scr[s, hh] = s_new
            outs.append(o)
        o_scr[s] = jnp.concatenate(outs, axis=1) if nh > 1 else outs[0]
        return carry

    if sb == 1:
        one(0, 0)
    else:
        lax.fori_loop(0, sb, one, 0)
    o_ref[...] = o_scr[...].reshape(o_ref.shape).astype(o_ref.dtype)

    @pl.when(j == pl.num_programs(2) - 1)
    def _():
        sout_ref[...] = s_scr[...]


def _hgrn(z3, s0_all, l_in, mix, s_acc, l, depth, lb, ng, *, col0, out_col0, wc, hcw, plan):
    b, t, _ = z3.shape
    sb, c = plan.sb_hgrn, plan.chunk
    dk, dv = s0_all.shape[-2], s0_all.shape[-1]
    nb = wc // hcw
    nh = hcw // dk
    off = col0 // hcw
    zspec = lambda o: pl.BlockSpec((sb, c, hcw), lambda i, h, j, o=o: (i, j, o + h))
    st = lambda ll: pl.BlockSpec((None, sb, nh, dk, dv), lambda i, h, j, ll=ll: (ll, i, h, 0, 0))
    par = pl.BlockSpec((1, hcw), lambda i, h, j: (0, h))
    aliased = [mix] + ([] if s_acc is None else [s_acc])
    n_in = _HGRN_INPUTS
    aliases = {n_in: 0} if s_acc is None else {n_in: 0, n_in + 1: 1}
    return pl.pallas_call(
        functools.partial(_hgrn_body, n_valid=min(plan.n_valid, c), dk=dk, sub=min(2 * SUBLANES, c),
                          n_alias=len(aliased)),
        grid=(b // sb, nb, t // c),
        in_specs=[zspec(off), zspec(off + nb), zspec(off + 2 * nb), zspec(off + 3 * nb), st(l_in),
                  par, par] + [_ANY] * len(aliased),
        out_specs=[_act_spec(plan, t, sb, c, hcw, lambda h: out_col0 // hcw + h, 3), st(l)],
        out_shape=[jax.ShapeDtypeStruct(mix.shape, mix.dtype),
                   jax.ShapeDtypeStruct((depth,) + s0_all.shape[1:], F32)],
        input_output_aliases=aliases,
        scratch_shapes=[pltpu.VMEM((sb, nh, dk, dv), F32), pltpu.VMEM((sb, c, hcw), F32)],
        compiler_params=_cparams(3),
        name="hgrn2",
    )(z3, z3, z3, z3, s0_all, lb, ng, *aliased)


def _post1_body(h_ref, m_ref, g_ref, h1_ref, hn_ref):
    m = m_ref[...]
    h1 = h_ref[...] + m * lax.rsqrt(jnp.mean(m * m, axis=-1, keepdims=True) + RMS_EPS) * g_ref[...]
    h1_ref[...] = h1
    hn_ref[...] = (h1 * lax.rsqrt(jnp.mean(h1 * h1, axis=-1, keepdims=True) + RMS_EPS)).astype(hn_ref.dtype)


def _post2_body(h_ref, p_ref, proj_ref, gate_ref, g_ref, o_ref):
    u = jnp.dot(p_ref[...], proj_ref[...], preferred_element_type=F32)
    x = u * jax.nn.sigmoid(gate_ref[...])
    o_ref[...] = h_ref[...] + x * lax.rsqrt(jnp.mean(x * x, axis=-1, keepdims=True) + RMS_EPS) * g_ref[...]


def _post_ple(h1, p2, proj_all, l, gate, gain, tr=256):
    n, d = h1.shape
    kp = p2.shape[1]
    tr = min(tr, n)
    row = pl.BlockSpec((tr, d), lambda i: (i, 0))
    return pl.pallas_call(
        _post2_body,
        grid=(n // tr,),
        in_specs=[row, pl.BlockSpec((tr, kp), lambda i: (i, 0)),
                  pl.BlockSpec((None, kp, d), lambda i: (l, 0, 0)), row,
                  pl.BlockSpec((1, d), lambda i: (0, 0))],
        out_specs=row,
        out_shape=jax.ShapeDtypeStruct((n, d), F32),
        compiler_params=_cparams(1),
        name="post_ple",
    )(h1, p2, proj_all, gate, gain)


def _rowwise(body, name, arrays, gain, out_dtypes, tr=256):
    n, d = arrays[0].shape
    tr = min(tr, n)
    row = pl.BlockSpec((tr, d), lambda i: (i, 0))
    return pl.pallas_call(
        body,
        grid=(n // tr,),
        in_specs=[row] * len(arrays) + [pl.BlockSpec((1, d), lambda i: (0, 0))],
        out_specs=[row] * len(out_dtypes),
        out_shape=[jax.ShapeDtypeStruct((n, d), dt) for dt in out_dtypes],
        compiler_params=_cparams(1),
        name=name,
    )(*arrays, gain)


def _block_diag_tiles(w):
    nb, bs, _ = w.shape
    per = LANES // bs
    w = w.reshape(nb // per, per, bs, bs)
    eye = jnp.eye(per, dtype=w.dtype)
    return jnp.einsum('tpab,pq->tpaqb', w, eye).reshape(nb // per, LANES, LANES)


def _layer_stack(x, p, st, W, plan, zero_state):
    st_a_h, st_a_conv, st_b_s, st_b_x, st_b_z, st_c_s = st
    b, t, d = x.shape
    depth = W['w_in'].shape[0]
    wa = st_a_h.shape[-1]
    hd = st_b_s.shape[-1]
    wb = st_b_s.shape[-3] * hd
    dk = st_c_s.shape[-2]
    wc = st_c_s.shape[-3] * dk
    mix_w = wa + wb + wc
    n = b * t
    nv = plan.n_valid
    hbw = math.gcd(math.gcd(2 * wa, wb), math.gcd(wa, 512))
    hcw = math.gcd(math.gcd(2 * wa + 4 * wb, wc), math.gcd(wa + wb, 512))

    lb_soft = jax.nn.softmax(W['hgrn_lb'].astype(F32), axis=0)
    lb_all = jnp.cumsum(lb_soft, axis=0) - lb_soft[0]
    p_act = p.astype(BF16)

    if plan.rwkv_steps:
        st_b_s = jnp.transpose(st_b_s, (0, 2, 3, 4, 1))
    rwkv_params = lambda l: (
        W['rwkv_mu_z'][l][None], W['rwkv_kk'][l][None], W['rwkv_ka'][l][None],
        W['rwkv_rk'][l].reshape(1, wb), W['rwkv_lnx_w'][l][None], W['rwkv_lnx_b'][l][None])

    h = x
    small = [[] for _ in range(4)]
    sb_acc = sc_acc = None
    for l in range(depth):
        ls = 0 if zero_state else l
        xn, lw, a, xlast = _prenorm(
            h, W['g_pre'][l].reshape(1, 1, d), st_b_x[ls][:, None, :],
            W['rwkv_mu_w'][l].reshape(1, 1, d), W['rwkv_mu_a'][l].reshape(1, 1, d),
            W['rwkv_w1'][l], W['rwkv_a1'][l], W['rwkv_w2'][l], W['rwkv_a2'][l],
            W['rwkv_w0'][l][None], W['rwkv_a0'][l][None], plan)
        z3 = _matmul(xn.reshape(n, d), W['w_in'], l).reshape(b, t, -1)
        cb8 = jnp.pad(st_a_conv[ls], ((0, 0), (SUBLANES - (CONV_W - 1), 0), (0, 0)))
        mix, ha, tail = _rglru(
            z3, st_a_h[ls][:, None, :], cb8, W['conv_a_w'][l][None], W['conv_a_b'][l].reshape(1, 1, wa),
            W['lru_wr_bd'][l], W['lru_br'][l][None], W['lru_wi_bd'][l], W['lru_bi'][l][None],
            W['lru_lambda'][l][None], wa=wa, mix_w=mix_w, plan=plan)
        if plan.rwkv_steps:
            mix, sb_acc = _rwkv_steps(
                z3.reshape(n, -1), lw.reshape(n, wb), a.reshape(n, wb), st_b_z[ls], st_b_s, ls,
                mix, sb_acc, l, depth, *rwkv_params(l), wa=wa, wb=wb, plan=plan, tpad=t)
        else:
            mix, sb_acc = _rwkv(
                z3, lw, a, st_b_z[ls][:, None, :], st_b_s, ls, mix, sb_acc, l, depth,
                *rwkv_params(l), wa=wa, wb=wb, hbw=hbw, plan=plan)
        mix, sc_acc = _hgrn(
            z3, st_c_s, ls, mix, sc_acc, l, depth, lb_all[l][None], W['hgrn_norm_g'][l][None],
            col0=2 * wa + 4 * wb, out_col0=wa + wb, wc=wc, hcw=hcw, plan=plan)
        mm = _matmul(mix.reshape(n, mix_w), W['w_out'], l)
        h1, hn = _rowwise(_post1_body, "post_mix", [h.reshape(n, d), mm], W['g_post'][l][None],
                          [F32, BF16])
        gp = _matmul(hn, W['ple_gate'], l)
        h = _post_ple(h1, p_act[l].reshape(n, -1), W['ple_proj'], l, gp, W['g_ple'][l][None]).reshape(b, t, d)
        small[0].append(ha[:, 0])
        small[1].append(tail[:, SUBLANES - (CONV_W - 1):])
        small[2].append(xlast[:, 0])
        small[3].append(z3[:, nv - 1, 2 * wa:2 * wa + 3 * wb])
    na_h, na_c, nb_x, nb_z = (jnp.stack(o) for o in small)
    if plan.rwkv_steps:
        sb_acc = jnp.transpose(sb_acc, (0, 4, 1, 2, 3))
    return h, (na_h, na_c, sb_acc, nb_x, nb_z, sc_acc)


def kernel(x_prompt, x_sample, p_prompt, p_sample, state_a_h, state_a_conv, state_b_S,
           state_b_xprev, state_b_zprev, state_c_S, g_pre, g_post, w_in, w_out,
           conv_a_w, conv_a_b, lru_wr, lru_br, lru_wi, lru_bi, lru_lambda,
           rwkv_mu_z, rwkv_mu_w, rwkv_mu_a, rwkv_w0, rwkv_w1, rwkv_w2, rwkv_a0, rwkv_a1,
           rwkv_a2, rwkv_kk, rwkv_ka, rwkv_rk, rwkv_lnx_w, rwkv_lnx_b, hgrn_lb, hgrn_norm_g,
           ple_proj, ple_gate, g_ple):
    depth = w_in.shape[0]
    W = dict(g_pre=g_pre, g_post=g_post, conv_a_w=conv_a_w, conv_a_b=conv_a_b,
             lru_br=lru_br, lru_bi=lru_bi, lru_lambda=lru_lambda, rwkv_mu_z=rwkv_mu_z,
             rwkv_mu_w=rwkv_mu_w, rwkv_mu_a=rwkv_mu_a, rwkv_w0=rwkv_w0, rwkv_a0=rwkv_a0,
             rwkv_kk=rwkv_kk, rwkv_ka=rwkv_ka, rwkv_rk=rwkv_rk, rwkv_lnx_w=rwkv_lnx_w,
             rwkv_lnx_b=rwkv_lnx_b, hgrn_lb=hgrn_lb, hgrn_norm_g=hgrn_norm_g, g_ple=g_ple)
    for name, w in (('w_in', w_in), ('w_out', w_out), ('rwkv_w1', rwkv_w1), ('rwkv_w2', rwkv_w2),
                    ('rwkv_a1', rwkv_a1), ('rwkv_a2', rwkv_a2), ('ple_proj', ple_proj),
                    ('ple_gate', ple_gate)):
        W[name] = w.astype(BF16)
    W['lru_wr_bd'] = jnp.stack([_block_diag_tiles(lru_wr[l]) for l in range(depth)]).astype(BF16)
    W['lru_wi_bd'] = jnp.stack([_block_diag_tiles(lru_wi[l]) for l in range(depth)]).astype(BF16)

    bp, tp, d = x_prompt.shape
    bs, ts, _ = x_sample.shape
    dt = x_prompt.dtype
    st_s = (state_a_h, state_a_conv, state_b_S, state_b_xprev, state_b_zprev, state_c_S)
    st_p = tuple(jnp.zeros((1, bp) + s.shape[2:], dt) for s in st_s)
    y_p, out_p = _layer_stack(x_prompt, p_prompt, st_p, W, _make_plan(bp, tp, tp), True)

    tpad = -(-ts // SUBLANES) * SUBLANES
    pad_t = lambda v, ax: jnp.pad(v, [(0, tpad - ts) if i == ax else (0, 0) for i in range(v.ndim)])
    y_s, out_s = _layer_stack(pad_t(x_sample, 1), pad_t(p_sample, 2), st_s, W,
                              _make_plan(bs, tpad, ts), False)
    return (y_p, y_s[:, :ts]) + out_p + out_s
```

```python
import functools
import math
from typing import NamedTuple

import jax
import jax.numpy as jnp
from jax import lax
from jax.experimental import pallas as pl
from jax.experimental.pallas import tpu as pltpu

F32 = jnp.float32
BF16 = jnp.bfloat16

RMS_EPS = 1e-6
GN_EPS = 64e-5
LRU_C = 8.0
TINY = 1e-30
KK_EPS = 1e-12
CONV_W = 4

LANES = 128
SUBLANES = 8
BF16_ROWS = 16
VMEM_LIMIT = 52 * 1024 * 1024

NT = (((1,), (1,)), ((), ()))
TN = (((0,), (0,)), ((), ()))
BNT = (((2,), (2,)), ((0,), (0,)))
BNN = (((2,), (1,)), ((0,), (0,)))
BTN = (((1,), (1,)), ((0,), (0,)))


class Plan(NamedTuple):
    n_valid: int
    tt: int
    sb_rows: int
    chunk: int
    sb_rwkv: int
    sb_hgrn: int
    flat_acts: bool
    rwkv_steps: bool


def _make_plan(b, t, n_valid):
    tt = min(t, 128)
    chunk = min(t, 64)
    rows = 128
    sb = max(1, min(b, rows // tt))
    flat = chunk == t
    assert flat or (chunk % BF16_ROWS == 0 and tt % BF16_ROWS == 0)
    return Plan(n_valid=n_valid, tt=tt, sb_rows=sb, chunk=chunk,
                sb_rwkv=math.gcd(b, max(4, rows // chunk)), sb_hgrn=sb if chunk < 64 else 1,
                flat_acts=flat, rwkv_steps=flat and b % LANES == 0 and t <= SUBLANES)


def _act_shape(plan, b, t, width):
    return jax.ShapeDtypeStruct((b * t, width) if plan.flat_acts else (b, t, width), BF16)


def _act_spec(plan, t, sb, rows, width, col, grid_rank):
    if grid_rank == 2:
        if plan.flat_acts:
            return pl.BlockSpec((sb * rows, width), lambda i, j: (i * (t // rows) + j, col(0)))
        return pl.BlockSpec((sb, rows, width), lambda i, j: (i, j, col(0)))
    if plan.flat_acts:
        return pl.BlockSpec((sb * rows, width), lambda i, h, j: (i * (t // rows) + j, col(h)))
    return pl.BlockSpec((sb, rows, width), lambda i, h, j: (i, j, col(h)))


def _cparams(n_axes):
    return pltpu.CompilerParams(dimension_semantics=("arbitrary",) * n_axes,
                                vmem_limit_bytes=VMEM_LIMIT)


def _softplus(x):
    return jnp.maximum(x, 0.0) + jnp.log1p(jnp.exp(-jnp.abs(x)))


def _silu(x):
    return x * jax.nn.sigmoid(x)


def _bdot(a, b, dims=None):
    a = a.astype(BF16)
    b = b.astype(BF16)
    if dims is None:
        return jnp.dot(a, b, preferred_element_type=F32)
    return lax.dot_general(a, b, dims, preferred_element_type=F32)


def _cumsum_time(x, axis):
    n = x.shape[axis]
    idx = lax.broadcasted_iota(jnp.int32, x.shape, axis)
    d = 1
    while d < n:
        x = x + jnp.where(idx >= d, pltpu.roll(x, d, axis), 0.0)
        d *= 2
    return x


_ANY = pl.BlockSpec(memory_space=pl.ANY)


MM_TILE = 1024


def _mm_body(x_ref, w_ref, o_ref):
    o_ref[...] = jnp.dot(x_ref[...], w_ref[...], preferred_element_type=F32).astype(o_ref.dtype)


def _mm_wcast_body(x_ref, w_ref, o_ref, wb_ref):
    wb = w_ref[...].astype(BF16)
    wb_ref[...] = wb
    o_ref[...] = jnp.dot(x_ref[...], wb, preferred_element_type=F32).astype(o_ref.dtype)


def _matmul(x, w, out_dtype):
    m, k = x.shape
    n = w.shape[1]
    tm = math.gcd(m, MM_TILE)
    tn = math.gcd(n, MM_TILE)
    return pl.pallas_call(
        _mm_body,
        grid=(m // tm, n // tn),
        in_specs=[pl.BlockSpec((tm, k), lambda i, j: (i, 0)),
                  pl.BlockSpec((k, tn), lambda i, j: (0, j))],
        out_specs=pl.BlockSpec((tm, tn), lambda i, j: (i, j)),
        out_shape=jax.ShapeDtypeStruct((m, n), out_dtype),
        compiler_params=_cparams(2),
        name="matmul",
    )(x, w)


def _matmul_wcast(x, w_all, l, out_dtype):
    m, k = x.shape
    n = w_all.shape[2]
    tn = math.gcd(n, MM_TILE // 2)
    return pl.pallas_call(
        _mm_wcast_body,
        grid=(n // tn,),
        in_specs=[pl.BlockSpec((m, k), lambda j: (0, 0)),
                  pl.BlockSpec((None, k, tn), lambda j: (l, 0, j))],
        out_specs=[pl.BlockSpec((m, tn), lambda j: (0, j)),
                   pl.BlockSpec((k, tn), lambda j: (0, j))],
        out_shape=[jax.ShapeDtypeStruct((m, n), out_dtype), jax.ShapeDtypeStruct((k, n), BF16)],
        compiler_params=_cparams(1),
        name="matmul_wcast",
    )(x, w_all)


def _prenorm_body(h_ref, g_ref, xprev_ref, muw_ref, mua_ref, w1_ref, a1_ref, w2_ref, a2_ref,
                  w0_ref, a0_ref, xn_ref, lw_ref, a_ref, xlast_ref, carry_ref, *, last_row):
    j = pl.program_id(1)

    @pl.when(j == 0)
    def _():
        carry_ref[...] = xprev_ref[...]

    h = h_ref[...]
    sb, tt, d = h.shape
    xn = h * lax.rsqrt(jnp.mean(h * h, axis=-1, keepdims=True) + RMS_EPS) * g_ref[...]
    xn_ref[...] = xn.reshape(xn_ref.shape).astype(xn_ref.dtype)
    t_idx = lax.broadcasted_iota(jnp.int32, (sb, tt, d), 1)
    prev = jnp.where(t_idx == 0, carry_ref[...], pltpu.roll(xn, 1, 1))
    carry_ref[...] = xn[:, tt - 1:tt, :]

    @pl.when(j == pl.num_programs(1) - 1)
    def _():
        xlast_ref[...] = xn[:, last_row:last_row + 1, :]

    dx = prev - xn
    xw = (xn + dx * muw_ref[...]).reshape(sb * tt, d)
    xa = (xn + dx * mua_ref[...]).reshape(sb * tt, d)
    yw = w0_ref[...] + _bdot(jnp.tanh(_bdot(xw, w1_ref[...])), w2_ref[...])
    wl = -_softplus(-yw) - 0.5
    lw_ref[...] = (-jnp.exp(wl)).reshape(lw_ref.shape)
    ya = a0_ref[...] + _bdot(_bdot(xa, a1_ref[...]), a2_ref[...])
    a_ref[...] = jax.nn.sigmoid(ya).reshape(a_ref.shape)


def _prenorm(h3, g, xprev, muw, mua, w1, a1, w2, a2, w0, a0, plan):
    b, t, d = h3.shape
    sb, tt = plan.sb_rows, plan.tt
    wb = w2.shape[1]
    lw_dim = w1.shape[1]
    row3 = lambda i, j: (i, j, 0)
    par3 = lambda i, j: (0, 0, 0)
    par2 = lambda i, j: (0, 0)
    seq3 = lambda i, j: (i, 0, 0)
    return pl.pallas_call(
        functools.partial(_prenorm_body, last_row=(plan.n_valid - 1) % tt),
        grid=(b // sb, t // tt),
        in_specs=[pl.BlockSpec((sb, tt, d), row3),
                  pl.BlockSpec((1, 1, d), par3),
                  pl.BlockSpec((sb, 1, d), seq3),
                  pl.BlockSpec((1, 1, d), par3),
                  pl.BlockSpec((1, 1, d), par3),
                  pl.BlockSpec((d, lw_dim), par2),
                  pl.BlockSpec((d, lw_dim), par2),
                  pl.BlockSpec((lw_dim, wb), par2),
                  pl.BlockSpec((lw_dim, wb), par2),
                  pl.BlockSpec((1, wb), par2),
                  pl.BlockSpec((1, wb), par2)],
        out_specs=[_act_spec(plan, t, sb, tt, d, lambda h: 0, 2),
                   pl.BlockSpec((sb, tt, wb), row3),
                   pl.BlockSpec((sb, tt, wb), row3),
                   pl.BlockSpec((sb, 1, d), seq3)],
        out_shape=[_act_shape(plan, b, t, d),
                   jax.ShapeDtypeStruct((b, t, wb), F32),
                   jax.ShapeDtypeStruct((b, t, wb), F32),
                   jax.ShapeDtypeStruct((b, 1, d), F32)],
        scratch_shapes=[pltpu.VMEM((sb, 1, d), F32)],
        compiler_params=_cparams(2),
        name="prenorm_lora",
    )(h3, g, xprev, muw, mua, w1, a1, w2, a2, w0, a0)


def _rglru_body(x_ref, gt_ref, h0_ref, cb_ref, cw_ref, cbias_ref, wr_ref, br_ref, wi_ref, bi_ref,
                lam_ref, o_ref, hout_ref, tail_ref, hc_ref, tl_ref, *, nv_last):
    j = pl.program_id(1)

    @pl.when(j == 0)
    def _():
        hc_ref[...] = h0_ref[...]
        tl_ref[...] = cb_ref[...]

    x = x_ref[...]
    sb, tt, w = x.shape
    ext = jnp.concatenate([tl_ref[...], x], axis=1)
    tl_ref[...] = ext[:, tt:tt + SUBLANES, :]
    u = cbias_ref[...]
    for tap in range(CONV_W):
        sh = CONV_W - 1 - tap
        xs = x if sh == 0 else pltpu.roll(ext, sh, 1)[:, SUBLANES:, :]
        u = u + xs * cw_ref[:, tap:tap + 1, :]
    u2 = u.reshape(sb * tt, w)
    rp, ip = [], []
    for m in range(w // LANES):
        um = u2[:, m * LANES:(m + 1) * LANES]
        rp.append(_bdot(um, wr_ref[m]))
        ip.append(_bdot(um, wi_ref[m]))
    r = jax.nn.sigmoid(jnp.concatenate(rp, axis=1) + br_ref[...])
    i = jax.nn.sigmoid(jnp.concatenate(ip, axis=1) + bi_ref[...])
    log_a = -LRU_C * r * _softplus(-lam_ref[...])
    th = jnp.tanh(log_a)
    bcoef = jnp.sqrt(-2.0 * th / (1.0 - th)) * (i * u2)
    a_cum = jnp.exp(log_a).reshape(sb, tt, w)
    b_cum = bcoef.reshape(sb, tt, w)
    t_idx = lax.broadcasted_iota(jnp.int32, (sb, tt, w), 1)
    d = 1
    while d < tt:
        keep = t_idx >= d
        a_prev = jnp.where(keep, pltpu.roll(a_cum, d, 1), 1.0)
        b_prev = jnp.where(keep, pltpu.roll(b_cum, d, 1), 0.0)
        b_cum = a_cum * b_prev + b_cum
        a_cum = a_cum * a_prev
        d *= 2
    hs = a_cum * hc_ref[...] + b_cum
    hc_ref[...] = hs[:, tt - 1:tt, :]
    o_ref[...] = (hs * _silu(gt_ref[...])).reshape(o_ref.shape).astype(o_ref.dtype)

    @pl.when(j == pl.num_programs(1) - 1)
    def _():
        hout_ref[...] = hs[:, nv_last - 1:nv_last, :]
        if nv_last == tt:
            tail_ref[...] = ext[:, tt:tt + SUBLANES, :]
        else:
            tail_ref[...] = pltpu.roll(ext, tt + SUBLANES - nv_last, 1)[:, 0:SUBLANES, :]


def _rglru(z3, h0, cb8, cw, cbias, wr_bd, br, wi_bd, bi, lam, *, wa, mix_w, plan):
    b, t, _ = z3.shape
    sb, tt = plan.sb_rows, plan.tt
    nt = wa // LANES
    par3 = lambda i, j: (0, 0, 0)
    seq3 = lambda i, j: (i, 0, 0)
    return pl.pallas_call(
        functools.partial(_rglru_body, nv_last=(plan.n_valid - 1) % tt + 1),
        grid=(b // sb, t // tt),
        in_specs=[pl.BlockSpec((sb, tt, wa), lambda i, j: (i, j, 0)),
                  pl.BlockSpec((sb, tt, wa), lambda i, j: (i, j, 1)),
                  pl.BlockSpec((sb, 1, wa), seq3),
                  pl.BlockSpec((sb, SUBLANES, wa), seq3),
                  pl.BlockSpec((1, CONV_W, wa), par3),
                  pl.BlockSpec((1, 1, wa), par3),
                  pl.BlockSpec((nt, LANES, LANES), par3),
                  pl.BlockSpec((1, wa), lambda i, j: (0, 0)),
                  pl.BlockSpec((nt, LANES, LANES), par3),
                  pl.BlockSpec((1, wa), lambda i, j: (0, 0)),
                  pl.BlockSpec((1, wa), lambda i, j: (0, 0))],
        out_specs=[_act_spec(plan, t, sb, tt, wa, lambda h: 0, 2),
                   pl.BlockSpec((sb, 1, wa), seq3),
                   pl.BlockSpec((sb, SUBLANES, wa), seq3)],
        out_shape=[_act_shape(plan, b, t, mix_w),
                   jax.ShapeDtypeStruct((b, 1, wa), F32),
                   jax.ShapeDtypeStruct((b, SUBLANES, wa), F32)],
        scratch_shapes=[pltpu.VMEM((sb, 1, wa), F32), pltpu.VMEM((sb, SUBLANES, wa), F32)],
        compiler_params=_cparams(2),
        name="rglru",
    )(z3, z3, h0, cb8, cw, cbias, wr_bd, br, wi_bd, bi, lam)


def _split_heads(x, hd):
    sb, c, w = x.shape
    nh = w // hd
    st = jnp.stack([x[:, :, h * hd:(h + 1) * hd] for h in range(nh)], axis=1)
    return st.reshape(sb * nh, c, hd)


def _merge_heads(x, sb):
    n, c, hd = x.shape
    nh = n // sb
    x4 = x.reshape(sb, nh, c, hd)
    return jnp.concatenate([x4[:, h] for h in range(nh)], axis=-1)


def _pair_diag(x, half):
    lo = lax.broadcasted_iota(jnp.int32, x.shape, 2) < half
    return jnp.concatenate([jnp.where(lo, x, 0.0), jnp.where(lo, 0.0, x)], axis=1)


def _pair_sum(x, half):
    lo = lax.broadcasted_iota(jnp.int32, x.shape, 2) < half
    s_lo = jnp.sum(jnp.where(lo, x, 0.0), axis=-1, keepdims=True)
    s_hi = jnp.sum(jnp.where(lo, 0.0, x), axis=-1, keepdims=True)
    return jnp.where(lo, s_lo, s_hi)


_RWKV_INPUTS = 18


def _rwkv_body(*refs, n_valid, hd, n_alias):
    (zr_ref, zk_ref, zv_ref, zg_ref, lw_ref, a_ref, pr_ref, pk_ref, pv_ref, s0_ref,
     mur_ref, muk_ref, muv_ref, kk_ref, ka_ref, rk_ref, lnw_ref, lnb_ref) = refs[:_RWKV_INPUTS]
    o_ref, sout_ref, s_scr, cr_scr, ck_scr, cv_scr = refs[_RWKV_INPUTS + n_alias:]
    j = pl.program_id(2)

    sb, c, hbw = zr_ref.shape
    pw = 2 * hd
    npair = hbw // pw
    n = sb * npair

    @pl.when(j == 0)
    def _():
        s0 = s0_ref[...].reshape(sb, npair, 2, hd, hd)
        zero = jnp.zeros((sb, npair, hd, hd), F32)
        top = jnp.concatenate([s0[:, :, 0], zero], axis=-1)
        bot = jnp.concatenate([zero, s0[:, :, 1]], axis=-1)
        s_scr[...] = jnp.concatenate([top, bot], axis=-2).reshape(n, pw, pw)
        cr_scr[...] = pr_ref[...]
        ck_scr[...] = pk_ref[...]
        cv_scr[...] = pv_ref[...]

    row = lax.broadcasted_iota(jnp.int32, (sb, c, hbw), 1)
    first = row == 0

    def mix(z_ref, carry_ref, mu_ref):
        z = z_ref[...]
        zm = z + (jnp.where(first, carry_ref[...], pltpu.roll(z, 1, 1)) - z) * mu_ref[...]
        carry_ref[...] = z[:, c - 1:c, :]
        return zm

    r = mix(zr_ref, cr_scr, mur_ref)
    k = mix(zk_ref, ck_scr, muk_ref)
    v = mix(zv_ref, cv_scr, muv_ref)
    a = a_ref[...]
    lw = lw_ref[...]
    kk_raw = k * kk_ref[...]
    kmod = k * (1.0 + (a - 1.0) * ka_ref[...])
    if n_valid < c:
        valid = row < n_valid
        lw = jnp.where(valid, lw, 0.0)
        kmod = jnp.where(valid, kmod, 0.0)
        v = jnp.where(valid, v, 0.0)
        a = jnp.where(valid, a, 0.0)
    cl = _cumsum_time(lw, 1)

    pairs = lambda x: _split_heads(x, pw)
    par_pair = lambda ref: jnp.concatenate([_split_heads(ref[...][None], pw)] * sb, axis=0)
    kkp, ap, kmp, vp, rp, clp, lwp = map(pairs, (kk_raw, a, kmod, v, r, cl, lw))
    kkp = kkp / jnp.maximum(jnp.sqrt(_pair_sum(kkp * kkp, hd)), KK_EPS)
    bp = kkp * ap
    cl_last = clp[:, c - 1:c, :]
    e_neg = jnp.exp(-clp)
    e_end = jnp.exp(cl_last - clp)
    lhs = jnp.concatenate([kkp * jnp.exp(clp - lwp), rp * jnp.exp(clp)], axis=1)
    s0 = s_scr[...]

    ti = lax.broadcasted_iota(jnp.int32, (c, 2 * c), 0)
    si = lax.broadcasted_iota(jnp.int32, (c, 2 * c), 1)
    si = jnp.where(si >= c, si - c, si)
    strict = (ti > si)[None]
    lower = (ti >= si)[None]
    ab = _bdot(lhs, _pair_diag(bp * e_neg, hd), BNT)
    ak = _bdot(lhs, _pair_diag(kmp * e_neg, hd), BNT)
    su = _bdot(lhs, s0, BNT)
    v_bd = _pair_diag(vp, hd)
    x = su[:, :c] + _bdot(jnp.where(strict, ak[:, :c], 0.0), v_bd, BNN)
    lp = jnp.where(strict, ab[:, :c], 0.0)
    x = x - _bdot(lp, _pair_diag(x, hd), BNN)
    p = 2
    while p < c:
        lp = _bdot(lp, _pair_diag(lp, c), BNN)
        x = x + _bdot(lp, _pair_diag(x, hd), BNN)
        p *= 2
    y = (su[:, c:] + _bdot(jnp.where(lower, ak[:, c:], 0.0), v_bd, BNN)
         - _bdot(jnp.where(lower, ab[:, c:], 0.0), _pair_diag(x, hd), BNN))
    full = s0 * jnp.exp(cl_last) + _bdot(vp, kmp * e_end, BTN) - _bdot(x, bp * e_end, BTN)
    same_head = ((lax.broadcasted_iota(jnp.int32, (pw, pw), 0) < hd)
                 == (lax.broadcasted_iota(jnp.int32, (pw, pw), 1) < hd))[None]
    s_new = jnp.where(same_head, full, 0.0)
    s_scr[...] = s_new

    mean = _pair_sum(y, hd) * (1.0 / hd)
    var = _pair_sum(jnp.square(y - mean), hd) * (1.0 / hd)
    yn = (y - mean) * lax.rsqrt(var + GN_EPS) * par_pair(lnw_ref) + par_pair(lnb_ref)
    bonus = _pair_sum(rp * kmp * par_pair(rk_ref), hd) * vp
    out = _merge_heads(yn + bonus, sb) * _silu(zg_ref[...])
    o_ref[...] = out.reshape(o_ref.shape).astype(o_ref.dtype)

    @pl.when(j == pl.num_programs(2) - 1)
    def _():
        s4 = s_new.reshape(sb, npair, pw, pw)
        both = jnp.stack([s4[:, :, :hd, :hd], s4[:, :, hd:, hd:]], axis=2)
        sout_ref[...] = both.reshape(sout_ref.shape)


def _rwkv(z3, lw, a, zprev, s0_all, l_in, mix, s_acc, l, depth, muz, kkp, kap, rkp, lnw, lnb,
          *, wa, wb, hbw, plan):
    b, t, _ = z3.shape
    sb, c = plan.sb_rwkv, plan.chunk
    hd = s0_all.shape[-1]
    nb = wb // hbw
    nh = hbw // hd
    off = 2 * wa // hbw
    zspec = lambda o: pl.BlockSpec((sb, c, hbw), lambda i, h, j, o=o: (i, j, o + h))
    pspec = lambda o: pl.BlockSpec((sb, 1, hbw), lambda i, h, j, o=o: (i, 0, o + h))
    mspec = lambda o: pl.BlockSpec((1, hbw), lambda i, h, j, o=o: (0, o + h))
    act = pl.BlockSpec((sb, c, hbw), lambda i, h, j: (i, j, h))
    st = lambda ll: pl.BlockSpec((None, sb, nh, hd, hd), lambda i, h, j, ll=ll: (ll, i, h, 0, 0))
    aliased = [mix] + ([] if s_acc is None else [s_acc])
    n_in = _RWKV_INPUTS
    aliases = {n_in: 0} if s_acc is None else {n_in: 0, n_in + 1: 1}
    out = pl.pallas_call(
        functools.partial(_rwkv_body, n_valid=min(plan.n_valid, c), hd=hd, n_alias=len(aliased)),
        grid=(b // sb, nb, t // c),
        in_specs=[zspec(off), zspec(off + nb), zspec(off + 2 * nb), zspec(off + 3 * nb), act, act,
                  pspec(0), pspec(nb), pspec(2 * nb), st(l_in),
                  mspec(0), mspec(nb), mspec(2 * nb), mspec(0), mspec(0), mspec(0), mspec(0), mspec(0)]
        + [_ANY] * len(aliased),
        out_specs=[_act_spec(plan, t, sb, c, hbw, lambda h: wa // hbw + h, 3), st(l)],
        out_shape=[jax.ShapeDtypeStruct(mix.shape, mix.dtype),
                   jax.ShapeDtypeStruct((depth,) + s0_all.shape[1:], F32)],
        input_output_aliases=aliases,
        scratch_shapes=[pltpu.VMEM((sb * nh // 2, 2 * hd, 2 * hd), F32), pltpu.VMEM((sb, 1, hbw), F32),
                        pltpu.VMEM((sb, 1, hbw), F32), pltpu.VMEM((sb, 1, hbw), F32)],
        compiler_params=_cparams(3),
        name="rwkv7",
    )(z3, z3, z3, z3, lw, a, zprev, zprev, zprev, s0_all, muz, muz, muz, kkp, kap, rkp, lnw, lnb,
      *aliased)
    return out


def _rwkv_steps_body(*refs, n_valid, tpad, hd, n_alias):
    (zr_ref, zk_ref, zv_ref, zg_ref, lw_ref, a_ref, pr_ref, pk_ref, pv_ref, s0_ref,
     mur_ref, muk_ref, muv_ref, kk_ref, ka_ref, rk_ref, lnw_ref, lnb_ref) = refs[:_RWKV_INPUTS]
    o_ref, sout_ref, o_scr, vt_scr, y_scr = refs[_RWKV_INPUTS + n_alias:]
    bsz, cols = pr_ref.shape
    nh = cols // hd
    sout_ref[...] = s0_ref[...]
    o_scr[...] = jnp.zeros_like(o_scr)
    per_col = lambda ref: jnp.broadcast_to(ref[...], (bsz, cols)).T
    lnw_c, lnb_c, rk_c = per_col(lnw_ref), per_col(lnb_ref), per_col(rk_ref)
    step_rows = lambda ref, t: ref[pl.ds(t, bsz, stride=tpad), :]
    prev = (pr_ref[...], pk_ref[...], pv_ref[...])
    for t in range(n_valid):
        z = (step_rows(zr_ref, t), step_rows(zk_ref, t), step_rows(zv_ref, t))
        r, k, v = (zz + (pp - zz) * mu[...] for zz, pp, mu in zip(z, prev, (mur_ref, muk_ref, muv_ref)))
        prev = z
        a = step_rows(a_ref, t)
        w_t = jnp.exp(step_rows(lw_ref, t)).T
        a_t = a.T
        r_t = r.T
        v_t = v.T
        kk_t = (k * kk_ref[...]).T
        km_t = (k * (1.0 + (a - 1.0) * ka_ref[...])).T
        vt_scr[...] = v_t
        for hh in range(nh):
            sl = slice(hh * hd, (hh + 1) * hd)
            kkh = kk_t[sl]
            kkh = kkh / jnp.maximum(jnp.sqrt(jnp.sum(kkh * kkh, axis=0, keepdims=True)), KK_EPS)
            bh, wh, kmh, rh = kkh * a_t[sl], w_t[sl], km_t[sl], r_t[sl]

            def value_row(vi, carry, hh=hh, kkh=kkh, bh=bh, wh=wh, kmh=kmh, rh=rh):
                s = sout_ref[hh, vi]
                sa = jnp.sum(s * kkh, axis=0, keepdims=True)
                s = s * wh - sa * bh + vt_scr[pl.ds(hh * hd + vi, 1), :] * kmh
                sout_ref[hh, vi] = s
                y_scr[pl.ds(hh * hd + vi, 1), :] = jnp.sum(s * rh, axis=0, keepdims=True)
                return carry

            lax.fori_loop(0, hd, value_row, 0, unroll=4)
        y = y_scr[...]
        outs = []
        for hh in range(nh):
            sl = slice(hh * hd, (hh + 1) * hd)
            yh = y[sl]
            mean = jnp.mean(yh, axis=0, keepdims=True)
            var = jnp.mean(jnp.square(yh - mean), axis=0, keepdims=True)
            yn = (yh - mean) * lax.rsqrt(var + GN_EPS) * lnw_c[sl] + lnb_c[sl]
            bonus = jnp.sum(r_t[sl] * km_t[sl] * rk_c[sl], axis=0, keepdims=True) * v_t[sl]
            outs.append(yn + bonus)
        out = jnp.concatenate(outs, axis=0).T * _silu(step_rows(zg_ref, t))
        o_scr[pl.ds(t, bsz, stride=tpad), :] = out
    o_ref[...] = o_scr[...].astype(o_ref.dtype)


def _rwkv_steps(z2, lw2, a2, zprev, s0_all, l_in, mix, s_acc, l, depth, muz, kkp, kap, rkp, lnw, lnb,
                *, wa, wb, plan, tpad):
    bsz = zprev.shape[0]
    rows = z2.shape[0]
    hd = s0_all.shape[-2]
    cols = LANES
    nb = wb // cols
    nh = cols // hd
    off = 2 * wa // cols
    zspec = lambda o: pl.BlockSpec((rows, cols), lambda h, o=o: (0, o + h))
    pspec = lambda o: pl.BlockSpec((bsz, cols), lambda h, o=o: (0, o + h))
    mspec = lambda o: pl.BlockSpec((1, cols), lambda h, o=o: (0, o + h))
    st = lambda ll: pl.BlockSpec((None, nh, hd, hd, bsz), lambda h, ll=ll: (ll, h, 0, 0, 0))
    aliased = [mix] + ([] if s_acc is None else [s_acc])
    n_in = _RWKV_INPUTS
    aliases = {n_in: 0} if s_acc is None else {n_in: 0, n_in + 1: 1}
    return pl.pallas_call(
        functools.partial(_rwkv_steps_body, n_valid=plan.n_valid, tpad=tpad, hd=hd, n_alias=len(aliased)),
        grid=(nb,),
        in_specs=[zspec(off), zspec(off + nb), zspec(off + 2 * nb), zspec(off + 3 * nb),
                  zspec(0), zspec(0), pspec(0), pspec(nb), pspec(2 * nb), st(l_in),
                  mspec(0), mspec(nb), mspec(2 * nb), mspec(0), mspec(0), mspec(0), mspec(0), mspec(0)]
        + [_ANY] * len(aliased),
        out_specs=[pl.BlockSpec((rows, cols), lambda h: (0, wa // cols + h)), st(l)],
        out_shape=[jax.ShapeDtypeStruct(mix.shape, mix.dtype),
                   jax.ShapeDtypeStruct((depth,) + s0_all.shape[1:], F32)],
        input_output_aliases=aliases,
        scratch_shapes=[pltpu.VMEM((rows, cols), F32), pltpu.VMEM((cols, bsz), F32),
                        pltpu.VMEM((cols, bsz), F32)],
        compiler_params=_cparams(1),
        name="rwkv7_steps",
    )(z2, z2, z2, z2, lw2, a2, zprev, zprev, zprev, s0_all, muz, muz, muz, kkp, kap, rkp, lnw, lnb,
      *aliased)


def _hgrn_head(q, f, v, g, s0, lb, ng, n_valid, sub):
    c, dk = q.shape
    sig = jax.nn.sigmoid(f)
    logg = jnp.log(jnp.maximum(lb + (1.0 - lb) * sig, TINY))
    kf = (1.0 - lb) * jax.nn.sigmoid(-f)
    if n_valid < c:
        valid = lax.broadcasted_iota(jnp.int32, (c, dk), 0) < n_valid
        logg = jnp.where(valid, logg, 0.0)
        kf = jnp.where(valid, kf, 0.0)
    bc = _cumsum_time(logg, 0)
    o_inter = _bdot(q * jnp.exp(bc), s0)
    t_row = lax.broadcasted_iota(jnp.int32, (SUBLANES, 1), 0)
    blocks = []
    for bi in range(c // sub):
        lo = bi * sub
        qb, bb, kb, vb = q[lo:lo + sub], bc[lo:lo + sub], kf[lo:lo + sub], v[lo:lo + sub]
        ob = o_inter[lo:lo + sub]
        if bi > 0:
            bs = bc[lo - 1:lo]
            att = _bdot(qb * jnp.exp(bb - bs), kf[:lo] * jnp.exp(bs - bc[:lo]), NT)
            ob = ob + _bdot(att, v[:lo])
        for r0 in range(0, sub, SUBLANES):
            qg, bg, og = qb[r0:r0 + SUBLANES], bb[r0:r0 + SUBLANES], ob[r0:r0 + SUBLANES]
            for s2 in range(r0 + SUBLANES):
                dec = jnp.exp(bg - bb[s2:s2 + 1])
                col = jnp.sum(qg * dec * kb[s2:s2 + 1], axis=-1, keepdims=True)
                if s2 > r0:
                    col = jnp.where(t_row >= s2 - r0, col, 0.0)
                og = og + col * vb[s2:s2 + 1]
            blocks.append(og)
    o = jnp.concatenate(blocks, axis=0) if len(blocks) > 1 else blocks[0]
    b_last = bc[c - 1:c]
    e_col = jnp.broadcast_to(jnp.exp(b_last), (v.shape[1], dk)).T
    s_new = e_col * s0 + _bdot(kf * jnp.exp(b_last - bc), v, TN)
    on = o * lax.rsqrt(jnp.mean(o * o, axis=-1, keepdims=True) + RMS_EPS) * ng
    return on * _silu(g), s_new


_HGRN_INPUTS = 7


def _hgrn_body(*refs, n_valid, dk, sub, n_alias):
    q_ref, f_ref, i_ref, g_ref, s0_ref, lb_ref, ng_ref = refs[:_HGRN_INPUTS]
    o_ref, sout_ref, s_scr, o_scr = refs[_HGRN_INPUTS + n_alias:]
    j = pl.program_id(2)

    @pl.when(j == 0)
    def _():
        s_scr[...] = s0_ref[...]

    sb, c, hcw = q_ref.shape
    nh = hcw // dk
    lb = lb_ref[...]
    ng = ng_ref[...]

    def one(s, carry):
        q, f, v, g = q_ref[s], f_ref[s], i_ref[s], g_ref[s]
        outs = []
        for hh in range(nh):
            sl = slice(hh * dk, (hh + 1) * dk)
            o, s_new = _hgrn_head(q[:, sl], f[:, sl], v[:, sl], g[:, sl], s_scr[s, hh],
                                  lb[:, sl], ng[:, sl], n_valid, sub)
            s_scr[s, hh] = s_new
            outs.append(o)
        o_scr[s] = jnp.concatenate(outs, axis=1) if nh > 1 else outs[0]
        return carry

    if sb == 1:
        one(0, 0)
    else:
        lax.fori_loop(0, sb, one, 0)
    o_ref[...] = o_scr[...].reshape(o_ref.shape).astype(o_ref.dtype)

    @pl.when(j == pl.num_programs(2) - 1)
    def _():
        sout_ref[...] = s_scr[...]


def _hgrn(z3, s0_all, l_in, mix, s_acc, l, depth, lb, ng, *, col0, out_col0, wc, hcw, plan):
    b, t, _ = z3.shape
    sb, c = plan.sb_hgrn, plan.chunk
    dk, dv = s0_all.shape[-2], s0_all.shape[-1]
    nb = wc // hcw
    nh = hcw // dk
    off = col0 // hcw
    zspec = lambda o: pl.BlockSpec((sb, c, hcw), lambda i, h, j, o=o: (i, j, o + h))
    st = lambda ll: pl.BlockSpec((None, sb, nh, dk, dv), lambda i, h, j, ll=ll: (ll, i, h, 0, 0))
    par = pl.BlockSpec((1, hcw), lambda i, h, j: (0, h))
    aliased = [mix] + ([] if s_acc is None else [s_acc])
    n_in = _HGRN_INPUTS
    aliases = {n_in: 0} if s_acc is None else {n_in: 0, n_in + 1: 1}
    return pl.pallas_call(
        functools.partial(_hgrn_body, n_valid=min(plan.n_valid, c), dk=dk, sub=min(2 * SUBLANES, c),
                          n_alias=len(aliased)),
        grid=(b // sb, nb, t // c),
        in_specs=[zspec(off), zspec(off + nb), zspec(off + 2 * nb), zspec(off + 3 * nb), st(l_in),
                  par, par] + [_ANY] * len(aliased),
        out_specs=[_act_spec(plan, t, sb, c, hcw, lambda h: out_col0 // hcw + h, 3), st(l)],
        out_shape=[jax.ShapeDtypeStruct(mix.shape, mix.dtype),
                   jax.ShapeDtypeStruct((depth,) + s0_all.shape[1:], F32)],
        input_output_aliases=aliases,
        scratch_shapes=[pltpu.VMEM((sb, nh, dk, dv), F32), pltpu.VMEM((sb, c, hcw), F32)],
        compiler_params=_cparams(3),
        name="hgrn2",
    )(z3, z3, z3, z3, s0_all, lb, ng, *aliased)


def _post1_body(h_ref, m_ref, g_ref, h1_ref, hn_ref):
    m = m_ref[...].astype(F32)
    h1 = h_ref[...] + m * lax.rsqrt(jnp.mean(m * m, axis=-1, keepdims=True) + RMS_EPS) * g_ref[...]
    h1_ref[...] = h1
    hn_ref[...] = (h1 * lax.rsqrt(jnp.mean(h1 * h1, axis=-1, keepdims=True) + RMS_EPS)).astype(hn_ref.dtype)


def _post2_body(h_ref, p_ref, proj_ref, gate_ref, g_ref, o_ref):
    u = jnp.dot(p_ref[...], proj_ref[...], preferred_element_type=F32)
    x = u * jax.nn.sigmoid(gate_ref[...].astype(F32))
    o_ref[...] = h_ref[...] + x * lax.rsqrt(jnp.mean(x * x, axis=-1, keepdims=True) + RMS_EPS) * g_ref[...]


def _post_ple(h1, p2, proj_all, l, gate, gain, tr=256):
    n, d = h1.shape
    kp = p2.shape[1]
    tr = min(tr, n)
    row = pl.BlockSpec((tr, d), lambda i: (i, 0))
    return pl.pallas_call(
        _post2_body,
        grid=(n // tr,),
        in_specs=[row, pl.BlockSpec((tr, kp), lambda i: (i, 0)),
                  pl.BlockSpec((None, kp, d), lambda i: (l, 0, 0)), row,
                  pl.BlockSpec((1, d), lambda i: (0, 0))],
        out_specs=row,
        out_shape=jax.ShapeDtypeStruct((n, d), F32),
        compiler_params=_cparams(1),
        name="post_ple",
    )(h1, p2, proj_all, gate, gain)


def _rowwise(body, name, arrays, gain, out_dtypes, tr=256):
    n, d = arrays[0].shape
    tr = min(tr, n)
    row = pl.BlockSpec((tr, d), lambda i: (i, 0))
    return pl.pallas_call(
        body,
        grid=(n // tr,),
        in_specs=[row] * len(arrays) + [pl.BlockSpec((1, d), lambda i: (0, 0))],
        out_specs=[row] * len(out_dtypes),
        out_shape=[jax.ShapeDtypeStruct((n, d), dt) for dt in out_dtypes],
        compiler_params=_cparams(1),
        name=name,
    )(*arrays, gain)


def _block_diag_tiles(w):
    nb, bs, _ = w.shape
    per = LANES // bs
    w = w.reshape(nb // per, per, bs, bs)
    eye = jnp.eye(per, dtype=w.dtype)
    return jnp.einsum('tpab,pq->tpaqb', w, eye).reshape(nb // per, LANES, LANES)


def _layer_stack(x, p, st, W, mm, plan, zero_state):
    st_a_h, st_a_conv, st_b_s, st_b_x, st_b_z, st_c_s = st
    b, t, d = x.shape
    depth = W['g_pre'].shape[0]
    wa = st_a_h.shape[-1]
    hd = st_b_s.shape[-1]
    wb = st_b_s.shape[-3] * hd
    dk = st_c_s.shape[-2]
    wc = st_c_s.shape[-3] * dk
    mix_w = wa + wb + wc
    n = b * t
    nv = plan.n_valid
    hbw = math.gcd(math.gcd(2 * wa, wb), math.gcd(wa, 512))
    hcw = math.gcd(math.gcd(2 * wa + 4 * wb, wc), math.gcd(wa + wb, 512))

    lb_soft = jax.nn.softmax(W['hgrn_lb'].astype(F32), axis=0)
    lb_all = jnp.cumsum(lb_soft, axis=0) - lb_soft[0]
    p_act = p.astype(BF16)

    if plan.rwkv_steps:
        st_b_s = jnp.transpose(st_b_s, (0, 2, 3, 4, 1))
    rwkv_params = lambda l: (
        W['rwkv_mu_z'][l][None], W['rwkv_kk'][l][None], W['rwkv_ka'][l][None],
        W['rwkv_rk'][l].reshape(1, wb), W['rwkv_lnx_w'][l][None], W['rwkv_lnx_b'][l][None])

    h = x
    small = [[] for _ in range(4)]
    sb_acc = sc_acc = None
    for l in range(depth):
        ls = 0 if zero_state else l
        xn, lw, a, xlast = _prenorm(
            h, W['g_pre'][l].reshape(1, 1, d), st_b_x[ls][:, None, :],
            W['rwkv_mu_w'][l].reshape(1, 1, d), W['rwkv_mu_a'][l].reshape(1, 1, d),
            W['rwkv_w1'][l], W['rwkv_a1'][l], W['rwkv_w2'][l], W['rwkv_a2'][l],
            W['rwkv_w0'][l][None], W['rwkv_a0'][l][None], plan)
        z3 = mm('w_in', xn.reshape(n, d), l, F32).reshape(b, t, -1)
        cb8 = jnp.pad(st_a_conv[ls], ((0, 0), (SUBLANES - (CONV_W - 1), 0), (0, 0)))
        mix, ha, tail = _rglru(
            z3, st_a_h[ls][:, None, :], cb8, W['conv_a_w'][l][None], W['conv_a_b'][l].reshape(1, 1, wa),
            W['lru_wr_bd'][l], W['lru_br'][l][None], W['lru_wi_bd'][l], W['lru_bi'][l][None],
            W['lru_lambda'][l][None], wa=wa, mix_w=mix_w, plan=plan)
        if plan.rwkv_steps:
            mix, sb_acc = _rwkv_steps(
                z3.reshape(n, -1), lw.reshape(n, wb), a.reshape(n, wb), st_b_z[ls], st_b_s, ls,
                mix, sb_acc, l, depth, *rwkv_params(l), wa=wa, wb=wb, plan=plan, tpad=t)
        else:
            mix, sb_acc = _rwkv(
                z3, lw, a, st_b_z[ls][:, None, :], st_b_s, ls, mix, sb_acc, l, depth,
                *rwkv_params(l), wa=wa, wb=wb, hbw=hbw, plan=plan)
        mix, sc_acc = _hgrn(
            z3, st_c_s, ls, mix, sc_acc, l, depth, lb_all[l][None], W['hgrn_norm_g'][l][None],
            col0=2 * wa + 4 * wb, out_col0=wa + wb, wc=wc, hcw=hcw, plan=plan)
        mo = mm('w_out', mix.reshape(n, mix_w), l, BF16)
        h1, hn = _rowwise(_post1_body, "post_mix", [h.reshape(n, d), mo], W['g_post'][l][None],
                          [F32, BF16])
        gp = mm('ple_gate', hn, l, BF16)
        h = _post_ple(h1, p_act[l].reshape(n, -1), W['ple_proj'], l, gp, W['g_ple'][l][None]).reshape(b, t, d)
        small[0].append(ha[:, 0])
        small[1].append(tail[:, SUBLANES - (CONV_W - 1):])
        small[2].append(xlast[:, 0])
        small[3].append(z3[:, nv - 1, 2 * wa:2 * wa + 3 * wb])
    na_h, na_c, nb_x, nb_z = (jnp.stack(o) for o in small)
    if plan.rwkv_steps:
        sb_acc = jnp.transpose(sb_acc, (0, 4, 1, 2, 3))
    return h, (na_h, na_c, sb_acc, nb_x, nb_z, sc_acc)


def kernel(x_prompt, x_sample, p_prompt, p_sample, state_a_h, state_a_conv, state_b_S,
           state_b_xprev, state_b_zprev, state_c_S, g_pre, g_post, w_in, w_out,
           conv_a_w, conv_a_b, lru_wr, lru_br, lru_wi, lru_bi, lru_lambda,
           rwkv_mu_z, rwkv_mu_w, rwkv_mu_a, rwkv_w0, rwkv_w1, rwkv_w2, rwkv_a0, rwkv_a1,
           rwkv_a2, rwkv_kk, rwkv_ka, rwkv_rk, rwkv_lnx_w, rwkv_lnx_b, hgrn_lb, hgrn_norm_g,
           ple_proj, ple_gate, g_ple):
    depth = w_in.shape[0]
    W = dict(g_pre=g_pre, g_post=g_post, conv_a_w=conv_a_w, conv_a_b=conv_a_b,
             lru_br=lru_br, lru_bi=lru_bi, lru_lambda=lru_lambda, rwkv_mu_z=rwkv_mu_z,
             rwkv_mu_w=rwkv_mu_w, rwkv_mu_a=rwkv_mu_a, rwkv_w0=rwkv_w0, rwkv_a0=rwkv_a0,
             rwkv_kk=rwkv_kk, rwkv_ka=rwkv_ka, rwkv_rk=rwkv_rk, rwkv_lnx_w=rwkv_lnx_w,
             rwkv_lnx_b=rwkv_lnx_b, hgrn_lb=hgrn_lb, hgrn_norm_g=hgrn_norm_g, g_ple=g_ple)
    for name, w in (('rwkv_w1', rwkv_w1), ('rwkv_w2', rwkv_w2), ('rwkv_a1', rwkv_a1),
                    ('rwkv_a2', rwkv_a2), ('ple_proj', ple_proj)):
        W[name] = w.astype(BF16)
    W['lru_wr_bd'] = jnp.stack([_block_diag_tiles(lru_wr[l]) for l in range(depth)]).astype(BF16)
    W['lru_wi_bd'] = jnp.stack([_block_diag_tiles(lru_wi[l]) for l in range(depth)]).astype(BF16)

    bp, tp, d = x_prompt.shape
    bs, ts, _ = x_sample.shape
    dt = x_prompt.dtype
    st_s = (state_a_h, state_a_conv, state_b_S, state_b_xprev, state_b_zprev, state_c_S)
    st_p = tuple(jnp.zeros((1, bp) + s.shape[2:], dt) for s in st_s)

    big = dict(w_in=w_in, w_out=w_out, ple_gate=ple_gate)
    wq = {name: [None] * depth for name in big}

    def mm_sample(name, x, l, out_dtype):
        if x.shape[0] <= MM_TILE:
            y, wq[name][l] = _matmul_wcast(x, big[name], l, out_dtype)
            return y
        wq[name][l] = big[name][l].astype(BF16)
        return _matmul(x, wq[name][l], out_dtype)

    def mm_prompt(name, x, l, out_dtype):
        return _matmul(x, wq[name][l], out_dtype)

    tpad = -(-ts // SUBLANES) * SUBLANES
    pad_t = lambda v, ax: jnp.pad(v, [(0, tpad - ts) if i == ax else (0, 0) for i in range(v.ndim)])
    y_s, out_s = _layer_stack(pad_t(x_sample, 1), pad_t(p_sample, 2), st_s, W, mm_sample,
                              _make_plan(bs, tpad, ts), False)
    y_p, out_p = _layer_stack(x_prompt, p_prompt, st_p, W, mm_prompt, _make_plan(bp, tp, tp), True)
    return (y_p, y_s[:, :ts]) + out_p + out_s
```

```python
import functools
import math
from typing import NamedTuple

import jax
import jax.numpy as jnp
from jax import lax
from jax.experimental import pallas as pl
from jax.experimental.pallas import tpu as pltpu

F32 = jnp.float32
BF16 = jnp.bfloat16

RMS_EPS = 1e-6
GN_EPS = 64e-5
LRU_C = 8.0
TINY = 1e-30
KK_EPS = 1e-12
CONV_W = 4
LOG2E = 1.4426950408889634
HGRN_UNROLL_SEQS = 4

LANES = 128
SUBLANES = 8
BF16_ROWS = 16
VMEM_LIMIT = 52 * 1024 * 1024

NT = (((1,), (1,)), ((), ()))
TN = (((0,), (0,)), ((), ()))
BNT = (((2,), (2,)), ((0,), (0,)))
BNN = (((2,), (1,)), ((0,), (0,)))
BTN = (((1,), (1,)), ((0,), (0,)))


class Plan(NamedTuple):
    n_valid: int
    tt: int
    sb_rows: int
    chunk: int
    sb_rwkv: int
    sb_hgrn: int
    flat_acts: bool
    rwkv_steps: bool


def _make_plan(b, t, n_valid):
    tt = min(t, 128)
    chunk = min(t, 64)
    rows = 128
    sb = max(1, min(b, rows // tt))
    flat = chunk == t
    assert flat or (chunk % BF16_ROWS == 0 and tt % BF16_ROWS == 0)
    return Plan(n_valid=n_valid, tt=tt, sb_rows=sb, chunk=chunk,
                sb_rwkv=math.gcd(b, max(4, rows // chunk)), sb_hgrn=sb if chunk < 64 else math.gcd(b, HGRN_UNROLL_SEQS),
                flat_acts=flat, rwkv_steps=flat and b % LANES == 0 and t <= SUBLANES)


def _act_shape(plan, b, t, width):
    return jax.ShapeDtypeStruct((b * t, width) if plan.flat_acts else (b, t, width), BF16)


def _act_spec(plan, t, sb, rows, width, col, grid_rank):
    if grid_rank == 2:
        if plan.flat_acts:
            return pl.BlockSpec((sb * rows, width), lambda i, j: (i * (t // rows) + j, col(0)))
        return pl.BlockSpec((sb, rows, width), lambda i, j: (i, j, col(0)))
    if plan.flat_acts:
        return pl.BlockSpec((sb * rows, width), lambda i, h, j: (i * (t // rows) + j, col(h)))
    return pl.BlockSpec((sb, rows, width), lambda i, h, j: (i, j, col(h)))


def _cparams(n_axes):
    return pltpu.CompilerParams(dimension_semantics=("arbitrary",) * n_axes,
                                vmem_limit_bytes=VMEM_LIMIT)


def _softplus(x):
    return jnp.maximum(x, 0.0) + jnp.log1p(jnp.exp(-jnp.abs(x)))


def _silu(x):
    return x * jax.nn.sigmoid(x)


def _bdot(a, b, dims=None):
    a = a.astype(BF16)
    b = b.astype(BF16)
    if dims is None:
        return jnp.dot(a, b, preferred_element_type=F32)
    return lax.dot_general(a, b, dims, preferred_element_type=F32)


def _cumsum_time(x, axis):
    n = x.shape[axis]
    idx = lax.broadcasted_iota(jnp.int32, x.shape, axis)
    d = 1
    while d < n:
        x = x + jnp.where(idx >= d, pltpu.roll(x, d, axis), 0.0)
        d *= 2
    return x


_ANY = pl.BlockSpec(memory_space=pl.ANY)


MM_TILE = 1024


def _mm_body(x_ref, w_ref, o_ref):
    o_ref[...] = jnp.dot(x_ref[...], w_ref[...], preferred_element_type=F32).astype(o_ref.dtype)


def _mm_wcast_body(x_ref, w_ref, o_ref, wb_ref):
    wb = w_ref[...].astype(BF16)
    wb_ref[...] = wb
    o_ref[...] = jnp.dot(x_ref[...], wb, preferred_element_type=F32).astype(o_ref.dtype)


def _matmul(x, w, out_dtype):
    m, k = x.shape
    n = w.shape[1]
    tm = math.gcd(m, MM_TILE)
    tn = math.gcd(n, MM_TILE)
    return pl.pallas_call(
        _mm_body,
        grid=(m // tm, n // tn),
        in_specs=[pl.BlockSpec((tm, k), lambda i, j: (i, 0)),
                  pl.BlockSpec((k, tn), lambda i, j: (0, j))],
        out_specs=pl.BlockSpec((tm, tn), lambda i, j: (i, j)),
        out_shape=jax.ShapeDtypeStruct((m, n), out_dtype),
        compiler_params=_cparams(2),
        name="matmul",
    )(x, w)


def _matmul_wcast(x, w_all, l, out_dtype):
    m, k = x.shape
    n = w_all.shape[2]
    tn = math.gcd(n, MM_TILE // 2)
    return pl.pallas_call(
        _mm_wcast_body,
        grid=(n // tn,),
        in_specs=[pl.BlockSpec((m, k), lambda j: (0, 0)),
                  pl.BlockSpec((None, k, tn), lambda j: (l, 0, j))],
        out_specs=[pl.BlockSpec((m, tn), lambda j: (0, j)),
                   pl.BlockSpec((k, tn), lambda j: (0, j))],
        out_shape=[jax.ShapeDtypeStruct((m, n), out_dtype), jax.ShapeDtypeStruct((k, n), BF16)],
        compiler_params=_cparams(1),
        name="matmul_wcast",
    )(x, w_all)


def _prenorm_body(h_ref, g_ref, xprev_ref, muw_ref, mua_ref, w1_ref, a1_ref, w2_ref, a2_ref,
                  w0_ref, a0_ref, xn_ref, lw_ref, a_ref, xlast_ref, carry_ref, *, last_row):
    j = pl.program_id(1)

    @pl.when(j == 0)
    def _():
        carry_ref[...] = xprev_ref[...]

    h = h_ref[...]
    sb, tt, d = h.shape
    xn = h * lax.rsqrt(jnp.mean(h * h, axis=-1, keepdims=True) + RMS_EPS) * g_ref[...]
    xn_ref[...] = xn.reshape(xn_ref.shape).astype(xn_ref.dtype)
    t_idx = lax.broadcasted_iota(jnp.int32, (sb, tt, d), 1)
    prev = jnp.where(t_idx == 0, carry_ref[...], pltpu.roll(xn, 1, 1))
    carry_ref[...] = xn[:, tt - 1:tt, :]

    @pl.when(j == pl.num_programs(1) - 1)
    def _():
        xlast_ref[...] = xn[:, last_row:last_row + 1, :]

    dx = prev - xn
    xw = (xn + dx * muw_ref[...]).reshape(sb * tt, d)
    xa = (xn + dx * mua_ref[...]).reshape(sb * tt, d)
    yw = w0_ref[...] + _bdot(jnp.tanh(_bdot(xw, w1_ref[...])), w2_ref[...])
    wl = -_softplus(-yw) - 0.5
    lw_ref[...] = (-jnp.exp(wl)).reshape(lw_ref.shape)
    ya = a0_ref[...] + _bdot(_bdot(xa, a1_ref[...]), a2_ref[...])
    a_ref[...] = jax.nn.sigmoid(ya).reshape(a_ref.shape)


def _prenorm(h3, g, xprev, muw, mua, w1, a1, w2, a2, w0, a0, plan):
    b, t, d = h3.shape
    sb, tt = plan.sb_rows, plan.tt
    wb = w2.shape[1]
    lw_dim = w1.shape[1]
    row3 = lambda i, j: (i, j, 0)
    par3 = lambda i, j: (0, 0, 0)
    par2 = lambda i, j: (0, 0)
    seq3 = lambda i, j: (i, 0, 0)
    return pl.pallas_call(
        functools.partial(_prenorm_body, last_row=(plan.n_valid - 1) % tt),
        grid=(b // sb, t // tt),
        in_specs=[pl.BlockSpec((sb, tt, d), row3),
                  pl.BlockSpec((1, 1, d), par3),
                  pl.BlockSpec((sb, 1, d), seq3),
                  pl.BlockSpec((1, 1, d), par3),
                  pl.BlockSpec((1, 1, d), par3),
                  pl.BlockSpec((d, lw_dim), par2),
                  pl.BlockSpec((d, lw_dim), par2),
                  pl.BlockSpec((lw_dim, wb), par2),
                  pl.BlockSpec((lw_dim, wb), par2),
                  pl.BlockSpec((1, wb), par2),
                  pl.BlockSpec((1, wb), par2)],
        out_specs=[_act_spec(plan, t, sb, tt, d, lambda h: 0, 2),
                   pl.BlockSpec((sb, tt, wb), row3),
                   pl.BlockSpec((sb, tt, wb), row3),
                   pl.BlockSpec((sb, 1, d), seq3)],
        out_shape=[_act_shape(plan, b, t, d),
                   jax.ShapeDtypeStruct((b, t, wb), F32),
                   jax.ShapeDtypeStruct((b, t, wb), F32),
                   jax.ShapeDtypeStruct((b, 1, d), F32)],
        scratch_shapes=[pltpu.VMEM((sb, 1, d), F32)],
        compiler_params=_cparams(2),
        name="prenorm_lora",
    )(h3, g, xprev, muw, mua, w1, a1, w2, a2, w0, a0)


def _rglru_body(x_ref, gt_ref, h0_ref, cb_ref, cw_ref, cbias_ref, wr_ref, br_ref, wi_ref, bi_ref,
                lam_ref, o_ref, hout_ref, tail_ref, hc_ref, tl_ref, *, nv_last):
    j = pl.program_id(1)

    @pl.when(j == 0)
    def _():
        hc_ref[...] = h0_ref[...]
        tl_ref[...] = cb_ref[...]

    x = x_ref[...]
    sb, tt, w = x.shape
    ext = jnp.concatenate([tl_ref[...], x], axis=1)
    tl_ref[...] = ext[:, tt:tt + SUBLANES, :]
    u = cbias_ref[...]
    for tap in range(CONV_W):
        sh = CONV_W - 1 - tap
        xs = x if sh == 0 else pltpu.roll(ext, sh, 1)[:, SUBLANES:, :]
        u = u + xs * cw_ref[:, tap:tap + 1, :]
    u2 = u.reshape(sb * tt, w)
    rp, ip = [], []
    for m in range(w // LANES):
        um = u2[:, m * LANES:(m + 1) * LANES]
        rp.append(_bdot(um, wr_ref[m]))
        ip.append(_bdot(um, wi_ref[m]))
    r = jax.nn.sigmoid(jnp.concatenate(rp, axis=1) + br_ref[...])
    i = jax.nn.sigmoid(jnp.concatenate(ip, axis=1) + bi_ref[...])
    log_a = -LRU_C * r * _softplus(-lam_ref[...])
    th = jnp.tanh(log_a)
    bcoef = jnp.sqrt(-2.0 * th / (1.0 - th)) * (i * u2)
    a_cum = jnp.exp(log_a).reshape(sb, tt, w)
    b_cum = bcoef.reshape(sb, tt, w)
    t_idx = lax.broadcasted_iota(jnp.int32, (sb, tt, w), 1)
    d = 1
    while d < tt:
        keep = t_idx >= d
        a_prev = jnp.where(keep, pltpu.roll(a_cum, d, 1), 1.0)
        b_prev = jnp.where(keep, pltpu.roll(b_cum, d, 1), 0.0)
        b_cum = a_cum * b_prev + b_cum
        a_cum = a_cum * a_prev
        d *= 2
    hs = a_cum * hc_ref[...] + b_cum
    hc_ref[...] = hs[:, tt - 1:tt, :]
    o_ref[...] = (hs * _silu(gt_ref[...])).reshape(o_ref.shape).astype(o_ref.dtype)

    @pl.when(j == pl.num_programs(1) - 1)
    def _():
        hout_ref[...] = hs[:, nv_last - 1:nv_last, :]
        if nv_last == tt:
            tail_ref[...] = ext[:, tt:tt + SUBLANES, :]
        else:
            tail_ref[...] = pltpu.roll(ext, tt + SUBLANES - nv_last, 1)[:, 0:SUBLANES, :]


def _rglru(z3, h0, cb8, cw, cbias, wr_bd, br, wi_bd, bi, lam, *, wa, mix_w, plan):
    b, t, _ = z3.shape
    sb, tt = plan.sb_rows, plan.tt
    nt = wa // LANES
    par3 = lambda i, j: (0, 0, 0)
    seq3 = lambda i, j: (i, 0, 0)
    return pl.pallas_call(
        functools.partial(_rglru_body, nv_last=(plan.n_valid - 1) % tt + 1),
        grid=(b // sb, t // tt),
        in_specs=[pl.BlockSpec((sb, tt, wa), lambda i, j: (i, j, 0)),
                  pl.BlockSpec((sb, tt, wa), lambda i, j: (i, j, 1)),
                  pl.BlockSpec((sb, 1, wa), seq3),
                  pl.BlockSpec((sb, SUBLANES, wa), seq3),
                  pl.BlockSpec((1, CONV_W, wa), par3),
                  pl.BlockSpec((1, 1, wa), par3),
                  pl.BlockSpec((nt, LANES, LANES), par3),
                  pl.BlockSpec((1, wa), lambda i, j: (0, 0)),
                  pl.BlockSpec((nt, LANES, LANES), par3),
                  pl.BlockSpec((1, wa), lambda i, j: (0, 0)),
                  pl.BlockSpec((1, wa), lambda i, j: (0, 0))],
        out_specs=[_act_spec(plan, t, sb, tt, wa, lambda h: 0, 2),
                   pl.BlockSpec((sb, 1, wa), seq3),
                   pl.BlockSpec((sb, SUBLANES, wa), seq3)],
        out_shape=[_act_shape(plan, b, t, mix_w),
                   jax.ShapeDtypeStruct((b, 1, wa), F32),
                   jax.ShapeDtypeStruct((b, SUBLANES, wa), F32)],
        scratch_shapes=[pltpu.VMEM((sb, 1, wa), F32), pltpu.VMEM((sb, SUBLANES, wa), F32)],
        compiler_params=_cparams(2),
        name="rglru",
    )(z3, z3, h0, cb8, cw, cbias, wr_bd, br, wi_bd, bi, lam)


def _split_heads(x, hd):
    sb, c, w = x.shape
    nh = w // hd
    st = jnp.stack([x[:, :, h * hd:(h + 1) * hd] for h in range(nh)], axis=1)
    return st.reshape(sb * nh, c, hd)


def _merge_heads(x, sb):
    n, c, hd = x.shape
    nh = n // sb
    x4 = x.reshape(sb, nh, c, hd)
    return jnp.concatenate([x4[:, h] for h in range(nh)], axis=-1)


def _pair_diag(x, half):
    lo = lax.broadcasted_iota(jnp.int32, x.shape, 2) < half
    return jnp.concatenate([jnp.where(lo, x, 0.0), jnp.where(lo, 0.0, x)], axis=1)


def _pair_sum(x, half):
    lo = lax.broadcasted_iota(jnp.int32, x.shape, 2) < half
    s_lo = jnp.sum(jnp.where(lo, x, 0.0), axis=-1, keepdims=True)
    s_hi = jnp.sum(jnp.where(lo, 0.0, x), axis=-1, keepdims=True)
    return jnp.where(lo, s_lo, s_hi)


_RWKV_INPUTS = 18


def _rwkv_prep(z_refs, carry_refs, mu_refs, lw, a, kk_p, ka_p, rk_p, n_valid, hd):
    sb, c, hbw = lw.shape
    pw = 2 * hd
    row = lax.broadcasted_iota(jnp.int32, (sb, c, hbw), 1)
    first = row == 0

    def mix(z_ref, carry_ref, mu_ref):
        z = z_ref[...]
        zm = z + (jnp.where(first, carry_ref[...], pltpu.roll(z, 1, 1)) - z) * mu_ref[...]
        carry_ref[...] = z[:, c - 1:c, :]
        return zm

    r, k, v = (mix(zr, cr, mu) for zr, cr, mu in zip(z_refs, carry_refs, mu_refs))
    kk_raw = k * kk_p
    kmod = k * (1.0 + (a - 1.0) * ka_p)
    if n_valid < c:
        valid = row < n_valid
        lw = jnp.where(valid, lw, 0.0)
        kmod = jnp.where(valid, kmod, 0.0)
        v = jnp.where(valid, v, 0.0)
        a = jnp.where(valid, a, 0.0)
    cl = _cumsum_time(lw, 1)

    pairs = lambda x: _split_heads(x, pw)
    rk_pair = jnp.concatenate([_split_heads(rk_p[None], pw)] * sb, axis=0)
    kkp, ap, kmp, vp, rp, clp, lwp = map(pairs, (kk_raw, a, kmod, v, r, cl, lw))
    kkp = kkp / jnp.maximum(jnp.sqrt(_pair_sum(kkp * kkp, hd)), KK_EPS)
    bp = kkp * ap
    cl_last = clp[:, c - 1:c, :]
    e_neg = jnp.exp(-clp)
    e_end = jnp.exp(cl_last - clp)
    return dict(
        lhs=jnp.concatenate([kkp * jnp.exp(clp - lwp), rp * jnp.exp(clp)], axis=1),
        bd_b=_pair_diag(bp * e_neg, hd), bd_k=_pair_diag(kmp * e_neg, hd), bd_v=_pair_diag(vp, hd),
        k_end=kmp * e_end, b_end=bp * e_end, vp=vp,
        bonus=_pair_sum(rp * kmp * rk_pair, hd) * vp,
        e_tot=jnp.exp(cl_last))


def _rwkv_solve(p, s0, lnw_pair, lnb_pair, hd, c):
    pw = 2 * hd
    ti = lax.broadcasted_iota(jnp.int32, (c, 2 * c), 0)
    si = lax.broadcasted_iota(jnp.int32, (c, 2 * c), 1)
    si = jnp.where(si >= c, si - c, si)
    strict = (ti > si)[None]
    lower = (ti >= si)[None]
    lhs = p["lhs"]
    ab = _bdot(lhs, p["bd_b"], BNT)
    ak = _bdot(lhs, p["bd_k"], BNT)
    su = _bdot(lhs, s0, BNT)
    x = su[:, :c] + _bdot(jnp.where(strict, ak[:, :c], 0.0), p["bd_v"], BNN)
    lp = jnp.where(strict, ab[:, :c], 0.0)
    x = x - _bdot(lp, _pair_diag(x, hd), BNN)
    pw2 = 2
    while pw2 < c:
        lp = _bdot(lp, _pair_diag(lp, c), BNN)
        x = x + _bdot(lp, _pair_diag(x, hd), BNN)
        pw2 *= 2
    y = (su[:, c:] + _bdot(jnp.where(lower, ak[:, c:], 0.0), p["bd_v"], BNN)
         - _bdot(jnp.where(lower, ab[:, c:], 0.0), _pair_diag(x, hd), BNN))
    full = s0 * p["e_tot"] + _bdot(p["vp"], p["k_end"], BTN) - _bdot(x, p["b_end"], BTN)
    same_head = ((lax.broadcasted_iota(jnp.int32, (pw, pw), 0) < hd)
                 == (lax.broadcasted_iota(jnp.int32, (pw, pw), 1) < hd))[None]
    s_new = jnp.where(same_head, full, 0.0)
    mean = _pair_sum(y, hd) * (1.0 / hd)
    var = _pair_sum(jnp.square(y - mean), hd) * (1.0 / hd)
    yn = (y - mean) * lax.rsqrt(var + GN_EPS) * lnw_pair + lnb_pair
    return yn + p["bonus"], s_new


def _rwkv_body(*refs, n_valid, hd, n_alias):
    (zr_ref, zk_ref, zv_ref, zg_ref, lw_ref, a_ref, pr_ref, pk_ref, pv_ref, s0_ref,
     mur_ref, muk_ref, muv_ref, kk_ref, ka_ref, rk_ref, lnw_ref, lnb_ref) = refs[:_RWKV_INPUTS]
    o_ref, sout_ref, s_scr, cr_scr, ck_scr, cv_scr = refs[_RWKV_INPUTS + n_alias:]
    j = pl.program_id(2)
    sb, c, hbw = zr_ref.shape
    pw = 2 * hd
    npair = hbw // pw
    n = sb * npair

    @pl.when(j == 0)
    def _():
        s0 = s0_ref[...].reshape(sb, npair, 2, hd, hd)
        zero = jnp.zeros((sb, npair, hd, hd), F32)
        top = jnp.concatenate([s0[:, :, 0], zero], axis=-1)
        bot = jnp.concatenate([zero, s0[:, :, 1]], axis=-1)
        s_scr[...] = jnp.concatenate([top, bot], axis=-2).reshape(n, pw, pw)
        cr_scr[...] = pr_ref[...]
        ck_scr[...] = pk_ref[...]
        cv_scr[...] = pv_ref[...]

    par_pair = lambda ref: jnp.concatenate([_split_heads(ref[...][None], pw)] * sb, axis=0)
    ops = _rwkv_prep((zr_ref, zk_ref, zv_ref), (cr_scr, ck_scr, cv_scr), (mur_ref, muk_ref, muv_ref),
                     lw_ref[...], a_ref[...], kk_ref[...], ka_ref[...], rk_ref[...], n_valid, hd)
    y, s_new = _rwkv_solve(ops, s_scr[...], par_pair(lnw_ref), par_pair(lnb_ref), hd, c)
    s_scr[...] = s_new
    out = _merge_heads(y, sb) * _silu(zg_ref[...])
    o_ref[...] = out.reshape(o_ref.shape).astype(o_ref.dtype)

    @pl.when(j == pl.num_programs(2) - 1)
    def _():
        s4 = s_new.reshape(sb, npair, pw, pw)
        both = jnp.stack([s4[:, :, :hd, :hd], s4[:, :, hd:, hd:]], axis=2)
        sout_ref[...] = both.reshape(sout_ref.shape)


def _rwkv(z3, lw, a, zprev, s0_all, l_in, mix, s_acc, l, depth, muz, kkp, kap, rkp, lnw, lnb,
          *, wa, wb, hbw, plan):
    b, t, _ = z3.shape
    sb, c = plan.sb_rwkv, plan.chunk
    hd = s0_all.shape[-1]
    nb = wb // hbw
    nh = hbw // hd
    off = 2 * wa // hbw
    zspec = lambda o: pl.BlockSpec((sb, c, hbw), lambda i, h, j, o=o: (i, j, o + h))
    pspec = lambda o: pl.BlockSpec((sb, 1, hbw), lambda i, h, j, o=o: (i, 0, o + h))
    mspec = lambda o: pl.BlockSpec((1, hbw), lambda i, h, j, o=o: (0, o + h))
    act = pl.BlockSpec((sb, c, hbw), lambda i, h, j: (i, j, h))
    st = lambda ll: pl.BlockSpec((None, sb, nh, hd, hd), lambda i, h, j, ll=ll: (ll, i, h, 0, 0))
    aliased = [mix] + ([] if s_acc is None else [s_acc])
    n_in = _RWKV_INPUTS
    aliases = {n_in: 0} if s_acc is None else {n_in: 0, n_in + 1: 1}
    out = pl.pallas_call(
        functools.partial(_rwkv_body, n_valid=min(plan.n_valid, c), hd=hd, n_alias=len(aliased)),
        grid=(b // sb, nb, t // c),
        in_specs=[zspec(off), zspec(off + nb), zspec(off + 2 * nb), zspec(off + 3 * nb), act, act,
                  pspec(0), pspec(nb), pspec(2 * nb), st(l_in),
                  mspec(0), mspec(nb), mspec(2 * nb), mspec(0), mspec(0), mspec(0), mspec(0), mspec(0)]
        + [_ANY] * len(aliased),
        out_specs=[_act_spec(plan, t, sb, c, hbw, lambda h: wa // hbw + h, 3), st(l)],
        out_shape=[jax.ShapeDtypeStruct(mix.shape, mix.dtype),
                   jax.ShapeDtypeStruct((depth,) + s0_all.shape[1:], F32)],
        input_output_aliases=aliases,
        scratch_shapes=[pltpu.VMEM((sb * nh // 2, 2 * hd, 2 * hd), F32), pltpu.VMEM((sb, 1, hbw), F32),
                        pltpu.VMEM((sb, 1, hbw), F32), pltpu.VMEM((sb, 1, hbw), F32)],
        compiler_params=_cparams(3),
        name="rwkv7",
    )(z3, z3, z3, z3, lw, a, zprev, zprev, zprev, s0_all, muz, muz, muz, kkp, kap, rkp, lnw, lnb,
      *aliased)
    return out


def _rwkv_steps_body(*refs, n_valid, tpad, hd, n_alias):
    (zr_ref, zk_ref, zv_ref, zg_ref, lw_ref, a_ref, pr_ref, pk_ref, pv_ref, s0_ref,
     mur_ref, muk_ref, muv_ref, kk_ref, ka_ref, rk_ref, lnw_ref, lnb_ref) = refs[:_RWKV_INPUTS]
    o_ref, sout_ref, o_scr, vt_scr, y_scr = refs[_RWKV_INPUTS + n_alias:]
    bsz, cols = pr_ref.shape
    nh = cols // hd
    sout_ref[...] = s0_ref[...]
    o_scr[...] = jnp.zeros_like(o_scr)
    per_col = lambda ref: jnp.broadcast_to(ref[...], (bsz, cols)).T
    lnw_c, lnb_c, rk_c = per_col(lnw_ref), per_col(lnb_ref), per_col(rk_ref)
    step_rows = lambda ref, t: ref[pl.ds(t, bsz, stride=tpad), :]
    prev = (pr_ref[...], pk_ref[...], pv_ref[...])
    for t in range(n_valid):
        z = (step_rows(zr_ref, t), step_rows(zk_ref, t), step_rows(zv_ref, t))
        r, k, v = (zz + (pp - zz) * mu[...] for zz, pp, mu in zip(z, prev, (mur_ref, muk_ref, muv_ref)))
        prev = z
        a = step_rows(a_ref, t)
        w_t = jnp.exp(step_rows(lw_ref, t)).T
        a_t = a.T
        r_t = r.T
        v_t = v.T
        kk_t = (k * kk_ref[...]).T
        km_t = (k * (1.0 + (a - 1.0) * ka_ref[...])).T
        vt_scr[...] = v_t
        for hh in range(nh):
            sl = slice(hh * hd, (hh + 1) * hd)
            kkh = kk_t[sl]
            kkh = kkh / jnp.maximum(jnp.sqrt(jnp.sum(kkh * kkh, axis=0, keepdims=True)), KK_EPS)
            bh, wh, kmh, rh = kkh * a_t[sl], w_t[sl], km_t[sl], r_t[sl]

            def value_row(vi, carry, hh=hh, kkh=kkh, bh=bh, wh=wh, kmh=kmh, rh=rh):
                s = sout_ref[hh, vi]
                sa = jnp.sum(s * kkh, axis=0, keepdims=True)
                s = s * wh - sa * bh + vt_scr[pl.ds(hh * hd + vi, 1), :] * kmh
                sout_ref[hh, vi] = s
                y_scr[pl.ds(hh * hd + vi, 1), :] = jnp.sum(s * rh, axis=0, keepdims=True)
                return carry

            lax.fori_loop(0, hd, value_row, 0, unroll=4)
        y = y_scr[...]
        outs = []
        for hh in range(nh):
            sl = slice(hh * hd, (hh + 1) * hd)
            yh = y[sl]
            mean = jnp.mean(yh, axis=0, keepdims=True)
            var = jnp.mean(jnp.square(yh - mean), axis=0, keepdims=True)
            yn = (yh - mean) * lax.rsqrt(var + GN_EPS) * lnw_c[sl] + lnb_c[sl]
            bonus = jnp.sum(r_t[sl] * km_t[sl] * rk_c[sl], axis=0, keepdims=True) * v_t[sl]
            outs.append(yn + bonus)
        out = jnp.concatenate(outs, axis=0).T * _silu(step_rows(zg_ref, t))
        o_scr[pl.ds(t, bsz, stride=tpad), :] = out
    o_ref[...] = o_scr[...].astype(o_ref.dtype)


def _rwkv_steps(z2, lw2, a2, zprev, s0_all, l_in, mix, s_acc, l, depth, muz, kkp, kap, rkp, lnw, lnb,
                *, wa, wb, plan, tpad):
    bsz = zprev.shape[0]
    rows = z2.shape[0]
    hd = s0_all.shape[-2]
    cols = LANES
    nb = wb // cols
    nh = cols // hd
    off = 2 * wa // cols
    zspec = lambda o: pl.BlockSpec((rows, cols), lambda h, o=o: (0, o + h))
    pspec = lambda o: pl.BlockSpec((bsz, cols), lambda h, o=o: (0, o + h))
    mspec = lambda o: pl.BlockSpec((1, cols), lambda h, o=o: (0, o + h))
    st = lambda ll: pl.BlockSpec((None, nh, hd, hd, bsz), lambda h, ll=ll: (ll, h, 0, 0, 0))
    aliased = [mix] + ([] if s_acc is None else [s_acc])
    n_in = _RWKV_INPUTS
    aliases = {n_in: 0} if s_acc is None else {n_in: 0, n_in + 1: 1}
    return pl.pallas_call(
        functools.partial(_rwkv_steps_body, n_valid=plan.n_valid, tpad=tpad, hd=hd, n_alias=len(aliased)),
        grid=(nb,),
        in_specs=[zspec(off), zspec(off + nb), zspec(off + 2 * nb), zspec(off + 3 * nb),
                  zspec(0), zspec(0), pspec(0), pspec(nb), pspec(2 * nb), st(l_in),
                  mspec(0), mspec(nb), mspec(2 * nb), mspec(0), mspec(0), mspec(0), mspec(0), mspec(0)]
        + [_ANY] * len(aliased),
        out_specs=[pl.BlockSpec((rows, cols), lambda h: (0, wa // cols + h)), st(l)],
        out_shape=[jax.ShapeDtypeStruct(mix.shape, mix.dtype),
                   jax.ShapeDtypeStruct((depth,) + s0_all.shape[1:], F32)],
        input_output_aliases=aliases,
        scratch_shapes=[pltpu.VMEM((rows, cols), F32), pltpu.VMEM((cols, bsz), F32),
                        pltpu.VMEM((cols, bsz), F32)],
        compiler_params=_cparams(1),
        name="rwkv7_steps",
    )(z2, z2, z2, z2, lw2, a2, zprev, zprev, zprev, s0_all, muz, muz, muz, kkp, kap, rkp, lnw, lnb,
      *aliased)


def _hgrn_head(q, f, v, g, s0, lb, ng, n_valid, sub, stage):
    b2_ref, kf_ref, v_ref = stage
    bcast_row = lambda ref, i: ref[pl.ds(i, 1), :]
    c, dk = q.shape
    sig = jax.nn.sigmoid(f)
    logg = jnp.log(jnp.maximum(lb + (1.0 - lb) * sig, TINY))
    kf = (1.0 - lb) * jax.nn.sigmoid(-f)
    if n_valid < c:
        valid = lax.broadcasted_iota(jnp.int32, (c, dk), 0) < n_valid
        logg = jnp.where(valid, logg, 0.0)
        kf = jnp.where(valid, kf, 0.0)
    bc = _cumsum_time(logg, 0)
    b2 = bc * LOG2E
    b2_ref[...] = b2
    kf_ref[...] = kf
    v_ref[...] = v
    o_inter = _bdot(q * jnp.exp(bc), s0)
    pairs, w_rows = [], []
    for lo in range(0, c, sub):
        for r0 in range(lo, lo + sub, SUBLANES):
            qg, bg = q[r0:r0 + SUBLANES], b2[r0:r0 + SUBLANES]
            for src in range(lo, r0 + SUBLANES):
                pairs.append((r0, src))
                w_rows.append(qg * jnp.exp2(bg - bcast_row(b2_ref, src)) * bcast_row(kf_ref, src))
    att_rows = _bdot(jnp.concatenate(w_rows, axis=0), jnp.ones((dk, v.shape[1]), BF16))
    t_row = lax.broadcasted_iota(jnp.int32, (SUBLANES, 1), 0)
    groups = {}
    for u, (r0, src) in enumerate(pairs):
        col = att_rows[u * SUBLANES:(u + 1) * SUBLANES]
        if src > r0:
            col = jnp.where(t_row >= src - r0, col, 0.0)
        term = col * bcast_row(v_ref, src)
        groups[r0] = groups[r0] + term if r0 in groups else term
    blocks = []
    for lo in range(0, c, sub):
        ob = o_inter[lo:lo + sub]
        if lo > 0:
            bs = bc[lo - 1:lo]
            att = _bdot(q[lo:lo + sub] * jnp.exp(bc[lo:lo + sub] - bs), kf[:lo] * jnp.exp(bs - bc[:lo]), NT)
            ob = ob + _bdot(att, v[:lo])
        blocks.append(ob + jnp.concatenate([groups[r0] for r0 in range(lo, lo + sub, SUBLANES)], axis=0))
    o = jnp.concatenate(blocks, axis=0) if len(blocks) > 1 else blocks[0]
    b_last = bc[c - 1:c]
    e_col = jnp.broadcast_to(jnp.exp(b_last), (v.shape[1], dk)).T
    s_new = e_col * s0 + _bdot(kf * jnp.exp(b_last - bc), v, TN)
    on = o * lax.rsqrt(jnp.mean(o * o, axis=-1, keepdims=True) + RMS_EPS) * ng
    return on * _silu(g), s_new


_HGRN_INPUTS = 7


def _hgrn_body(*refs, n_valid, dk, sub, n_alias):
    q_ref, f_ref, i_ref, g_ref, s0_ref, lb_ref, ng_ref = refs[:_HGRN_INPUTS]
    o_ref, sout_ref, s_scr, o_scr, b2_scr, kf_scr, v_scr = refs[_HGRN_INPUTS + n_alias:]
    j = pl.program_id(2)

    @pl.when(j == 0)
    def _():
        s_scr[...] = s0_ref[...]

    sb, c, hcw = q_ref.shape
    nh = hcw // dk
    lb = lb_ref[...]
    ng = ng_ref[...]

    def one(s, carry, slot0=0):
        q, f, v, g = q_ref[s], f_ref[s], i_ref[s], g_ref[s]
        outs = []
        for hh in range(nh):
            sl = slice(hh * dk, (hh + 1) * dk)
            slot = slot0 + hh
            o, s_new = _hgrn_head(q[:, sl], f[:, sl], v[:, sl], g[:, sl], s_scr[s, hh],
                                  lb[:, sl], ng[:, sl], n_valid, sub,
                                  (b2_scr.at[slot], kf_scr.at[slot], v_scr.at[slot]))
            s_scr[s, hh] = s_new
            outs.append(o)
        o_scr[s] = jnp.concatenate(outs, axis=1) if nh > 1 else outs[0]
        return carry

    unroll = math.gcd(sb, HGRN_UNROLL_SEQS)

    def group(gi, carry):
        for u in range(unroll):
            one(gi * unroll + u, carry, u * nh)
        return carry

    if sb == unroll:
        group(0, 0)
    else:
        lax.fori_loop(0, sb // unroll, group, 0)
    o_ref[...] = o_scr[...].reshape(o_ref.shape).astype(o_ref.dtype)

    @pl.when(j == pl.num_programs(2) - 1)
    def _():
        sout_ref[...] = s_scr[...]


def _hgrn(z3, s0_all, l_in, mix, s_acc, l, depth, lb, ng, *, col0, out_col0, wc, hcw, plan):
    b, t, _ = z3.shape
    sb, c = plan.sb_hgrn, plan.chunk
    dk, dv = s0_all.shape[-2], s0_all.shape[-1]
    nb = wc // hcw
    nh = hcw // dk
    off = col0 // hcw
    zspec = lambda o: pl.BlockSpec((sb, c, hcw), lambda i, h, j, o=o: (i, j, o + h))
    st = lambda ll: pl.BlockSpec((None, sb, nh, dk, dv), lambda i, h, j, ll=ll: (ll, i, h, 0, 0))
    par = pl.BlockSpec((1, hcw), lambda i, h, j: (0, h))
    aliased = [mix] + ([] if s_acc is None else [s_acc])
    n_in = _HGRN_INPUTS
    aliases = {n_in: 0} if s_acc is None else {n_in: 0, n_in + 1: 1}
    return pl.pallas_call(
        functools.partial(_hgrn_body, n_valid=min(plan.n_valid, c), dk=dk, sub=min(2 * SUBLANES, c),
                          n_alias=len(aliased)),
        grid=(b // sb, nb, t // c),
        in_specs=[zspec(off), zspec(off + nb), zspec(off + 2 * nb), zspec(off + 3 * nb), st(l_in),
                  par, par] + [_ANY] * len(aliased),
        out_specs=[_act_spec(plan, t, sb, c, hcw, lambda h: out_col0 // hcw + h, 3), st(l)],
        out_shape=[jax.ShapeDtypeStruct(mix.shape, mix.dtype),
                   jax.ShapeDtypeStruct((depth,) + s0_all.shape[1:], F32)],
        input_output_aliases=aliases,
        scratch_shapes=[pltpu.VMEM((sb, nh, dk, dv), F32), pltpu.VMEM((sb, c, hcw), F32)]
        + [pltpu.VMEM((nh * math.gcd(sb, HGRN_UNROLL_SEQS), c, dk), F32)] * 3,
        compiler_params=_cparams(3),
        name="hgrn2",
    )(z3, z3, z3, z3, s0_all, lb, ng, *aliased)


def _post1_body(h_ref, m_ref, g_ref, h1_ref, hn_ref):
    m = m_ref[...].astype(F32)
    h1 = h_ref[...] + m * lax.rsqrt(jnp.mean(m * m, axis=-1, keepdims=True) + RMS_EPS) * g_ref[...]
    h1_ref[...] = h1
    hn_ref[...] = (h1 * lax.rsqrt(jnp.mean(h1 * h1, axis=-1, keepdims=True) + RMS_EPS)).astype(hn_ref.dtype)


def _post2_body(h_ref, p_ref, proj_ref, gate_ref, g_ref, o_ref):
    u = jnp.dot(p_ref[...], proj_ref[...], preferred_element_type=F32)
    x = u * jax.nn.sigmoid(gate_ref[...].astype(F32))
    o_ref[...] = h_ref[...] + x * lax.rsqrt(jnp.mean(x * x, axis=-1, keepdims=True) + RMS_EPS) * g_ref[...]


def _post_ple(h1, p2, proj_all, l, gate, gain, tr=256):
    n, d = h1.shape
    kp = p2.shape[1]
    tr = min(tr, n)
    row = pl.BlockSpec((tr, d), lambda i: (i, 0))
    return pl.pallas_call(
        _post2_body,
        grid=(n // tr,),
        in_specs=[row, pl.BlockSpec((tr, kp), lambda i: (i, 0)),
                  pl.BlockSpec((None, kp, d), lambda i: (l, 0, 0)), row,
                  pl.BlockSpec((1, d), lambda i: (0, 0))],
        out_specs=row,
        out_shape=jax.ShapeDtypeStruct((n, d), F32),
        compiler_params=_cparams(1),
        name="post_ple",
    )(h1, p2, proj_all, gate, gain)


def _rowwise(body, name, arrays, gain, out_dtypes, tr=256):
    n, d = arrays[0].shape
    tr = min(tr, n)
    row = pl.BlockSpec((tr, d), lambda i: (i, 0))
    return pl.pallas_call(
        body,
        grid=(n // tr,),
        in_specs=[row] * len(arrays) + [pl.BlockSpec((1, d), lambda i: (0, 0))],
        out_specs=[row] * len(out_dtypes),
        out_shape=[jax.ShapeDtypeStruct((n, d), dt) for dt in out_dtypes],
        compiler_params=_cparams(1),
        name=name,
    )(*arrays, gain)


def _block_diag_tiles(w):
    nb, bs, _ = w.shape
    per = LANES // bs
    w = w.reshape(nb // per, per, bs, bs)
    eye = jnp.eye(per, dtype=w.dtype)
    return jnp.einsum('tpab,pq->tpaqb', w, eye).reshape(nb // per, LANES, LANES)


def _layer_stack(x, p, st, W, mm, plan, zero_state):
    st_a_h, st_a_conv, st_b_s, st_b_x, st_b_z, st_c_s = st
    b, t, d = x.shape
    depth = W['g_pre'].shape[0]
    wa = st_a_h.shape[-1]
    hd = st_b_s.shape[-1]
    wb = st_b_s.shape[-3] * hd
    dk = st_c_s.shape[-2]
    wc = st_c_s.shape[-3] * dk
    mix_w = wa + wb + wc
    n = b * t
    nv = plan.n_valid
    hbw = math.gcd(math.gcd(2 * wa, wb), math.gcd(wa, 512))
    hcw = math.gcd(math.gcd(2 * wa + 4 * wb, wc), math.gcd(wa + wb, 512))

    lb_soft = jax.nn.softmax(W['hgrn_lb'].astype(F32), axis=0)
    lb_all = jnp.cumsum(lb_soft, axis=0) - lb_soft[0]
    p_act = p.astype(BF16)

    if plan.rwkv_steps:
        st_b_s = jnp.transpose(st_b_s, (0, 2, 3, 4, 1))
    rwkv_params = lambda l: (
        W['rwkv_mu_z'][l][None], W['rwkv_kk'][l][None], W['rwkv_ka'][l][None],
        W['rwkv_rk'][l].reshape(1, wb), W['rwkv_lnx_w'][l][None], W['rwkv_lnx_b'][l][None])

    h = x
    small = [[] for _ in range(4)]
    sb_acc = sc_acc = None
    for l in range(depth):
        ls = 0 if zero_state else l
        xn, lw, a, xlast = _prenorm(
            h, W['g_pre'][l].reshape(1, 1, d), st_b_x[ls][:, None, :],
            W['rwkv_mu_w'][l].reshape(1, 1, d), W['rwkv_mu_a'][l].reshape(1, 1, d),
            W['rwkv_w1'][l], W['rwkv_a1'][l], W['rwkv_w2'][l], W['rwkv_a2'][l],
            W['rwkv_w0'][l][None], W['rwkv_a0'][l][None], plan)
        z3 = mm('w_in', xn.reshape(n, d), l, F32).reshape(b, t, -1)
        cb8 = jnp.pad(st_a_conv[ls], ((0, 0), (SUBLANES - (CONV_W - 1), 0), (0, 0)))
        mix, ha, tail = _rglru(
            z3, st_a_h[ls][:, None, :], cb8, W['conv_a_w'][l][None], W['conv_a_b'][l].reshape(1, 1, wa),
            W['lru_wr_bd'][l], W['lru_br'][l][None], W['lru_wi_bd'][l], W['lru_bi'][l][None],
            W['lru_lambda'][l][None], wa=wa, mix_w=mix_w, plan=plan)
        if plan.rwkv_steps:
            mix, sb_acc = _rwkv_steps(
                z3.reshape(n, -1), lw.reshape(n, wb), a.reshape(n, wb), st_b_z[ls], st_b_s, ls,
                mix, sb_acc, l, depth, *rwkv_params(l), wa=wa, wb=wb, plan=plan, tpad=t)
        else:
            mix, sb_acc = _rwkv(
                z3, lw, a, st_b_z[ls][:, None, :], st_b_s, ls, mix, sb_acc, l, depth,
                *rwkv_params(l), wa=wa, wb=wb, hbw=hbw, plan=plan)
        mix, sc_acc = _hgrn(
            z3, st_c_s, ls, mix, sc_acc, l, depth, lb_all[l][None], W['hgrn_norm_g'][l][None],
            col0=2 * wa + 4 * wb, out_col0=wa + wb, wc=wc, hcw=hcw, plan=plan)
        mo = mm('w_out', mix.reshape(n, mix_w), l, BF16)
        h1, hn = _rowwise(_post1_body, "post_mix", [h.reshape(n, d), mo], W['g_post'][l][None],
                          [F32, BF16])
        gp = mm('ple_gate', hn, l, BF16)
        h = _post_ple(h1, p_act[l].reshape(n, -1), W['ple_proj'], l, gp, W['g_ple'][l][None]).reshape(b, t, d)
        small[0].append(ha[:, 0])
        small[1].append(tail[:, SUBLANES - (CONV_W - 1):])
        small[2].append(xlast[:, 0])
        small[3].append(z3[:, nv - 1, 2 * wa:2 * wa + 3 * wb])
    na_h, na_c, nb_x, nb_z = (jnp.stack(o) for o in small)
    if plan.rwkv_steps:
        sb_acc = jnp.transpose(sb_acc, (0, 4, 1, 2, 3))
    return h, (na_h, na_c, sb_acc, nb_x, nb_z, sc_acc)


def kernel(x_prompt, x_sample, p_prompt, p_sample, state_a_h, state_a_conv, state_b_S,
           state_b_xprev, state_b_zprev, state_c_S, g_pre, g_post, w_in, w_out,
           conv_a_w, conv_a_b, lru_wr, lru_br, lru_wi, lru_bi, lru_lambda,
           rwkv_mu_z, rwkv_mu_w, rwkv_mu_a, rwkv_w0, rwkv_w1, rwkv_w2, rwkv_a0, rwkv_a1,
           rwkv_a2, rwkv_kk, rwkv_ka, rwkv_rk, rwkv_lnx_w, rwkv_lnx_b, hgrn_lb, hgrn_norm_g,
           ple_proj, ple_gate, g_ple):
    depth = w_in.shape[0]
    W = dict(g_pre=g_pre, g_post=g_post, conv_a_w=conv_a_w, conv_a_b=conv_a_b,
             lru_br=lru_br, lru_bi=lru_bi, lru_lambda=lru_lambda, rwkv_mu_z=rwkv_mu_z,
             rwkv_mu_w=rwkv_mu_w, rwkv_mu_a=rwkv_mu_a, rwkv_w0=rwkv_w0, rwkv_a0=rwkv_a0,
             rwkv_kk=rwkv_kk, rwkv_ka=rwkv_ka, rwkv_rk=rwkv_rk, rwkv_lnx_w=rwkv_lnx_w,
             rwkv_lnx_b=rwkv_lnx_b, hgrn_lb=hgrn_lb, hgrn_norm_g=hgrn_norm_g, g_ple=g_ple)
    for name, w in (('rwkv_w1', rwkv_w1), ('rwkv_w2', rwkv_w2), ('rwkv_a1', rwkv_a1),
                    ('rwkv_a2', rwkv_a2), ('ple_proj', ple_proj)):
        W[name] = w.astype(BF16)
    W['lru_wr_bd'] = jnp.stack([_block_diag_tiles(lru_wr[l]) for l in range(depth)]).astype(BF16)
    W['lru_wi_bd'] = jnp.stack([_block_diag_tiles(lru_wi[l]) for l in range(depth)]).astype(BF16)

    bp, tp, d = x_prompt.shape
    bs, ts, _ = x_sample.shape
    dt = x_prompt.dtype
    st_s = (state_a_h, state_a_conv, state_b_S, state_b_xprev, state_b_zprev, state_c_S)
    st_p = tuple(jnp.zeros((1, bp) + s.shape[2:], dt) for s in st_s)

    big = dict(w_in=w_in, w_out=w_out, ple_gate=ple_gate)
    wq = {name: [None] * depth for name in big}

    def mm_sample(name, x, l, out_dtype):
        if x.shape[0] <= MM_TILE:
            y, wq[name][l] = _matmul_wcast(x, big[name], l, out_dtype)
            return y
        wq[name][l] = big[name][l].astype(BF16)
        return _matmul(x, wq[name][l], out_dtype)

    def mm_prompt(name, x, l, out_dtype):
        return _matmul(x, wq[name][l], out_dtype)

    tpad = -(-ts // SUBLANES) * SUBLANES
    pad_t = lambda v, ax: jnp.pad(v, [(0, tpad - ts) if i == ax else (0, 0) for i in range(v.ndim)])
    y_s, out_s = _layer_stack(pad_t(x_sample, 1), pad_t(p_sample, 2), st_s, W, mm_sample,
                              _make_plan(bs, tpad, ts), False)
    y_p, out_p = _layer_stack(x_prompt, p_prompt, st_p, W, mm_prompt, _make_plan(bp, tp, tp), True)
    return (y_p, y_s[:, :ts]) + out_p + out_s
```

```python
import functools
import math
from typing import NamedTuple

import jax
import jax.numpy as jnp
from jax import lax
from jax.experimental import pallas as pl
from jax.experimental.pallas import tpu as pltpu

F32 = jnp.float32
BF16 = jnp.bfloat16

RMS_EPS = 1e-6
GN_EPS = 64e-5
LRU_C = 8.0
TINY = 1e-30
KK_EPS = 1e-12
CONV_W = 4
LOG2E = 1.4426950408889634
NORM_GROUP_ROWS = 128
HGRN_UNROLL_SEQS = 4

LANES = 128
SUBLANES = 8
BF16_ROWS = 16
VMEM_LIMIT = 52 * 1024 * 1024

NT = (((1,), (1,)), ((), ()))
TN = (((0,), (0,)), ((), ()))
BNT = (((2,), (2,)), ((0,), (0,)))
BNN = (((2,), (1,)), ((0,), (0,)))
BTN = (((1,), (1,)), ((0,), (0,)))


class Plan(NamedTuple):
    n_valid: int
    tt: int
    sb_rows: int
    tt_norm: int
    sb_norm: int
    chunk: int
    sb_rwkv: int
    sb_hgrn: int
    flat_acts: bool
    rwkv_steps: bool


def _make_plan(b, t, n_valid):
    tt = min(t, 128)
    chunk = min(t, 64)
    rows = 128
    sb = max(1, min(b, rows // tt))
    tt_norm = min(t, 2 * rows)
    sb_norm = max(1, min(b, 2 * rows // tt_norm))
    flat = chunk == t
    assert flat or (chunk % BF16_ROWS == 0 and tt % BF16_ROWS == 0)
    return Plan(n_valid=n_valid, tt=tt, sb_rows=sb, tt_norm=tt_norm, sb_norm=sb_norm, chunk=chunk,
                sb_rwkv=math.gcd(b, max(4, rows // chunk)),
                sb_hgrn=sb if chunk < 64 else math.gcd(b, HGRN_UNROLL_SEQS),
                flat_acts=flat, rwkv_steps=flat and b % LANES == 0 and t <= SUBLANES)


def _act_shape(plan, b, t, width):
    return jax.ShapeDtypeStruct((b * t, width) if plan.flat_acts else (b, t, width), BF16)


def _act_spec(plan, t, sb, rows, width, col, grid_rank):
    if grid_rank == 2:
        if plan.flat_acts:
            return pl.BlockSpec((sb * rows, width), lambda i, j: (i * (t // rows) + j, col(0)))
        return pl.BlockSpec((sb, rows, width), lambda i, j: (i, j, col(0)))
    if plan.flat_acts:
        return pl.BlockSpec((sb * rows, width), lambda i, h, j: (i * (t // rows) + j, col(h)))
    return pl.BlockSpec((sb, rows, width), lambda i, h, j: (i, j, col(h)))


def _cparams(n_axes):
    return pltpu.CompilerParams(dimension_semantics=("arbitrary",) * n_axes,
                                vmem_limit_bytes=VMEM_LIMIT)


def _softplus(x):
    return jnp.maximum(x, 0.0) + jnp.log1p(jnp.exp(-jnp.abs(x)))


def _silu(x):
    return x * jax.nn.sigmoid(x)


def _bdot(a, b, dims=None):
    a = a.astype(BF16)
    b = b.astype(BF16)
    if dims is None:
        return jnp.dot(a, b, preferred_element_type=F32)
    return lax.dot_general(a, b, dims, preferred_element_type=F32)


def _cumsum_time(x, axis):
    n = x.shape[axis]
    idx = lax.broadcasted_iota(jnp.int32, x.shape, axis)
    d = 1
    while d < n:
        x = x + jnp.where(idx >= d, pltpu.roll(x, d, axis), 0.0)
        d *= 2
    return x


_ANY = pl.BlockSpec(memory_space=pl.ANY)


MM_TILE = 1024


def _mm_body(x_ref, w_ref, o_ref):
    o_ref[...] = jnp.dot(x_ref[...], w_ref[...], preferred_element_type=F32).astype(o_ref.dtype)


def _mm_wcast_body(x_ref, w_ref, o_ref, wb_ref):
    wb = w_ref[...].astype(BF16)
    wb_ref[...] = wb
    o_ref[...] = jnp.dot(x_ref[...], wb, preferred_element_type=F32).astype(o_ref.dtype)


def _matmul(x, w, out_dtype):
    m, k = x.shape
    n = w.shape[1]
    tm = math.gcd(m, MM_TILE)
    tn = math.gcd(n, MM_TILE)
    return pl.pallas_call(
        _mm_body,
        grid=(m // tm, n // tn),
        in_specs=[pl.BlockSpec((tm, k), lambda i, j: (i, 0)),
                  pl.BlockSpec((k, tn), lambda i, j: (0, j))],
        out_specs=pl.BlockSpec((tm, tn), lambda i, j: (i, j)),
        out_shape=jax.ShapeDtypeStruct((m, n), out_dtype),
        compiler_params=_cparams(2),
        name="matmul",
    )(x, w)


def _matmul_wcast(x, w_all, l, out_dtype):
    m, k = x.shape
    n = w_all.shape[2]
    tn = math.gcd(n, MM_TILE // 2)
    return pl.pallas_call(
        _mm_wcast_body,
        grid=(n // tn,),
        in_specs=[pl.BlockSpec((m, k), lambda j: (0, 0)),
                  pl.BlockSpec((None, k, tn), lambda j: (l, 0, j))],
        out_specs=[pl.BlockSpec((m, tn), lambda j: (0, j)),
                   pl.BlockSpec((k, tn), lambda j: (0, j))],
        out_shape=[jax.ShapeDtypeStruct((m, n), out_dtype), jax.ShapeDtypeStruct((k, n), BF16)],
        compiler_params=_cparams(1),
        name="matmul_wcast",
    )(x, w_all)


def _prenorm_body(h_ref, g_ref, xprev_ref, wf_ref, w2_ref, a2_ref, w0_ref, a0_ref,
                  xn_ref, lw_ref, a_ref, xlast_ref, carry_ref, *, last_row):
    j = pl.program_id(1)
    sb, tt, d = h_ref.shape
    r2 = wf_ref.shape[1] // 2
    rank = r2 // 2

    @pl.when(j == 0)
    def _():
        xp = jnp.broadcast_to(xprev_ref[...], (sb, SUBLANES, d)).reshape(sb * SUBLANES, d)
        carry_ref[...] = _bdot(xp, wf_ref[:, r2:]).reshape(sb, SUBLANES, r2)[:, 0:1, :]

    gt = min(tt, NORM_GROUP_ROWS)
    gs = max(1, min(sb, NORM_GROUP_ROWS // gt))
    for s0 in range(0, sb, gs):
        for t0 in range(0, tt, gt):
            h = h_ref[s0:s0 + gs, t0:t0 + gt, :]
            xn = h * lax.rsqrt(jnp.mean(h * h, axis=-1, keepdims=True) + RMS_EPS) * g_ref[...]
            xn_b = xn.reshape(gs * gt, d).astype(BF16)
            if len(xn_ref.shape) == 2:
                xn_ref[s0 * tt + t0:s0 * tt + t0 + gs * gt, :] = xn_b
            else:
                xn_ref[s0:s0 + gs, t0:t0 + gt, :] = xn_b.reshape(gs, gt, d)
            if t0 <= last_row < t0 + gt:
                @pl.when(j == pl.num_programs(1) - 1)
                def _(xn=xn, s0=s0, t0=t0):
                    xlast_ref[s0:s0 + gs] = xn[:, last_row - t0:last_row - t0 + 1, :]

            prod = jnp.dot(xn_b, wf_ref[...], preferred_element_type=F32)
            shifted = prod[:, r2:].reshape(gs, gt, r2)
            t_idx = lax.broadcasted_iota(jnp.int32, (gs, gt, r2), 1)
            prev = jnp.where(t_idx == 0, carry_ref[s0:s0 + gs], pltpu.roll(shifted, 1, 1))
            carry_ref[s0:s0 + gs] = shifted[:, gt - 1:gt, :]
            pre = prod[:, :r2] + prev.reshape(gs * gt, r2)
            yw = w0_ref[...] + _bdot(jnp.tanh(pre[:, :rank]), w2_ref[...])
            wl = -_softplus(-yw) - 0.5
            lw_ref[s0:s0 + gs, t0:t0 + gt, :] = (-jnp.exp(wl)).reshape(gs, gt, -1)
            ya = a0_ref[...] + _bdot(pre[:, rank:], a2_ref[...])
            a_ref[s0:s0 + gs, t0:t0 + gt, :] = jax.nn.sigmoid(ya).reshape(gs, gt, -1)


def _prenorm(h3, g, xprev, wf, w2, a2, w0, a0, plan):
    b, t, d = h3.shape
    sb, tt = plan.sb_norm, plan.tt_norm
    wb = w2.shape[1]
    rank = w2.shape[0]
    row3 = lambda i, j: (i, j, 0)
    par3 = lambda i, j: (0, 0, 0)
    par2 = lambda i, j: (0, 0)
    seq3 = lambda i, j: (i, 0, 0)
    return pl.pallas_call(
        functools.partial(_prenorm_body, last_row=(plan.n_valid - 1) % tt),
        grid=(b // sb, t // tt),
        in_specs=[pl.BlockSpec((sb, tt, d), row3),
                  pl.BlockSpec((1, 1, d), par3),
                  pl.BlockSpec((sb, 1, d), seq3),
                  pl.BlockSpec((d, 4 * rank), par2),
                  pl.BlockSpec((rank, wb), par2),
                  pl.BlockSpec((rank, wb), par2),
                  pl.BlockSpec((1, wb), par2),
                  pl.BlockSpec((1, wb), par2)],
        out_specs=[_act_spec(plan, t, sb, tt, d, lambda h: 0, 2),
                   pl.BlockSpec((sb, tt, wb), row3),
                   pl.BlockSpec((sb, tt, wb), row3),
                   pl.BlockSpec((sb, 1, d), seq3)],
        out_shape=[_act_shape(plan, b, t, d),
                   jax.ShapeDtypeStruct((b, t, wb), F32),
                   jax.ShapeDtypeStruct((b, t, wb), F32),
                   jax.ShapeDtypeStruct((b, 1, d), F32)],
        scratch_shapes=[pltpu.VMEM((sb, 1, 2 * rank), F32)],
        compiler_params=_cparams(2),
        name="prenorm_lora",
    )(h3, g, xprev, wf, w2, a2, w0, a0)


def _rglru_body(x_ref, gt_ref, h0_ref, cb_ref, cw_ref, cbias_ref, wr_ref, br_ref, wi_ref, bi_ref,
                lam_ref, o_ref, hout_ref, tail_ref, hc_ref, tl_ref, *, nv_last):
    j = pl.program_id(1)

    @pl.when(j == 0)
    def _():
        hc_ref[...] = h0_ref[...]
        tl_ref[...] = cb_ref[...]

    x = x_ref[...]
    sb, tt, w = x.shape
    ext = jnp.concatenate([tl_ref[...], x], axis=1)
    tl_ref[...] = ext[:, tt:tt + SUBLANES, :]
    u = cbias_ref[...]
    for tap in range(CONV_W):
        sh = CONV_W - 1 - tap
        xs = x if sh == 0 else pltpu.roll(ext, sh, 1)[:, SUBLANES:, :]
        u = u + xs * cw_ref[:, tap:tap + 1, :]
    u2 = u.reshape(sb * tt, w)
    rp, ip = [], []
    for m in range(w // LANES):
        um = u2[:, m * LANES:(m + 1) * LANES]
        rp.append(_bdot(um, wr_ref[m]))
        ip.append(_bdot(um, wi_ref[m]))
    r = jax.nn.sigmoid(jnp.concatenate(rp, axis=1) + br_ref[...])
    i = jax.nn.sigmoid(jnp.concatenate(ip, axis=1) + bi_ref[...])
    log_a = -LRU_C * r * _softplus(-lam_ref[...])
    th = jnp.tanh(log_a)
    bcoef = jnp.sqrt(-2.0 * th / (1.0 - th)) * (i * u2)
    a_cum = jnp.exp(log_a).reshape(sb, tt, w)
    b_cum = bcoef.reshape(sb, tt, w)
    t_idx = lax.broadcasted_iota(jnp.int32, (sb, tt, w), 1)
    d = 1
    while d < tt:
        keep = t_idx >= d
        a_prev = jnp.where(keep, pltpu.roll(a_cum, d, 1), 1.0)
        b_prev = jnp.where(keep, pltpu.roll(b_cum, d, 1), 0.0)
        b_cum = a_cum * b_prev + b_cum
        a_cum = a_cum * a_prev
        d *= 2
    hs = a_cum * hc_ref[...] + b_cum
    hc_ref[...] = hs[:, tt - 1:tt, :]
    o_ref[...] = (hs * _silu(gt_ref[...])).reshape(o_ref.shape).astype(o_ref.dtype)

    @pl.when(j == pl.num_programs(1) - 1)
    def _():
        hout_ref[...] = hs[:, nv_last - 1:nv_last, :]
        if nv_last == tt:
            tail_ref[...] = ext[:, tt:tt + SUBLANES, :]
        else:
            tail_ref[...] = pltpu.roll(ext, tt + SUBLANES - nv_last, 1)[:, 0:SUBLANES, :]


def _rglru(z3, h0, cb8, cw, cbias, wr_bd, br, wi_bd, bi, lam, *, wa, mix_w, plan):
    b, t, _ = z3.shape
    sb, tt = plan.sb_rows, plan.tt
    nt = wa // LANES
    par3 = lambda i, j: (0, 0, 0)
    seq3 = lambda i, j: (i, 0, 0)
    return pl.pallas_call(
        functools.partial(_rglru_body, nv_last=(plan.n_valid - 1) % tt + 1),
        grid=(b // sb, t // tt),
        in_specs=[pl.BlockSpec((sb, tt, wa), lambda i, j: (i, j, 0)),
                  pl.BlockSpec((sb, tt, wa), lambda i, j: (i, j, 1)),
                  pl.BlockSpec((sb, 1, wa), seq3),
                  pl.BlockSpec((sb, SUBLANES, wa), seq3),
                  pl.BlockSpec((1, CONV_W, wa), par3),
                  pl.BlockSpec((1, 1, wa), par3),
                  pl.BlockSpec((nt, LANES, LANES), par3),
                  pl.BlockSpec((1, wa), lambda i, j: (0, 0)),
                  pl.BlockSpec((nt, LANES, LANES), par3),
                  pl.BlockSpec((1, wa), lambda i, j: (0, 0)),
                  pl.BlockSpec((1, wa), lambda i, j: (0, 0))],
        out_specs=[_act_spec(plan, t, sb, tt, wa, lambda h: 0, 2),
                   pl.BlockSpec((sb, 1, wa), seq3),
                   pl.BlockSpec((sb, SUBLANES, wa), seq3)],
        out_shape=[_act_shape(plan, b, t, mix_w),
                   jax.ShapeDtypeStruct((b, 1, wa), F32),
                   jax.ShapeDtypeStruct((b, SUBLANES, wa), F32)],
        scratch_shapes=[pltpu.VMEM((sb, 1, wa), F32), pltpu.VMEM((sb, SUBLANES, wa), F32)],
        compiler_params=_cparams(2),
        name="rglru",
    )(z3, z3, h0, cb8, cw, cbias, wr_bd, br, wi_bd, bi, lam)


def _split_heads(x, hd):
    sb, c, w = x.shape
    nh = w // hd
    st = jnp.stack([x[:, :, h * hd:(h + 1) * hd] for h in range(nh)], axis=1)
    return st.reshape(sb * nh, c, hd)


def _merge_heads(x, sb):
    n, c, hd = x.shape
    nh = n // sb
    x4 = x.reshape(sb, nh, c, hd)
    return jnp.concatenate([x4[:, h] for h in range(nh)], axis=-1)


def _pair_diag(x, half):
    lo = lax.broadcasted_iota(jnp.int32, x.shape, 2) < half
    return jnp.concatenate([jnp.where(lo, x, 0.0), jnp.where(lo, 0.0, x)], axis=1)


def _pair_sum(x, half):
    lo = lax.broadcasted_iota(jnp.int32, x.shape, 2) < half
    s_lo = jnp.sum(jnp.where(lo, x, 0.0), axis=-1, keepdims=True)
    s_hi = jnp.sum(jnp.where(lo, 0.0, x), axis=-1, keepdims=True)
    return jnp.where(lo, s_lo, s_hi)


_RWKV_INPUTS = 18


def _rwkv_prep(z_refs, carry_refs, mu_refs, lw, a, kk_p, ka_p, rk_p, n_valid, hd):
    sb, c, hbw = lw.shape
    pw = 2 * hd
    row = lax.broadcasted_iota(jnp.int32, (sb, c, hbw), 1)
    first = row == 0

    def mix(z_ref, carry_ref, mu_ref):
        z = z_ref[...]
        zm = z + (jnp.where(first, carry_ref[...], pltpu.roll(z, 1, 1)) - z) * mu_ref[...]
        carry_ref[...] = z[:, c - 1:c, :]
        return zm

    r, k, v = (mix(zr, cr, mu) for zr, cr, mu in zip(z_refs, carry_refs, mu_refs))
    kk_raw = k * kk_p
    kmod = k * (1.0 + (a - 1.0) * ka_p)
    if n_valid < c:
        valid = row < n_valid
        lw = jnp.where(valid, lw, 0.0)
        kmod = jnp.where(valid, kmod, 0.0)
        v = jnp.where(valid, v, 0.0)
        a = jnp.where(valid, a, 0.0)
    cl = _cumsum_time(lw, 1)

    pairs = lambda x: _split_heads(x, pw)
    rk_pair = jnp.concatenate([_split_heads(rk_p[None], pw)] * sb, axis=0)
    kkp, ap, kmp, vp, rp, clp, lwp = map(pairs, (kk_raw, a, kmod, v, r, cl, lw))
    kkp = kkp / jnp.maximum(jnp.sqrt(_pair_sum(kkp * kkp, hd)), KK_EPS)
    bp = kkp * ap
    cl_last = clp[:, c - 1:c, :]
    e_neg = jnp.exp(-clp)
    e_end = jnp.exp(cl_last - clp)
    return dict(
        lhs=jnp.concatenate([kkp * jnp.exp(clp - lwp), rp * jnp.exp(clp)], axis=1),
        bd_b=_pair_diag(bp * e_neg, hd), bd_k=_pair_diag(kmp * e_neg, hd), bd_v=_pair_diag(vp, hd),
        k_end=kmp * e_end, b_end=bp * e_end, vp=vp,
        bonus=_pair_sum(rp * kmp * rk_pair, hd) * vp,
        e_tot=jnp.exp(cl_last))


def _rwkv_solve(p, s0, lnw_pair, lnb_pair, hd, c):
    pw = 2 * hd
    ti = lax.broadcasted_iota(jnp.int32, (c, 2 * c), 0)
    si = lax.broadcasted_iota(jnp.int32, (c, 2 * c), 1)
    si = jnp.where(si >= c, si - c, si)
    strict = (ti > si)[None]
    lower = (ti >= si)[None]
    lhs = p["lhs"]
    ab = _bdot(lhs, p["bd_b"], BNT)
    ak = _bdot(lhs, p["bd_k"], BNT)
    su = _bdot(lhs, s0, BNT)
    x = su[:, :c] + _bdot(jnp.where(strict, ak[:, :c], 0.0), p["bd_v"], BNN)
    lp = jnp.where(strict, ab[:, :c], 0.0)
    x = x - _bdot(lp, _pair_diag(x, hd), BNN)
    pw2 = 2
    while pw2 < c:
        lp = _bdot(lp, _pair_diag(lp, c), BNN)
        x = x + _bdot(lp, _pair_diag(x, hd), BNN)
        pw2 *= 2
    y = (su[:, c:] + _bdot(jnp.where(lower, ak[:, c:], 0.0), p["bd_v"], BNN)
         - _bdot(jnp.where(lower, ab[:, c:], 0.0), _pair_diag(x, hd), BNN))
    full = s0 * p["e_tot"] + _bdot(p["vp"], p["k_end"], BTN) - _bdot(x, p["b_end"], BTN)
    same_head = ((lax.broadcasted_iota(jnp.int32, (pw, pw), 0) < hd)
                 == (lax.broadcasted_iota(jnp.int32, (pw, pw), 1) < hd))[None]
    s_new = jnp.where(same_head, full, 0.0)
    mean = _pair_sum(y, hd) * (1.0 / hd)
    var = _pair_sum(jnp.square(y - mean), hd) * (1.0 / hd)
    yn = (y - mean) * lax.rsqrt(var + GN_EPS) * lnw_pair + lnb_pair
    return yn + p["bonus"], s_new


def _rwkv_body(*refs, n_valid, hd, n_alias):
    (zr_ref, zk_ref, zv_ref, zg_ref, lw_ref, a_ref, pr_ref, pk_ref, pv_ref, s0_ref,
     mur_ref, muk_ref, muv_ref, kk_ref, ka_ref, rk_ref, lnw_ref, lnb_ref) = refs[:_RWKV_INPUTS]
    o_ref, sout_ref, s_scr, cr_scr, ck_scr, cv_scr = refs[_RWKV_INPUTS + n_alias:]
    j = pl.program_id(2)
    sb, c, hbw = zr_ref.shape
    pw = 2 * hd
    npair = hbw // pw
    n = sb * npair

    @pl.when(j == 0)
    def _():
        s0 = s0_ref[...].reshape(sb, npair, 2, hd, hd)
        zero = jnp.zeros((sb, npair, hd, hd), F32)
        top = jnp.concatenate([s0[:, :, 0], zero], axis=-1)
        bot = jnp.concatenate([zero, s0[:, :, 1]], axis=-1)
        s_scr[...] = jnp.concatenate([top, bot], axis=-2).reshape(n, pw, pw)
        cr_scr[...] = pr_ref[...]
        ck_scr[...] = pk_ref[...]
        cv_scr[...] = pv_ref[...]

    par_pair = lambda ref: jnp.concatenate([_split_heads(ref[...][None], pw)] * sb, axis=0)
    ops = _rwkv_prep((zr_ref, zk_ref, zv_ref), (cr_scr, ck_scr, cv_scr), (mur_ref, muk_ref, muv_ref),
                     lw_ref[...], a_ref[...], kk_ref[...], ka_ref[...], rk_ref[...], n_valid, hd)
    y, s_new = _rwkv_solve(ops, s_scr[...], par_pair(lnw_ref), par_pair(lnb_ref), hd, c)
    s_scr[...] = s_new
    out = _merge_heads(y, sb) * _silu(zg_ref[...])
    o_ref[...] = out.reshape(o_ref.shape).astype(o_ref.dtype)

    @pl.when(j == pl.num_programs(2) - 1)
    def _():
        s4 = s_new.reshape(sb, npair, pw, pw)
        both = jnp.stack([s4[:, :, :hd, :hd], s4[:, :, hd:, hd:]], axis=2)
        sout_ref[...] = both.reshape(sout_ref.shape)


def _rwkv(z3, lw, a, zprev, s0_all, l_in, mix, s_acc, l, depth, muz, kkp, kap, rkp, lnw, lnb,
          *, wa, wb, hbw, plan):
    b, t, _ = z3.shape
    sb, c = plan.sb_rwkv, plan.chunk
    hd = s0_all.shape[-1]
    nb = wb // hbw
    nh = hbw // hd
    off = 2 * wa // hbw
    zspec = lambda o: pl.BlockSpec((sb, c, hbw), lambda i, h, j, o=o: (i, j, o + h))
    pspec = lambda o: pl.BlockSpec((sb, 1, hbw), lambda i, h, j, o=o: (i, 0, o + h))
    mspec = lambda o: pl.BlockSpec((1, hbw), lambda i, h, j, o=o: (0, o + h))
    act = pl.BlockSpec((sb, c, hbw), lambda i, h, j: (i, j, h))
    st = lambda ll: pl.BlockSpec((None, sb, nh, hd, hd), lambda i, h, j, ll=ll: (ll, i, h, 0, 0))
    aliased = [mix] + ([] if s_acc is None else [s_acc])
    n_in = _RWKV_INPUTS
    aliases = {n_in: 0} if s_acc is None else {n_in: 0, n_in + 1: 1}
    out = pl.pallas_call(
        functools.partial(_rwkv_body, n_valid=min(plan.n_valid, c), hd=hd, n_alias=len(aliased)),
        grid=(b // sb, nb, t // c),
        in_specs=[zspec(off), zspec(off + nb), zspec(off + 2 * nb), zspec(off + 3 * nb), act, act,
                  pspec(0), pspec(nb), pspec(2 * nb), st(l_in),
                  mspec(0), mspec(nb), mspec(2 * nb), mspec(0), mspec(0), mspec(0), mspec(0), mspec(0)]
        + [_ANY] * len(aliased),
        out_specs=[_act_spec(plan, t, sb, c, hbw, lambda h: wa // hbw + h, 3), st(l)],
        out_shape=[jax.ShapeDtypeStruct(mix.shape, mix.dtype),
                   jax.ShapeDtypeStruct((depth,) + s0_all.shape[1:], F32)],
        input_output_aliases=aliases,
        scratch_shapes=[pltpu.VMEM((sb * nh // 2, 2 * hd, 2 * hd), F32), pltpu.VMEM((sb, 1, hbw), F32),
                        pltpu.VMEM((sb, 1, hbw), F32), pltpu.VMEM((sb, 1, hbw), F32)],
        compiler_params=_cparams(3),
        name="rwkv7",
    )(z3, z3, z3, z3, lw, a, zprev, zprev, zprev, s0_all, muz, muz, muz, kkp, kap, rkp, lnw, lnb,
      *aliased)
    return out


def _rwkv_steps_body(*refs, n_valid, tpad, hd, n_alias):
    (zr_ref, zk_ref, zv_ref, zg_ref, lw_ref, a_ref, pr_ref, pk_ref, pv_ref, s0_ref,
     mur_ref, muk_ref, muv_ref, kk_ref, ka_ref, rk_ref, lnw_ref, lnb_ref) = refs[:_RWKV_INPUTS]
    o_ref, sout_ref, o_scr, vt_scr, y_scr = refs[_RWKV_INPUTS + n_alias:]
    bsz, cols = pr_ref.shape
    nh = cols // hd
    sout_ref[...] = s0_ref[...]
    o_scr[...] = jnp.zeros_like(o_scr)
    per_col = lambda ref: jnp.broadcast_to(ref[...], (bsz, cols)).T
    lnw_c, lnb_c, rk_c = per_col(lnw_ref), per_col(lnb_ref), per_col(rk_ref)
    step_rows = lambda ref, t: ref[pl.ds(t, bsz, stride=tpad), :]
    prev = (pr_ref[...], pk_ref[...], pv_ref[...])
    for t in range(n_valid):
        z = (step_rows(zr_ref, t), step_rows(zk_ref, t), step_rows(zv_ref, t))
        r, k, v = (zz + (pp - zz) * mu[...] for zz, pp, mu in zip(z, prev, (mur_ref, muk_ref, muv_ref)))
        prev = z
        a = step_rows(a_ref, t)
        w_t = jnp.exp(step_rows(lw_ref, t)).T
        a_t = a.T
        r_t = r.T
        v_t = v.T
        kk_t = (k * kk_ref[...]).T
        km_t = (k * (1.0 + (a - 1.0) * ka_ref[...])).T
        vt_scr[...] = v_t
        for hh in range(nh):
            sl = slice(hh * hd, (hh + 1) * hd)
            kkh = kk_t[sl]
            kkh = kkh / jnp.maximum(jnp.sqrt(jnp.sum(kkh * kkh, axis=0, keepdims=True)), KK_EPS)
            bh, wh, kmh, rh = kkh * a_t[sl], w_t[sl], km_t[sl], r_t[sl]

            def value_row(vi, carry, hh=hh, kkh=kkh, bh=bh, wh=wh, kmh=kmh, rh=rh):
                s = sout_ref[hh, vi]
                sa = jnp.sum(s * kkh, axis=0, keepdims=True)
                s = s * wh - sa * bh + vt_scr[pl.ds(hh * hd + vi, 1), :] * kmh
                sout_ref[hh, vi] = s
                y_scr[pl.ds(hh * hd + vi, 1), :] = jnp.sum(s * rh, axis=0, keepdims=True)
                return carry

            lax.fori_loop(0, hd, value_row, 0, unroll=4)
        y = y_scr[...]
        outs = []
        for hh in range(nh):
            sl = slice(hh * hd, (hh + 1) * hd)
            yh = y[sl]
            mean = jnp.mean(yh, axis=0, keepdims=True)
            var = jnp.mean(jnp.square(yh - mean), axis=0, keepdims=True)
            yn = (yh - mean) * lax.rsqrt(var + GN_EPS) * lnw_c[sl] + lnb_c[sl]
            bonus = jnp.sum(r_t[sl] * km_t[sl] * rk_c[sl], axis=0, keepdims=True) * v_t[sl]
            outs.append(yn + bonus)
        out = jnp.concatenate(outs, axis=0).T * _silu(step_rows(zg_ref, t))
        o_scr[pl.ds(t, bsz, stride=tpad), :] = out
    o_ref[...] = o_scr[...].astype(o_ref.dtype)


def _rwkv_steps(z2, lw2, a2, zprev, s0_all, l_in, mix, s_acc, l, depth, muz, kkp, kap, rkp, lnw, lnb,
                *, wa, wb, plan, tpad):
    bsz = zprev.shape[0]
    rows = z2.shape[0]
    hd = s0_all.shape[-2]
    cols = LANES
    nb = wb // cols
    nh = cols // hd
    off = 2 * wa // cols
    zspec = lambda o: pl.BlockSpec((rows, cols), lambda h, o=o: (0, o + h))
    pspec = lambda o: pl.BlockSpec((bsz, cols), lambda h, o=o: (0, o + h))
    mspec = lambda o: pl.BlockSpec((1, cols), lambda h, o=o: (0, o + h))
    st = lambda ll: pl.BlockSpec((None, nh, hd, hd, bsz), lambda h, ll=ll: (ll, h, 0, 0, 0))
    aliased = [mix] + ([] if s_acc is None else [s_acc])
    n_in = _RWKV_INPUTS
    aliases = {n_in: 0} if s_acc is None else {n_in: 0, n_in + 1: 1}
    return pl.pallas_call(
        functools.partial(_rwkv_steps_body, n_valid=plan.n_valid, tpad=tpad, hd=hd, n_alias=len(aliased)),
        grid=(nb,),
        in_specs=[zspec(off), zspec(off + nb), zspec(off + 2 * nb), zspec(off + 3 * nb),
                  zspec(0), zspec(0), pspec(0), pspec(nb), pspec(2 * nb), st(l_in),
                  mspec(0), mspec(nb), mspec(2 * nb), mspec(0), mspec(0), mspec(0), mspec(0), mspec(0)]
        + [_ANY] * len(aliased),
        out_specs=[pl.BlockSpec((rows, cols), lambda h: (0, wa // cols + h)), st(l)],
        out_shape=[jax.ShapeDtypeStruct(mix.shape, mix.dtype),
                   jax.ShapeDtypeStruct((depth,) + s0_all.shape[1:], F32)],
        input_output_aliases=aliases,
        scratch_shapes=[pltpu.VMEM((rows, cols), F32), pltpu.VMEM((cols, bsz), F32),
                        pltpu.VMEM((cols, bsz), F32)],
        compiler_params=_cparams(1),
        name="rwkv7_steps",
    )(z2, z2, z2, z2, lw2, a2, zprev, zprev, zprev, s0_all, muz, muz, muz, kkp, kap, rkp, lnw, lnb,
      *aliased)


def _hgrn_head(q, f, v, g, s0, lb, ng, n_valid, sub, stage):
    b2_ref, kf_ref, v_ref = stage
    bcast_row = lambda ref, i: ref[pl.ds(i, 1), :]
    c, dk = q.shape
    sig = jax.nn.sigmoid(f)
    logg = jnp.log(jnp.maximum(lb + (1.0 - lb) * sig, TINY))
    kf = (1.0 - lb) * jax.nn.sigmoid(-f)
    if n_valid < c:
        valid = lax.broadcasted_iota(jnp.int32, (c, dk), 0) < n_valid
        logg = jnp.where(valid, logg, 0.0)
        kf = jnp.where(valid, kf, 0.0)
    bc = _cumsum_time(logg, 0)
    b2 = bc * LOG2E
    b2_ref[...] = b2
    kf_ref[...] = kf
    v_ref[...] = v
    o_inter = _bdot(q * jnp.exp(bc), s0)
    pairs, w_rows = [], []
    for lo in range(0, c, sub):
        for r0 in range(lo, lo + sub, SUBLANES):
            qg, bg = q[r0:r0 + SUBLANES], b2[r0:r0 + SUBLANES]
            for src in range(lo, r0 + SUBLANES):
                pairs.append((r0, src))
                w_rows.append(qg * jnp.exp2(bg - bcast_row(b2_ref, src)) * bcast_row(kf_ref, src))
    att_rows = _bdot(jnp.concatenate(w_rows, axis=0), jnp.ones((dk, v.shape[1]), BF16))
    t_row = lax.broadcasted_iota(jnp.int32, (SUBLANES, 1), 0)
    groups = {}
    for u, (r0, src) in enumerate(pairs):
        col = att_rows[u * SUBLANES:(u + 1) * SUBLANES]
        if src > r0:
            col = jnp.where(t_row >= src - r0, col, 0.0)
        term = col * bcast_row(v_ref, src)
        groups[r0] = groups[r0] + term if r0 in groups else term
    blocks = []
    for lo in range(0, c, sub):
        ob = o_inter[lo:lo + sub]
        if lo > 0:
            bs = bc[lo - 1:lo]
            att = _bdot(q[lo:lo + sub] * jnp.exp(bc[lo:lo + sub] - bs), kf[:lo] * jnp.exp(bs - bc[:lo]), NT)
            ob = ob + _bdot(att, v[:lo])
        blocks.append(ob + jnp.concatenate([groups[r0] for r0 in range(lo, lo + sub, SUBLANES)], axis=0))
    o = jnp.concatenate(blocks, axis=0) if len(blocks) > 1 else blocks[0]
    b_last = bc[c - 1:c]
    e_col = jnp.broadcast_to(jnp.exp(b_last), (v.shape[1], dk)).T
    s_new = e_col * s0 + _bdot(kf * jnp.exp(b_last - bc), v, TN)
    on = o * lax.rsqrt(jnp.mean(o * o, axis=-1, keepdims=True) + RMS_EPS) * ng
    return on * _silu(g), s_new


_HGRN_INPUTS = 7


def _hgrn_body(*refs, n_valid, dk, sub, n_alias):
    q_ref, f_ref, i_ref, g_ref, s0_ref, lb_ref, ng_ref = refs[:_HGRN_INPUTS]
    o_ref, sout_ref, s_scr, o_scr, b2_scr, kf_scr, v_scr = refs[_HGRN_INPUTS + n_alias:]
    j = pl.program_id(2)

    @pl.when(j == 0)
    def _():
        s_scr[...] = s0_ref[...]

    sb, c, hcw = q_ref.shape
    nh = hcw // dk
    lb = lb_ref[...]
    ng = ng_ref[...]

    def one(s, carry, slot0=0):
        q, f, v, g = q_ref[s], f_ref[s], i_ref[s], g_ref[s]
        outs = []
        for hh in range(nh):
            sl = slice(hh * dk, (hh + 1) * dk)
            slot = slot0 + hh
            o, s_new = _hgrn_head(q[:, sl], f[:, sl], v[:, sl], g[:, sl], s_scr[s, hh],
                                  lb[:, sl], ng[:, sl], n_valid, sub,
                                  (b2_scr.at[slot], kf_scr.at[slot], v_scr.at[slot]))
            s_scr[s, hh] = s_new
            outs.append(o)
        o_scr[s] = jnp.concatenate(outs, axis=1) if nh > 1 else outs[0]
        return carry

    unroll = math.gcd(sb, HGRN_UNROLL_SEQS)

    def group(gi, carry):
        for u in range(unroll):
            one(gi * unroll + u, carry, u * nh)
        return carry

    if sb == unroll:
        group(0, 0)
    else:
        lax.fori_loop(0, sb // unroll, group, 0)
    o_ref[...] = o_scr[...].reshape(o_ref.shape).astype(o_ref.dtype)

    @pl.when(j == pl.num_programs(2) - 1)
    def _():
        sout_ref[...] = s_scr[...]


def _hgrn(z3, s0_all, l_in, mix, s_acc, l, depth, lb, ng, *, col0, out_col0, wc, hcw, plan):
    b, t, _ = z3.shape
    sb, c = plan.sb_hgrn, plan.chunk
    dk, dv = s0_all.shape[-2], s0_all.shape[-1]
    nb = wc // hcw
    nh = hcw // dk
    off = col0 // hcw
    zspec = lambda o: pl.BlockSpec((sb, c, hcw), lambda i, h, j, o=o: (i, j, o + h))
    st = lambda ll: pl.BlockSpec((None, sb, nh, dk, dv), lambda i, h, j, ll=ll: (ll, i, h, 0, 0))
    par = pl.BlockSpec((1, hcw), lambda i, h, j: (0, h))
    aliased = [mix] + ([] if s_acc is None else [s_acc])
    n_in = _HGRN_INPUTS
    aliases = {n_in: 0} if s_acc is None else {n_in: 0, n_in + 1: 1}
    return pl.pallas_call(
        functools.partial(_hgrn_body, n_valid=min(plan.n_valid, c), dk=dk, sub=min(2 * SUBLANES, c),
                          n_alias=len(aliased)),
        grid=(b // sb, nb, t // c),
        in_specs=[zspec(off), zspec(off + nb), zspec(off + 2 * nb), zspec(off + 3 * nb), st(l_in),
                  par, par] + [_ANY] * len(aliased),
        out_specs=[_act_spec(plan, t, sb, c, hcw, lambda h: out_col0 // hcw + h, 3), st(l)],
        out_shape=[jax.ShapeDtypeStruct(mix.shape, mix.dtype),
                   jax.ShapeDtypeStruct((depth,) + s0_all.shape[1:], F32)],
        input_output_aliases=aliases,
        scratch_shapes=[pltpu.VMEM((sb, nh, dk, dv), F32), pltpu.VMEM((sb, c, hcw), F32)]
        + [pltpu.VMEM((nh * math.gcd(sb, HGRN_UNROLL_SEQS), c, dk), F32)] * 3,
        compiler_params=_cparams(3),
        name="hgrn2",
    )(z3, z3, z3, z3, s0_all, lb, ng, *aliased)


def _post1_body(h_ref, m_ref, g_ref, h1_ref, hn_ref):
    m = m_ref[...].astype(F32)
    h1 = h_ref[...] + m * lax.rsqrt(jnp.mean(m * m, axis=-1, keepdims=True) + RMS_EPS) * g_ref[...]
    h1_ref[...] = h1
    hn_ref[...] = (h1 * lax.rsqrt(jnp.mean(h1 * h1, axis=-1, keepdims=True) + RMS_EPS)).astype(hn_ref.dtype)


def _post2_body(h_ref, p_ref, proj_ref, gate_ref, g_ref, o_ref):
    u = jnp.dot(p_ref[...], proj_ref[...], preferred_element_type=F32)
    x = u * jax.nn.sigmoid(gate_ref[...].astype(F32))
    o_ref[...] = h_ref[...] + x * lax.rsqrt(jnp.mean(x * x, axis=-1, keepdims=True) + RMS_EPS) * g_ref[...]


def _post_ple(h1, p2, proj_all, l, gate, gain, tr=256):
    n, d = h1.shape
    kp = p2.shape[1]
    tr = min(tr, n)
    row = pl.BlockSpec((tr, d), lambda i: (i, 0))
    return pl.pallas_call(
        _post2_body,
        grid=(n // tr,),
        in_specs=[row, pl.BlockSpec((tr, kp), lambda i: (i, 0)),
                  pl.BlockSpec((None, kp, d), lambda i: (l, 0, 0)), row,
                  pl.BlockSpec((1, d), lambda i: (0, 0))],
        out_specs=row,
        out_shape=jax.ShapeDtypeStruct((n, d), F32),
        compiler_params=_cparams(1),
        name="post_ple",
    )(h1, p2, proj_all, gate, gain)


def _rowwise(body, name, arrays, gain, out_dtypes, tr=256):
    n, d = arrays[0].shape
    tr = min(tr, n)
    row = pl.BlockSpec((tr, d), lambda i: (i, 0))
    return pl.pallas_call(
        body,
        grid=(n // tr,),
        in_specs=[row] * len(arrays) + [pl.BlockSpec((1, d), lambda i: (0, 0))],
        out_specs=[row] * len(out_dtypes),
        out_shape=[jax.ShapeDtypeStruct((n, d), dt) for dt in out_dtypes],
        compiler_params=_cparams(1),
        name=name,
    )(*arrays, gain)


def _block_diag_tiles(w):
    nb, bs, _ = w.shape
    per = LANES // bs
    w = w.reshape(nb // per, per, bs, bs)
    eye = jnp.eye(per, dtype=w.dtype)
    return jnp.einsum('tpab,pq->tpaqb', w, eye).reshape(nb // per, LANES, LANES)


def _layer_stack(x, p, st, W, mm, plan, zero_state):
    st_a_h, st_a_conv, st_b_s, st_b_x, st_b_z, st_c_s = st
    b, t, d = x.shape
    depth = W['g_pre'].shape[0]
    wa = st_a_h.shape[-1]
    hd = st_b_s.shape[-1]
    wb = st_b_s.shape[-3] * hd
    dk = st_c_s.shape[-2]
    wc = st_c_s.shape[-3] * dk
    mix_w = wa + wb + wc
    n = b * t
    nv = plan.n_valid
    hbw = math.gcd(math.gcd(2 * wa, wb), math.gcd(wa, 512))
    hcw = math.gcd(math.gcd(2 * wa + 4 * wb, wc), math.gcd(wa + wb, 512))

    lb_soft = jax.nn.softmax(W['hgrn_lb'].astype(F32), axis=0)
    lb_all = jnp.cumsum(lb_soft, axis=0) - lb_soft[0]
    p_act = p.astype(BF16)

    if plan.rwkv_steps:
        st_b_s = jnp.transpose(st_b_s, (0, 2, 3, 4, 1))
    rwkv_params = lambda l: (
        W['rwkv_mu_z'][l][None], W['rwkv_kk'][l][None], W['rwkv_ka'][l][None],
        W['rwkv_rk'][l].reshape(1, wb), W['rwkv_lnx_w'][l][None], W['rwkv_lnx_b'][l][None])

    h = x
    small = [[] for _ in range(4)]
    sb_acc = sc_acc = None
    for l in range(depth):
        ls = 0 if zero_state else l
        xn, lw, a, xlast = _prenorm(
            h, W['g_pre'][l].reshape(1, 1, d), st_b_x[ls][:, None, :], W['rwkv_lora_in'][l],
            W['rwkv_w2'][l], W['rwkv_a2'][l], W['rwkv_w0'][l][None], W['rwkv_a0'][l][None], plan)
        z3 = mm('w_in', xn.reshape(n, d), l, F32).reshape(b, t, -1)
        cb8 = jnp.pad(st_a_conv[ls], ((0, 0), (SUBLANES - (CONV_W - 1), 0), (0, 0)))
        mix, ha, tail = _rglru(
            z3, st_a_h[ls][:, None, :], cb8, W['conv_a_w'][l][None], W['conv_a_b'][l].reshape(1, 1, wa),
            W['lru_wr_bd'][l], W['lru_br'][l][None], W['lru_wi_bd'][l], W['lru_bi'][l][None],
            W['lru_lambda'][l][None], wa=wa, mix_w=mix_w, plan=plan)
        if plan.rwkv_steps:
            mix, sb_acc = _rwkv_steps(
                z3.reshape(n, -1), lw.reshape(n, wb), a.reshape(n, wb), st_b_z[ls], st_b_s, ls,
                mix, sb_acc, l, depth, *rwkv_params(l), wa=wa, wb=wb, plan=plan, tpad=t)
        else:
            mix, sb_acc = _rwkv(
                z3, lw, a, st_b_z[ls][:, None, :], st_b_s, ls, mix, sb_acc, l, depth,
                *rwkv_params(l), wa=wa, wb=wb, hbw=hbw, plan=plan)
        mix, sc_acc = _hgrn(
            z3, st_c_s, ls, mix, sc_acc, l, depth, lb_all[l][None], W['hgrn_norm_g'][l][None],
            col0=2 * wa + 4 * wb, out_col0=wa + wb, wc=wc, hcw=hcw, plan=plan)
        mo = mm('w_out', mix.reshape(n, mix_w), l, BF16)
        h1, hn = _rowwise(_post1_body, "post_mix", [h.reshape(n, d), mo], W['g_post'][l][None],
                          [F32, BF16])
        gp = mm('ple_gate', hn, l, BF16)
        h = _post_ple(h1, p_act[l].reshape(n, -1), W['ple_proj'], l, gp, W['g_ple'][l][None]).reshape(b, t, d)
        small[0].append(ha[:, 0])
        small[1].append(tail[:, SUBLANES - (CONV_W - 1):])
        small[2].append(xlast[:, 0])
        small[3].append(z3[:, nv - 1, 2 * wa:2 * wa + 3 * wb])
    na_h, na_c, nb_x, nb_z = (jnp.stack(o) for o in small)
    if plan.rwkv_steps:
        sb_acc = jnp.transpose(sb_acc, (0, 4, 1, 2, 3))
    return h, (na_h, na_c, sb_acc, nb_x, nb_z, sc_acc)


def kernel(x_prompt, x_sample, p_prompt, p_sample, state_a_h, state_a_conv, state_b_S,
           state_b_xprev, state_b_zprev, state_c_S, g_pre, g_post, w_in, w_out,
           conv_a_w, conv_a_b, lru_wr, lru_br, lru_wi, lru_bi, lru_lambda,
           rwkv_mu_z, rwkv_mu_w, rwkv_mu_a, rwkv_w0, rwkv_w1, rwkv_w2, rwkv_a0, rwkv_a1,
           rwkv_a2, rwkv_kk, rwkv_ka, rwkv_rk, rwkv_lnx_w, rwkv_lnx_b, hgrn_lb, hgrn_norm_g,
           ple_proj, ple_gate, g_ple):
    depth = w_in.shape[0]
    W = dict(g_pre=g_pre, g_post=g_post, conv_a_w=conv_a_w, conv_a_b=conv_a_b,
             lru_br=lru_br, lru_bi=lru_bi, lru_lambda=lru_lambda, rwkv_mu_z=rwkv_mu_z,
             rwkv_mu_w=rwkv_mu_w, rwkv_mu_a=rwkv_mu_a, rwkv_w0=rwkv_w0, rwkv_a0=rwkv_a0,
             rwkv_kk=rwkv_kk, rwkv_ka=rwkv_ka, rwkv_rk=rwkv_rk, rwkv_lnx_w=rwkv_lnx_w,
             rwkv_lnx_b=rwkv_lnx_b, hgrn_lb=hgrn_lb, hgrn_norm_g=hgrn_norm_g, g_ple=g_ple)
    for name, w in (('rwkv_w2', rwkv_w2), ('rwkv_a2', rwkv_a2), ('ple_proj', ple_proj)):
        W[name] = w.astype(BF16)
    mw, ma = rwkv_mu_w[:, :, None], rwkv_mu_a[:, :, None]
    W['rwkv_lora_in'] = jnp.concatenate(
        [(1.0 - mw) * rwkv_w1, (1.0 - ma) * rwkv_a1, mw * rwkv_w1, ma * rwkv_a1], axis=-1).astype(BF16)
    W['lru_wr_bd'] = jnp.stack([_block_diag_tiles(lru_wr[l]) for l in range(depth)]).astype(BF16)
    W['lru_wi_bd'] = jnp.stack([_block_diag_tiles(lru_wi[l]) for l in range(depth)]).astype(BF16)

    bp, tp, d = x_prompt.shape
    bs, ts, _ = x_sample.shape
    dt = x_prompt.dtype
    st_s = (state_a_h, state_a_conv, state_b_S, state_b_xprev, state_b_zprev, state_c_S)
    st_p = tuple(jnp.zeros((1, bp) + s.shape[2:], dt) for s in st_s)

    big = dict(w_in=w_in, w_out=w_out, ple_gate=ple_gate)
    wq = {name: [None] * depth for name in big}

    def mm_sample(name, x, l, out_dtype):
        if x.shape[0] <= MM_TILE:
            y, wq[name][l] = _matmul_wcast(x, big[name], l, out_dtype)
            return y
        wq[name][l] = big[name][l].astype(BF16)
        return _matmul(x, wq[name][l], out_dtype)

    def mm_prompt(name, x, l, out_dtype):
        return _matmul(x, wq[name][l], out_dtype)

    tpad = -(-ts // SUBLANES) * SUBLANES
    pad_t = lambda v, ax: jnp.pad(v, [(0, tpad - ts) if i == ax else (0, 0) for i in range(v.ndim)])
    y_s, out_s = _layer_stack(pad_t(x_sample, 1), pad_t(p_sample, 2), st_s, W, mm_sample,
                              _make_plan(bs, tpad, ts), False)
    y_p, out_p = _layer_stack(x_prompt, p_prompt, st_p, W, mm_prompt, _make_plan(bp, tp, tp), True)
    return (y_p, y_s[:, :ts]) + out_p + out_s
```

```python
import functools
import math
from typing import NamedTuple

import jax
import jax.numpy as jnp
from jax import lax
from jax.experimental import pallas as pl
from jax.experimental.pallas import tpu as pltpu

F32 = jnp.float32
BF16 = jnp.bfloat16

RMS_EPS = 1e-6
GN_EPS = 64e-5
LRU_C = 8.0
TINY = 1e-30
KK_EPS = 1e-12
CONV_W = 4
LOG2E = 1.4426950408889634
NORM_GROUP_ROWS = 128
HGRN_UNROLL_SEQS = 4

LANES = 128
SUBLANES = 8
BF16_ROWS = 16
VMEM_LIMIT = 52 * 1024 * 1024

NT = (((1,), (1,)), ((), ()))
TN = (((0,), (0,)), ((), ()))
BNT = (((2,), (2,)), ((0,), (0,)))
BNN = (((2,), (1,)), ((0,), (0,)))
BTN = (((1,), (1,)), ((0,), (0,)))


class Plan(NamedTuple):
    n_valid: int
    tt: int
    sb_rows: int
    tt_norm: int
    sb_norm: int
    chunk: int
    sb_rwkv: int
    sb_hgrn: int
    flat_acts: bool
    rwkv_steps: bool


def _make_plan(b, t, n_valid):
    tt = min(t, 128)
    chunk = min(t, 64)
    rows = 128
    sb = max(1, min(b, rows // tt))
    tt_norm = min(t, 2 * rows)
    sb_norm = max(1, min(b, 2 * rows // tt_norm))
    flat = chunk == t
    assert flat or (chunk % BF16_ROWS == 0 and tt % BF16_ROWS == 0)
    return Plan(n_valid=n_valid, tt=tt, sb_rows=sb, tt_norm=tt_norm, sb_norm=sb_norm, chunk=chunk,
                sb_rwkv=math.gcd(b, max(4, rows // chunk)),
                sb_hgrn=sb if chunk < 64 else math.gcd(b, HGRN_UNROLL_SEQS),
                flat_acts=flat, rwkv_steps=flat and b % LANES == 0 and t <= SUBLANES)


def _act_shape(plan, b, t, width):
    return jax.ShapeDtypeStruct((b * t, width) if plan.flat_acts else (b, t, width), BF16)


def _act_spec(plan, t, sb, rows, width, col, grid_rank):
    if grid_rank == 2:
        if plan.flat_acts:
            return pl.BlockSpec((sb * rows, width), lambda i, j: (i * (t // rows) + j, col(0)))
        return pl.BlockSpec((sb, rows, width), lambda i, j: (i, j, col(0)))
    if plan.flat_acts:
        return pl.BlockSpec((sb * rows, width), lambda i, h, j: (i * (t // rows) + j, col(h)))
    return pl.BlockSpec((sb, rows, width), lambda i, h, j: (i, j, col(h)))


def _cparams(n_axes):
    return pltpu.CompilerParams(dimension_semantics=("arbitrary",) * n_axes,
                                vmem_limit_bytes=VMEM_LIMIT)


def _softplus(x):
    return jnp.maximum(x, 0.0) + jnp.log1p(jnp.exp(-jnp.abs(x)))


def _silu(x):
    return x * jax.nn.sigmoid(x)


def _bdot(a, b, dims=None):
    a = a.astype(BF16)
    b = b.astype(BF16)
    if dims is None:
        return jnp.dot(a, b, preferred_element_type=F32)
    return lax.dot_general(a, b, dims, preferred_element_type=F32)


def _cumsum_time(x, axis):
    n = x.shape[axis]
    idx = lax.broadcasted_iota(jnp.int32, x.shape, axis)
    d = 1
    while d < n:
        x = x + jnp.where(idx >= d, pltpu.roll(x, d, axis), 0.0)
        d *= 2
    return x


_ANY = pl.BlockSpec(memory_space=pl.ANY)


MM_TILE = 1024


def _mm_body(x_ref, w_ref, o_ref):
    o_ref[...] = jnp.dot(x_ref[...], w_ref[...], preferred_element_type=F32).astype(o_ref.dtype)


def _mm_wcast_body(x_ref, w_ref, o_ref, wb_ref):
    wb = w_ref[...].astype(BF16)
    wb_ref[...] = wb
    o_ref[...] = jnp.dot(x_ref[...], wb, preferred_element_type=F32).astype(o_ref.dtype)


def _matmul(x, w, out_dtype):
    m, k = x.shape
    n = w.shape[1]
    tm = math.gcd(m, MM_TILE)
    tn = math.gcd(n, MM_TILE)
    return pl.pallas_call(
        _mm_body,
        grid=(m // tm, n // tn),
        in_specs=[pl.BlockSpec((tm, k), lambda i, j: (i, 0)),
                  pl.BlockSpec((k, tn), lambda i, j: (0, j))],
        out_specs=pl.BlockSpec((tm, tn), lambda i, j: (i, j)),
        out_shape=jax.ShapeDtypeStruct((m, n), out_dtype),
        compiler_params=_cparams(2),
        name="matmul",
    )(x, w)


def _matmul_wcast(x, w_all, l, out_dtype):
    m, k = x.shape
    n = w_all.shape[2]
    tn = math.gcd(n, MM_TILE // 2)
    return pl.pallas_call(
        _mm_wcast_body,
        grid=(n // tn,),
        in_specs=[pl.BlockSpec((m, k), lambda j: (0, 0)),
                  pl.BlockSpec((None, k, tn), lambda j: (l, 0, j))],
        out_specs=[pl.BlockSpec((m, tn), lambda j: (0, j)),
                   pl.BlockSpec((k, tn), lambda j: (0, j))],
        out_shape=[jax.ShapeDtypeStruct((m, n), out_dtype), jax.ShapeDtypeStruct((k, n), BF16)],
        compiler_params=_cparams(1),
        name="matmul_wcast",
    )(x, w_all)


def _prenorm_body(h_ref, g_ref, xprev_ref, wf_ref, w2_ref, a2_ref, w0_ref, a0_ref,
                  xn_ref, lw_ref, a_ref, xlast_ref, carry_ref, *, last_row):
    j = pl.program_id(1)
    sb, tt, d = h_ref.shape
    r2 = wf_ref.shape[1] // 2
    rank = r2 // 2

    @pl.when(j == 0)
    def _():
        xp = jnp.broadcast_to(xprev_ref[...], (sb, SUBLANES, d)).reshape(sb * SUBLANES, d)
        carry_ref[...] = _bdot(xp, wf_ref[:, r2:]).reshape(sb, SUBLANES, r2)[:, 0:1, :]

    gt = min(tt, NORM_GROUP_ROWS)
    gs = max(1, min(sb, NORM_GROUP_ROWS // gt))
    for s0 in range(0, sb, gs):
        for t0 in range(0, tt, gt):
            h = h_ref[s0:s0 + gs, t0:t0 + gt, :]
            xn = h * lax.rsqrt(jnp.mean(h * h, axis=-1, keepdims=True) + RMS_EPS) * g_ref[...]
            xn_b = xn.reshape(gs * gt, d).astype(BF16)
            if len(xn_ref.shape) == 2:
                xn_ref[s0 * tt + t0:s0 * tt + t0 + gs * gt, :] = xn_b
            else:
                xn_ref[s0:s0 + gs, t0:t0 + gt, :] = xn_b.reshape(gs, gt, d)
            if t0 <= last_row < t0 + gt:
                @pl.when(j == pl.num_programs(1) - 1)
                def _(xn=xn, s0=s0, t0=t0):
                    xlast_ref[s0:s0 + gs] = xn[:, last_row - t0:last_row - t0 + 1, :]

            prod = jnp.dot(xn_b, wf_ref[...], preferred_element_type=F32)
            shifted = prod[:, r2:].reshape(gs, gt, r2)
            t_idx = lax.broadcasted_iota(jnp.int32, (gs, gt, r2), 1)
            prev = jnp.where(t_idx == 0, carry_ref[s0:s0 + gs], pltpu.roll(shifted, 1, 1))
            carry_ref[s0:s0 + gs] = shifted[:, gt - 1:gt, :]
            pre = prod[:, :r2] + prev.reshape(gs * gt, r2)
            yw = w0_ref[...] + _bdot(jnp.tanh(pre[:, :rank]), w2_ref[...])
            wl = -_softplus(-yw) - 0.5
            lw_ref[s0:s0 + gs, t0:t0 + gt, :] = (-jnp.exp(wl)).reshape(gs, gt, -1)
            ya = a0_ref[...] + _bdot(pre[:, rank:], a2_ref[...])
            a_ref[s0:s0 + gs, t0:t0 + gt, :] = jax.nn.sigmoid(ya).reshape(gs, gt, -1)


def _prenorm(h3, g, xprev, wf, w2, a2, w0, a0, plan):
    b, t, d = h3.shape
    sb, tt = plan.sb_norm, plan.tt_norm
    wb = w2.shape[1]
    rank = w2.shape[0]
    row3 = lambda i, j: (i, j, 0)
    par3 = lambda i, j: (0, 0, 0)
    par2 = lambda i, j: (0, 0)
    seq3 = lambda i, j: (i, 0, 0)
    return pl.pallas_call(
        functools.partial(_prenorm_body, last_row=(plan.n_valid - 1) % tt),
        grid=(b // sb, t // tt),
        in_specs=[pl.BlockSpec((sb, tt, d), row3),
                  pl.BlockSpec((1, 1, d), par3),
                  pl.BlockSpec((sb, 1, d), seq3),
                  pl.BlockSpec((d, 4 * rank), par2),
                  pl.BlockSpec((rank, wb), par2),
                  pl.BlockSpec((rank, wb), par2),
                  pl.BlockSpec((1, wb), par2),
                  pl.BlockSpec((1, wb), par2)],
        out_specs=[_act_spec(plan, t, sb, tt, d, lambda h: 0, 2),
                   pl.BlockSpec((sb, tt, wb), row3),
                   pl.BlockSpec((sb, tt, wb), row3),
                   pl.BlockSpec((sb, 1, d), seq3)],
        out_shape=[_act_shape(plan, b, t, d),
                   jax.ShapeDtypeStruct((b, t, wb), F32),
                   jax.ShapeDtypeStruct((b, t, wb), F32),
                   jax.ShapeDtypeStruct((b, 1, d), F32)],
        scratch_shapes=[pltpu.VMEM((sb, 1, 2 * rank), F32)],
        compiler_params=_cparams(2),
        name="prenorm_lora",
    )(h3, g, xprev, wf, w2, a2, w0, a0)


def _rglru_body(x_ref, gt_ref, h0_ref, cb_ref, cw_ref, cbias_ref, wr_ref, br_ref, wi_ref, bi_ref,
                lam_ref, o_ref, hout_ref, tail_ref, hc_ref, tl_ref, *, nv_last):
    j = pl.program_id(1)

    @pl.when(j == 0)
    def _():
        hc_ref[...] = h0_ref[...]
        tl_ref[...] = cb_ref[...]

    x = x_ref[...]
    sb, tt, w = x.shape
    ext = jnp.concatenate([tl_ref[...], x], axis=1)
    tl_ref[...] = ext[:, tt:tt + SUBLANES, :]
    u = cbias_ref[...]
    for tap in range(CONV_W):
        sh = CONV_W - 1 - tap
        xs = x if sh == 0 else pltpu.roll(ext, sh, 1)[:, SUBLANES:, :]
        u = u + xs * cw_ref[:, tap:tap + 1, :]
    u2 = u.reshape(sb * tt, w)
    rp, ip = [], []
    for m in range(w // LANES):
        um = u2[:, m * LANES:(m + 1) * LANES]
        rp.append(_bdot(um, wr_ref[m]))
        ip.append(_bdot(um, wi_ref[m]))
    r = jax.nn.sigmoid(jnp.concatenate(rp, axis=1) + br_ref[...])
    i = jax.nn.sigmoid(jnp.concatenate(ip, axis=1) + bi_ref[...])
    log_a = -LRU_C * r * _softplus(-lam_ref[...])
    th = jnp.tanh(log_a)
    bcoef = jnp.sqrt(-2.0 * th / (1.0 - th)) * (i * u2)
    nt = tt // SUBLANES
    a_cum = jnp.exp(log_a).reshape(sb * nt, SUBLANES, w)
    b_cum = bcoef.reshape(sb * nt, SUBLANES, w)
    t_idx = lax.broadcasted_iota(jnp.int32, (sb * nt, SUBLANES, w), 1)
    d = 1
    while d < SUBLANES:
        keep = t_idx >= d
        a_prev = jnp.where(keep, pltpu.roll(a_cum, d, 1), 1.0)
        b_prev = jnp.where(keep, pltpu.roll(b_cum, d, 1), 0.0)
        b_cum = a_cum * b_prev + b_cum
        a_cum = a_cum * a_prev
        d *= 2
    a_end = a_cum.reshape(sb, nt, SUBLANES, w)[:, :, SUBLANES - 1:, :]
    b_end = b_cum.reshape(sb, nt, SUBLANES, w)[:, :, SUBLANES - 1:, :]
    h_in = [hc_ref[...]]
    for g in range(nt):
        h_in.append(a_end[:, g] * h_in[g] + b_end[:, g])
    hc_ref[...] = h_in[nt]
    h_tile = jnp.stack(h_in[:nt], axis=1).reshape(sb * nt, 1, w)
    hs = (a_cum * h_tile + b_cum).reshape(sb, tt, w)
    o_ref[...] = (hs * _silu(gt_ref[...])).reshape(o_ref.shape).astype(o_ref.dtype)

    @pl.when(j == pl.num_programs(1) - 1)
    def _():
        hout_ref[...] = hs[:, nv_last - 1:nv_last, :]
        if nv_last == tt:
            tail_ref[...] = ext[:, tt:tt + SUBLANES, :]
        else:
            tail_ref[...] = pltpu.roll(ext, tt + SUBLANES - nv_last, 1)[:, 0:SUBLANES, :]


def _rglru(z3, h0, cb8, cw, cbias, wr_bd, br, wi_bd, bi, lam, *, wa, mix_w, plan):
    b, t, _ = z3.shape
    sb, tt = plan.sb_rows, plan.tt
    nt = wa // LANES
    par3 = lambda i, j: (0, 0, 0)
    seq3 = lambda i, j: (i, 0, 0)
    return pl.pallas_call(
        functools.partial(_rglru_body, nv_last=(plan.n_valid - 1) % tt + 1),
        grid=(b // sb, t // tt),
        in_specs=[pl.BlockSpec((sb, tt, wa), lambda i, j: (i, j, 0)),
                  pl.BlockSpec((sb, tt, wa), lambda i, j: (i, j, 1)),
                  pl.BlockSpec((sb, 1, wa), seq3),
                  pl.BlockSpec((sb, SUBLANES, wa), seq3),
                  pl.BlockSpec((1, CONV_W, wa), par3),
                  pl.BlockSpec((1, 1, wa), par3),
                  pl.BlockSpec((nt, LANES, LANES), par3),
                  pl.BlockSpec((1, wa), lambda i, j: (0, 0)),
                  pl.BlockSpec((nt, LANES, LANES), par3),
                  pl.BlockSpec((1, wa), lambda i, j: (0, 0)),
                  pl.BlockSpec((1, wa), lambda i, j: (0, 0))],
        out_specs=[_act_spec(plan, t, sb, tt, wa, lambda h: 0, 2),
                   pl.BlockSpec((sb, 1, wa), seq3),
                   pl.BlockSpec((sb, SUBLANES, wa), seq3)],
        out_shape=[_act_shape(plan, b, t, mix_w),
                   jax.ShapeDtypeStruct((b, 1, wa), F32),
                   jax.ShapeDtypeStruct((b, SUBLANES, wa), F32)],
        scratch_shapes=[pltpu.VMEM((sb, 1, wa), F32), pltpu.VMEM((sb, SUBLANES, wa), F32)],
        compiler_params=_cparams(2),
        name="rglru",
    )(z3, z3, h0, cb8, cw, cbias, wr_bd, br, wi_bd, bi, lam)


def _split_heads(x, hd):
    sb, c, w = x.shape
    nh = w // hd
    st = jnp.stack([x[:, :, h * hd:(h + 1) * hd] for h in range(nh)], axis=1)
    return st.reshape(sb * nh, c, hd)


def _merge_heads(x, sb):
    n, c, hd = x.shape
    nh = n // sb
    x4 = x.reshape(sb, nh, c, hd)
    return jnp.concatenate([x4[:, h] for h in range(nh)], axis=-1)


def _pair_diag(x, half):
    lo = lax.broadcasted_iota(jnp.int32, x.shape, 2) < half
    return jnp.concatenate([jnp.where(lo, x, 0.0), jnp.where(lo, 0.0, x)], axis=1)


def _pair_sum(x, half):
    lo = lax.broadcasted_iota(jnp.int32, x.shape, 2) < half
    s_lo = jnp.sum(jnp.where(lo, x, 0.0), axis=-1, keepdims=True)
    s_hi = jnp.sum(jnp.where(lo, 0.0, x), axis=-1, keepdims=True)
    return jnp.where(lo, s_lo, s_hi)


_RWKV_INPUTS = 18


def _rwkv_prep(z_refs, carry_refs, mu_refs, lw, a, kk_p, ka_p, rk_p, n_valid, hd):
    sb, c, hbw = lw.shape
    pw = 2 * hd
    row = lax.broadcasted_iota(jnp.int32, (sb, c, hbw), 1)
    first = row == 0

    def mix(z_ref, carry_ref, mu_ref):
        z = z_ref[...]
        zm = z + (jnp.where(first, carry_ref[...], pltpu.roll(z, 1, 1)) - z) * mu_ref[...]
        carry_ref[...] = z[:, c - 1:c, :]
        return zm

    r, k, v = (mix(zr, cr, mu) for zr, cr, mu in zip(z_refs, carry_refs, mu_refs))
    kk_raw = k * kk_p
    kmod = k * (1.0 + (a - 1.0) * ka_p)
    if n_valid < c:
        valid = row < n_valid
        lw = jnp.where(valid, lw, 0.0)
        kmod = jnp.where(valid, kmod, 0.0)
        v = jnp.where(valid, v, 0.0)
        a = jnp.where(valid, a, 0.0)
    cl = _cumsum_time(lw, 1)

    pairs = lambda x: _split_heads(x, pw)
    rk_pair = jnp.concatenate([_split_heads(rk_p[None], pw)] * sb, axis=0)
    kkp, ap, kmp, vp, rp, clp, lwp = map(pairs, (kk_raw, a, kmod, v, r, cl, lw))
    kkp = kkp / jnp.maximum(jnp.sqrt(_pair_sum(kkp * kkp, hd)), KK_EPS)
    bp = kkp * ap
    cl_last = clp[:, c - 1:c, :]
    e_neg = jnp.exp(-clp)
    e_end = jnp.exp(cl_last - clp)
    return dict(
        lhs=jnp.concatenate([kkp * jnp.exp(clp - lwp), rp * jnp.exp(clp)], axis=1),
        bd_b=_pair_diag(bp * e_neg, hd), bd_k=_pair_diag(kmp * e_neg, hd), bd_v=_pair_diag(vp, hd),
        k_end=kmp * e_end, b_end=bp * e_end, vp=vp,
        bonus=_pair_sum(rp * kmp * rk_pair, hd) * vp,
        e_tot=jnp.exp(cl_last))


def _rwkv_solve(p, s0, lnw_pair, lnb_pair, hd, c):
    pw = 2 * hd
    ti = lax.broadcasted_iota(jnp.int32, (c, 2 * c), 0)
    si = lax.broadcasted_iota(jnp.int32, (c, 2 * c), 1)
    si = jnp.where(si >= c, si - c, si)
    strict = (ti > si)[None]
    lower = (ti >= si)[None]
    lhs = p["lhs"]
    ab = _bdot(lhs, p["bd_b"], BNT)
    ak = _bdot(lhs, p["bd_k"], BNT)
    su = _bdot(lhs, s0, BNT)
    x = su[:, :c] + _bdot(jnp.where(strict, ak[:, :c], 0.0), p["bd_v"], BNN)
    lp = jnp.where(strict, ab[:, :c], 0.0)
    x = x - _bdot(lp, _pair_diag(x, hd), BNN)
    pw2 = 2
    while pw2 < c:
        lp = _bdot(lp, _pair_diag(lp, c), BNN)
        x = x + _bdot(lp, _pair_diag(x, hd), BNN)
        pw2 *= 2
    y = (su[:, c:] + _bdot(jnp.where(lower, ak[:, c:], 0.0), p["bd_v"], BNN)
         - _bdot(jnp.where(lower, ab[:, c:], 0.0), _pair_diag(x, hd), BNN))
    full = s0 * p["e_tot"] + _bdot(p["vp"], p["k_end"], BTN) - _bdot(x, p["b_end"], BTN)
    same_head = ((lax.broadcasted_iota(jnp.int32, (pw, pw), 0) < hd)
                 == (lax.broadcasted_iota(jnp.int32, (pw, pw), 1) < hd))[None]
    s_new = jnp.where(same_head, full, 0.0)
    mean = _pair_sum(y, hd) * (1.0 / hd)
    var = _pair_sum(jnp.square(y - mean), hd) * (1.0 / hd)
    yn = (y - mean) * lax.rsqrt(var + GN_EPS) * lnw_pair + lnb_pair
    return yn + p["bonus"], s_new


def _rwkv_body(*refs, n_valid, hd, n_alias):
    (zr_ref, zk_ref, zv_ref, zg_ref, lw_ref, a_ref, pr_ref, pk_ref, pv_ref, s0_ref,
     mur_ref, muk_ref, muv_ref, kk_ref, ka_ref, rk_ref, lnw_ref, lnb_ref) = refs[:_RWKV_INPUTS]
    o_ref, sout_ref, s_scr, cr_scr, ck_scr, cv_scr = refs[_RWKV_INPUTS + n_alias:]
    j = pl.program_id(2)
    sb, c, hbw = zr_ref.shape
    pw = 2 * hd
    npair = hbw // pw
    n = sb * npair

    @pl.when(j == 0)
    def _():
        s0 = s0_ref[...].reshape(sb, npair, 2, hd, hd)
        zero = jnp.zeros((sb, npair, hd, hd), F32)
        top = jnp.concatenate([s0[:, :, 0], zero], axis=-1)
        bot = jnp.concatenate([zero, s0[:, :, 1]], axis=-1)
        s_scr[...] = jnp.concatenate([top, bot], axis=-2).reshape(n, pw, pw)
        cr_scr[...] = pr_ref[...]
        ck_scr[...] = pk_ref[...]
        cv_scr[...] = pv_ref[...]

    par_pair = lambda ref: jnp.concatenate([_split_heads(ref[...][None], pw)] * sb, axis=0)
    ops = _rwkv_prep((zr_ref, zk_ref, zv_ref), (cr_scr, ck_scr, cv_scr), (mur_ref, muk_ref, muv_ref),
                     lw_ref[...], a_ref[...], kk_ref[...], ka_ref[...], rk_ref[...], n_valid, hd)
    y, s_new = _rwkv_solve(ops, s_scr[...], par_pair(lnw_ref), par_pair(lnb_ref), hd, c)
    s_scr[...] = s_new
    out = _merge_heads(y, sb) * _silu(zg_ref[...])
    o_ref[...] = out.reshape(o_ref.shape).astype(o_ref.dtype)

    @pl.when(j == pl.num_programs(2) - 1)
    def _():
        s4 = s_new.reshape(sb, npair, pw, pw)
        both = jnp.stack([s4[:, :, :hd, :hd], s4[:, :, hd:, hd:]], axis=2)
        sout_ref[...] = both.reshape(sout_ref.shape)


def _rwkv(z3, lw, a, zprev, s0_all, l_in, mix, s_acc, l, depth, muz, kkp, kap, rkp, lnw, lnb,
          *, wa, wb, hbw, plan):
    b, t, _ = z3.shape
    sb, c = plan.sb_rwkv, plan.chunk
    hd = s0_all.shape[-1]
    nb = wb // hbw
    nh = hbw // hd
    off = 2 * wa // hbw
    zspec = lambda o: pl.BlockSpec((sb, c, hbw), lambda i, h, j, o=o: (i, j, o + h))
    pspec = lambda o: pl.BlockSpec((sb, 1, hbw), lambda i, h, j, o=o: (i, 0, o + h))
    mspec = lambda o: pl.BlockSpec((1, hbw), lambda i, h, j, o=o: (0, o + h))
    act = pl.BlockSpec((sb, c, hbw), lambda i, h, j: (i, j, h))
    st = lambda ll: pl.BlockSpec((None, sb, nh, hd, hd), lambda i, h, j, ll=ll: (ll, i, h, 0, 0))
    aliased = [mix] + ([] if s_acc is None else [s_acc])
    n_in = _RWKV_INPUTS
    aliases = {n_in: 0} if s_acc is None else {n_in: 0, n_in + 1: 1}
    out = pl.pallas_call(
        functools.partial(_rwkv_body, n_valid=min(plan.n_valid, c), hd=hd, n_alias=len(aliased)),
        grid=(b // sb, nb, t // c),
        in_specs=[zspec(off), zspec(off + nb), zspec(off + 2 * nb), zspec(off + 3 * nb), act, act,
                  pspec(0), pspec(nb), pspec(2 * nb), st(l_in),
                  mspec(0), mspec(nb), mspec(2 * nb), mspec(0), mspec(0), mspec(0), mspec(0), mspec(0)]
        + [_ANY] * len(aliased),
        out_specs=[_act_spec(plan, t, sb, c, hbw, lambda h: wa // hbw + h, 3), st(l)],
        out_shape=[jax.ShapeDtypeStruct(mix.shape, mix.dtype),
                   jax.ShapeDtypeStruct((depth,) + s0_all.shape[1:], F32)],
        input_output_aliases=aliases,
        scratch_shapes=[pltpu.VMEM((sb * nh // 2, 2 * hd, 2 * hd), F32), pltpu.VMEM((sb, 1, hbw), F32),
                        pltpu.VMEM((sb, 1, hbw), F32), pltpu.VMEM((sb, 1, hbw), F32)],
        compiler_params=_cparams(3),
        name="rwkv7",
    )(z3, z3, z3, z3, lw, a, zprev, zprev, zprev, s0_all, muz, muz, muz, kkp, kap, rkp, lnw, lnb,
      *aliased)
    return out


def _rwkv_steps_body(*refs, n_valid, tpad, hd, n_alias):
    (zr_ref, zk_ref, zv_ref, zg_ref, lw_ref, a_ref, pr_ref, pk_ref, pv_ref, s0_ref,
     mur_ref, muk_ref, muv_ref, kk_ref, ka_ref, rk_ref, lnw_ref, lnb_ref) = refs[:_RWKV_INPUTS]
    o_ref, sout_ref, o_scr, vt_scr, y_scr = refs[_RWKV_INPUTS + n_alias:]
    bsz, cols = pr_ref.shape
    nh = cols // hd
    sout_ref[...] = s0_ref[...]
    o_scr[...] = jnp.zeros_like(o_scr)
    per_col = lambda ref: jnp.broadcast_to(ref[...], (bsz, cols)).T
    lnw_c, lnb_c, rk_c = per_col(lnw_ref), per_col(lnb_ref), per_col(rk_ref)
    step_rows = lambda ref, t: ref[pl.ds(t, bsz, stride=tpad), :]
    prev = (pr_ref[...], pk_ref[...], pv_ref[...])
    for t in range(n_valid):
        z = (step_rows(zr_ref, t), step_rows(zk_ref, t), step_rows(zv_ref, t))
        r, k, v = (zz + (pp - zz) * mu[...] for zz, pp, mu in zip(z, prev, (mur_ref, muk_ref, muv_ref)))
        prev = z
        a = step_rows(a_ref, t)
        w_t = jnp.exp(step_rows(lw_ref, t)).T
        a_t = a.T
        r_t = r.T
        v_t = v.T
        kk_t = (k * kk_ref[...]).T
        km_t = (k * (1.0 + (a - 1.0) * ka_ref[...])).T
        vt_scr[...] = v_t
        for hh in range(nh):
            sl = slice(hh * hd, (hh + 1) * hd)
            kkh = kk_t[sl]
            kkh = kkh / jnp.maximum(jnp.sqrt(jnp.sum(kkh * kkh, axis=0, keepdims=True)), KK_EPS)
            bh, wh, kmh, rh = kkh * a_t[sl], w_t[sl], km_t[sl], r_t[sl]

            def value_row(vi, carry, hh=hh, kkh=kkh, bh=bh, wh=wh, kmh=kmh, rh=rh):
                s = sout_ref[hh, vi]
                sa = jnp.sum(s * kkh, axis=0, keepdims=True)
                s = s * wh - sa * bh + vt_scr[pl.ds(hh * hd + vi, 1), :] * kmh
                sout_ref[hh, vi] = s
                y_scr[pl.ds(hh * hd + vi, 1), :] = jnp.sum(s * rh, axis=0, keepdims=True)
                return carry

            lax.fori_loop(0, hd, value_row, 0, unroll=8)
        y = y_scr[...]
        outs = []
        for hh in range(nh):
            sl = slice(hh * hd, (hh + 1) * hd)
            yh = y[sl]
            mean = jnp.mean(yh, axis=0, keepdims=True)
            var = jnp.mean(jnp.square(yh - mean), axis=0, keepdims=True)
            yn = (yh - mean) * lax.rsqrt(var + GN_EPS) * lnw_c[sl] + lnb_c[sl]
            bonus = jnp.sum(r_t[sl] * km_t[sl] * rk_c[sl], axis=0, keepdims=True) * v_t[sl]
            outs.append(yn + bonus)
        out = jnp.concatenate(outs, axis=0).T * _silu(step_rows(zg_ref, t))
        o_scr[pl.ds(t, bsz, stride=tpad), :] = out
    o_ref[...] = o_scr[...].astype(o_ref.dtype)


def _rwkv_steps(z2, lw2, a2, zprev, s0_all, l_in, mix, s_acc, l, depth, muz, kkp, kap, rkp, lnw, lnb,
                *, wa, wb, plan, tpad):
    bsz = zprev.shape[0]
    rows = z2.shape[0]
    hd = s0_all.shape[-2]
    cols = LANES
    nb = wb // cols
    nh = cols // hd
    off = 2 * wa // cols
    zspec = lambda o: pl.BlockSpec((rows, cols), lambda h, o=o: (0, o + h))
    pspec = lambda o: pl.BlockSpec((bsz, cols), lambda h, o=o: (0, o + h))
    mspec = lambda o: pl.BlockSpec((1, cols), lambda h, o=o: (0, o + h))
    st = lambda ll: pl.BlockSpec((None, nh, hd, hd, bsz), lambda h, ll=ll: (ll, h, 0, 0, 0))
    aliased = [mix] + ([] if s_acc is None else [s_acc])
    n_in = _RWKV_INPUTS
    aliases = {n_in: 0} if s_acc is None else {n_in: 0, n_in + 1: 1}
    return pl.pallas_call(
        functools.partial(_rwkv_steps_body, n_valid=plan.n_valid, tpad=tpad, hd=hd, n_alias=len(aliased)),
        grid=(nb,),
        in_specs=[zspec(off), zspec(off + nb), zspec(off + 2 * nb), zspec(off + 3 * nb),
                  zspec(0), zspec(0), pspec(0), pspec(nb), pspec(2 * nb), st(l_in),
                  mspec(0), mspec(nb), mspec(2 * nb), mspec(0), mspec(0), mspec(0), mspec(0), mspec(0)]
        + [_ANY] * len(aliased),
        out_specs=[pl.BlockSpec((rows, cols), lambda h: (0, wa // cols + h)), st(l)],
        out_shape=[jax.ShapeDtypeStruct(mix.shape, mix.dtype),
                   jax.ShapeDtypeStruct((depth,) + s0_all.shape[1:], F32)],
        input_output_aliases=aliases,
        scratch_shapes=[pltpu.VMEM((rows, cols), F32), pltpu.VMEM((cols, bsz), F32),
                        pltpu.VMEM((cols, bsz), F32)],
        compiler_params=_cparams(1),
        name="rwkv7_steps",
    )(z2, z2, z2, z2, lw2, a2, zprev, zprev, zprev, s0_all, muz, muz, muz, kkp, kap, rkp, lnw, lnb,
      *aliased)


def _hgrn_head(q, f, v, g, s0, lb, ng, n_valid, sub, stage):
    b2_ref, kf_ref, v_ref = stage
    bcast_row = lambda ref, i: ref[pl.ds(i, 1), :]
    c, dk = q.shape
    sig = jax.nn.sigmoid(f)
    logg = jnp.log(jnp.maximum(lb + (1.0 - lb) * sig, TINY))
    kf = (1.0 - lb) * jax.nn.sigmoid(-f)
    if n_valid < c:
        valid = lax.broadcasted_iota(jnp.int32, (c, dk), 0) < n_valid
        logg = jnp.where(valid, logg, 0.0)
        kf = jnp.where(valid, kf, 0.0)
    bc = _cumsum_time(logg, 0)
    b2 = bc * LOG2E
    b2_ref[...] = b2
    kf_ref[...] = kf
    v_ref[...] = v
    o_inter = _bdot(q * jnp.exp(bc), s0)
    pairs, w_rows = [], []
    for lo in range(0, c, sub):
        for r0 in range(lo, lo + sub, SUBLANES):
            qg, bg = q[r0:r0 + SUBLANES], b2[r0:r0 + SUBLANES]
            for src in range(lo, r0 + SUBLANES):
                pairs.append((r0, src))
                w_rows.append(qg * jnp.exp2(bg - bcast_row(b2_ref, src)) * bcast_row(kf_ref, src))
    att_rows = _bdot(jnp.concatenate(w_rows, axis=0), jnp.ones((dk, v.shape[1]), BF16))
    t_row = lax.broadcasted_iota(jnp.int32, (SUBLANES, 1), 0)
    groups = {}
    for u, (r0, src) in enumerate(pairs):
        col = att_rows[u * SUBLANES:(u + 1) * SUBLANES]
        if src > r0:
            col = jnp.where(t_row >= src - r0, col, 0.0)
        term = col * bcast_row(v_ref, src)
        groups[r0] = groups[r0] + term if r0 in groups else term
    blocks = []
    for lo in range(0, c, sub):
        ob = o_inter[lo:lo + sub]
        if lo > 0:
            bs = bc[lo - 1:lo]
            att = _bdot(q[lo:lo + sub] * jnp.exp(bc[lo:lo + sub] - bs), kf[:lo] * jnp.exp(bs - bc[:lo]), NT)
            ob = ob + _bdot(att, v[:lo])
        blocks.append(ob + jnp.concatenate([groups[r0] for r0 in range(lo, lo + sub, SUBLANES)], axis=0))
    o = jnp.concatenate(blocks, axis=0) if len(blocks) > 1 else blocks[0]
    b_last = bc[c - 1:c]
    e_col = jnp.broadcast_to(jnp.exp(b_last), (v.shape[1], dk)).T
    s_new = e_col * s0 + _bdot(kf * jnp.exp(b_last - bc), v, TN)
    on = o * lax.rsqrt(jnp.mean(o * o, axis=-1, keepdims=True) + RMS_EPS) * ng
    return on * _silu(g), s_new


_HGRN_INPUTS = 7


def _hgrn_body(*refs, n_valid, dk, sub, n_alias):
    q_ref, f_ref, i_ref, g_ref, s0_ref, lb_ref, ng_ref = refs[:_HGRN_INPUTS]
    o_ref, sout_ref, s_scr, o_scr, b2_scr, kf_scr, v_scr = refs[_HGRN_INPUTS + n_alias:]
    j = pl.program_id(2)

    @pl.when(j == 0)
    def _():
        s_scr[...] = s0_ref[...]

    sb, c, hcw = q_ref.shape
    nh = hcw // dk
    lb = lb_ref[...]
    ng = ng_ref[...]

    def one(s, carry, slot0=0):
        q, f, v, g = q_ref[s], f_ref[s], i_ref[s], g_ref[s]
        outs = []
        for hh in range(nh):
            sl = slice(hh * dk, (hh + 1) * dk)
            slot = slot0 + hh
            o, s_new = _hgrn_head(q[:, sl], f[:, sl], v[:, sl], g[:, sl], s_scr[s, hh],
                                  lb[:, sl], ng[:, sl], n_valid, sub,
                                  (b2_scr.at[slot], kf_scr.at[slot], v_scr.at[slot]))
            s_scr[s, hh] = s_new
            outs.append(o)
        o_scr[s] = jnp.concatenate(outs, axis=1) if nh > 1 else outs[0]
        return carry

    unroll = math.gcd(sb, HGRN_UNROLL_SEQS)

    def group(gi, carry):
        for u in range(unroll):
            one(gi * unroll + u, carry, u * nh)
        return carry

    if sb == unroll:
        group(0, 0)
    else:
        lax.fori_loop(0, sb // unroll, group, 0)
    o_ref[...] = o_scr[...].reshape(o_ref.shape).astype(o_ref.dtype)

    @pl.when(j == pl.num_programs(2) - 1)
    def _():
        sout_ref[...] = s_scr[...]


def _hgrn(z3, s0_all, l_in, mix, s_acc, l, depth, lb, ng, *, col0, out_col0, wc, hcw, plan):
    b, t, _ = z3.shape
    sb, c = plan.sb_hgrn, plan.chunk
    dk, dv = s0_all.shape[-2], s0_all.shape[-1]
    nb = wc // hcw
    nh = hcw // dk
    off = col0 // hcw
    zspec = lambda o: pl.BlockSpec((sb, c, hcw), lambda i, h, j, o=o: (i, j, o + h))
    st = lambda ll: pl.BlockSpec((None, sb, nh, dk, dv), lambda i, h, j, ll=ll: (ll, i, h, 0, 0))
    par = pl.BlockSpec((1, hcw), lambda i, h, j: (0, h))
    aliased = [mix] + ([] if s_acc is None else [s_acc])
    n_in = _HGRN_INPUTS
    aliases = {n_in: 0} if s_acc is None else {n_in: 0, n_in + 1: 1}
    return pl.pallas_call(
        functools.partial(_hgrn_body, n_valid=min(plan.n_valid, c), dk=dk, sub=min(2 * SUBLANES, c),
                          n_alias=len(aliased)),
        grid=(b // sb, nb, t // c),
        in_specs=[zspec(off), zspec(off + nb), zspec(off + 2 * nb), zspec(off + 3 * nb), st(l_in),
                  par, par] + [_ANY] * len(aliased),
        out_specs=[_act_spec(plan, t, sb, c, hcw, lambda h: out_col0 // hcw + h, 3), st(l)],
        out_shape=[jax.ShapeDtypeStruct(mix.shape, mix.dtype),
                   jax.ShapeDtypeStruct((depth,) + s0_all.shape[1:], F32)],
        input_output_aliases=aliases,
        scratch_shapes=[pltpu.VMEM((sb, nh, dk, dv), F32), pltpu.VMEM((sb, c, hcw), F32)]
        + [pltpu.VMEM((nh * math.gcd(sb, HGRN_UNROLL_SEQS), c, dk), F32)] * 3,
        compiler_params=_cparams(3),
        name="hgrn2",
    )(z3, z3, z3, z3, s0_all, lb, ng, *aliased)


def _post1_body(h_ref, m_ref, g_ref, h1_ref, hn_ref):
    m = m_ref[...].astype(F32)
    h1 = h_ref[...] + m * lax.rsqrt(jnp.mean(m * m, axis=-1, keepdims=True) + RMS_EPS) * g_ref[...]
    h1_ref[...] = h1
    hn_ref[...] = (h1 * lax.rsqrt(jnp.mean(h1 * h1, axis=-1, keepdims=True) + RMS_EPS)).astype(hn_ref.dtype)


def _post2_body(h_ref, p_ref, proj_ref, gate_ref, g_ref, o_ref):
    u = jnp.dot(p_ref[...], proj_ref[...], preferred_element_type=F32)
    x = u * jax.nn.sigmoid(gate_ref[...].astype(F32))
    o_ref[...] = h_ref[...] + x * lax.rsqrt(jnp.mean(x * x, axis=-1, keepdims=True) + RMS_EPS) * g_ref[...]


def _post_ple(h1, p2, proj_all, l, gate, gain, tr=256):
    n, d = h1.shape
    kp = p2.shape[1]
    tr = min(tr, n)
    row = pl.BlockSpec((tr, d), lambda i: (i, 0))
    return pl.pallas_call(
        _post2_body,
        grid=(n // tr,),
        in_specs=[row, pl.BlockSpec((tr, kp), lambda i: (i, 0)),
                  pl.BlockSpec((None, kp, d), lambda i: (l, 0, 0)), row,
                  pl.BlockSpec((1, d), lambda i: (0, 0))],
        out_specs=row,
        out_shape=jax.ShapeDtypeStruct((n, d), F32),
        compiler_params=_cparams(1),
        name="post_ple",
    )(h1, p2, proj_all, gate, gain)


def _rowwise(body, name, arrays, gain, out_dtypes, tr=256):
    n, d = arrays[0].shape
    tr = min(tr, n)
    row = pl.BlockSpec((tr, d), lambda i: (i, 0))
    return pl.pallas_call(
        body,
        grid=(n // tr,),
        in_specs=[row] * len(arrays) + [pl.BlockSpec((1, d), lambda i: (0, 0))],
        out_specs=[row] * len(out_dtypes),
        out_shape=[jax.ShapeDtypeStruct((n, d), dt) for dt in out_dtypes],
        compiler_params=_cparams(1),
        name=name,
    )(*arrays, gain)


def _block_diag_tiles(w):
    nb, bs, _ = w.shape
    per = LANES // bs
    w = w.reshape(nb // per, per, bs, bs)
    eye = jnp.eye(per, dtype=w.dtype)
    return jnp.einsum('tpab,pq->tpaqb', w, eye).reshape(nb // per, LANES, LANES)


def _layer_stack(x, p, st, W, mm, plan, zero_state):
    st_a_h, st_a_conv, st_b_s, st_b_x, st_b_z, st_c_s = st
    b, t, d = x.shape
    depth = W['g_pre'].shape[0]
    wa = st_a_h.shape[-1]
    hd = st_b_s.shape[-1]
    wb = st_b_s.shape[-3] * hd
    dk = st_c_s.shape[-2]
    wc = st_c_s.shape[-3] * dk
    mix_w = wa + wb + wc
    n = b * t
    nv = plan.n_valid
    hbw = math.gcd(math.gcd(2 * wa, wb), math.gcd(wa, 512))
    hcw = math.gcd(math.gcd(2 * wa + 4 * wb, wc), math.gcd(wa + wb, 512))

    lb_soft = jax.nn.softmax(W['hgrn_lb'].astype(F32), axis=0)
    lb_all = jnp.cumsum(lb_soft, axis=0) - lb_soft[0]
    p_act = p.astype(BF16)

    if plan.rwkv_steps:
        st_b_s = jnp.transpose(st_b_s, (0, 2, 3, 4, 1))
    rwkv_params = lambda l: (
        W['rwkv_mu_z'][l][None], W['rwkv_kk'][l][None], W['rwkv_ka'][l][None],
        W['rwkv_rk'][l].reshape(1, wb), W['rwkv_lnx_w'][l][None], W['rwkv_lnx_b'][l][None])

    h = x
    small = [[] for _ in range(4)]
    sb_acc = sc_acc = None
    for l in range(depth):
        ls = 0 if zero_state else l
        xn, lw, a, xlast = _prenorm(
            h, W['g_pre'][l].reshape(1, 1, d), st_b_x[ls][:, None, :], W['rwkv_lora_in'][l],
            W['rwkv_w2'][l], W['rwkv_a2'][l], W['rwkv_w0'][l][None], W['rwkv_a0'][l][None], plan)
        z3 = mm('w_in', xn.reshape(n, d), l, F32).reshape(b, t, -1)
        cb8 = jnp.pad(st_a_conv[ls], ((0, 0), (SUBLANES - (CONV_W - 1), 0), (0, 0)))
        mix, ha, tail = _rglru(
            z3, st_a_h[ls][:, None, :], cb8, W['conv_a_w'][l][None], W['conv_a_b'][l].reshape(1, 1, wa),
            W['lru_wr_bd'][l], W['lru_br'][l][None], W['lru_wi_bd'][l], W['lru_bi'][l][None],
            W['lru_lambda'][l][None], wa=wa, mix_w=mix_w, plan=plan)
        if plan.rwkv_steps:
            mix, sb_acc = _rwkv_steps(
                z3.reshape(n, -1), lw.reshape(n, wb), a.reshape(n, wb), st_b_z[ls], st_b_s, ls,
                mix, sb_acc, l, depth, *rwkv_params(l), wa=wa, wb=wb, plan=plan, tpad=t)
        else:
            mix, sb_acc = _rwkv(
                z3, lw, a, st_b_z[ls][:, None, :], st_b_s, ls, mix, sb_acc, l, depth,
                *rwkv_params(l), wa=wa, wb=wb, hbw=hbw, plan=plan)
        mix, sc_acc = _hgrn(
            z3, st_c_s, ls, mix, sc_acc, l, depth, lb_all[l][None], W['hgrn_norm_g'][l][None],
            col0=2 * wa + 4 * wb, out_col0=wa + wb, wc=wc, hcw=hcw, plan=plan)
        mo = mm('w_out', mix.reshape(n, mix_w), l, BF16)
        h1, hn = _rowwise(_post1_body, "post_mix", [h.reshape(n, d), mo], W['g_post'][l][None],
                          [F32, BF16])
        gp = mm('ple_gate', hn, l, BF16)
        h = _post_ple(h1, p_act[l].reshape(n, -1), W['ple_proj'], l, gp, W['g_ple'][l][None]).reshape(b, t, d)
        small[0].append(ha[:, 0])
        small[1].append(tail[:, SUBLANES - (CONV_W - 1):])
        small[2].append(xlast[:, 0])
        small[3].append(z3[:, nv - 1, 2 * wa:2 * wa + 3 * wb])
    na_h, na_c, nb_x, nb_z = (jnp.stack(o) for o in small)
    if plan.rwkv_steps:
        sb_acc = jnp.transpose(sb_acc, (0, 4, 1, 2, 3))
    return h, (na_h, na_c, sb_acc, nb_x, nb_z, sc_acc)


def kernel(x_prompt, x_sample, p_prompt, p_sample, state_a_h, state_a_conv, state_b_S,
           state_b_xprev, state_b_zprev, state_c_S, g_pre, g_post, w_in, w_out,
           conv_a_w, conv_a_b, lru_wr, lru_br, lru_wi, lru_bi, lru_lambda,
           rwkv_mu_z, rwkv_mu_w, rwkv_mu_a, rwkv_w0, rwkv_w1, rwkv_w2, rwkv_a0, rwkv_a1,
           rwkv_a2, rwkv_kk, rwkv_ka, rwkv_rk, rwkv_lnx_w, rwkv_lnx_b, hgrn_lb, hgrn_norm_g,
           ple_proj, ple_gate, g_ple):
    depth = w_in.shape[0]
    W = dict(g_pre=g_pre, g_post=g_post, conv_a_w=conv_a_w, conv_a_b=conv_a_b,
             lru_br=lru_br, lru_bi=lru_bi, lru_lambda=lru_lambda, rwkv_mu_z=rwkv_mu_z,
             rwkv_mu_w=rwkv_mu_w, rwkv_mu_a=rwkv_mu_a, rwkv_w0=rwkv_w0, rwkv_a0=rwkv_a0,
             rwkv_kk=rwkv_kk, rwkv_ka=rwkv_ka, rwkv_rk=rwkv_rk, rwkv_lnx_w=rwkv_lnx_w,
             rwkv_lnx_b=rwkv_lnx_b, hgrn_lb=hgrn_lb, hgrn_norm_g=hgrn_norm_g, g_ple=g_ple)
    for name, w in (('rwkv_w2', rwkv_w2), ('rwkv_a2', rwkv_a2), ('ple_proj', ple_proj)):
        W[name] = w.astype(BF16)
    mw, ma = rwkv_mu_w[:, :, None], rwkv_mu_a[:, :, None]
    W['rwkv_lora_in'] = jnp.concatenate(
        [(1.0 - mw) * rwkv_w1, (1.0 - ma) * rwkv_a1, mw * rwkv_w1, ma * rwkv_a1], axis=-1).astype(BF16)
    W['lru_wr_bd'] = jnp.stack([_block_diag_tiles(lru_wr[l]) for l in range(depth)]).astype(BF16)
    W['lru_wi_bd'] = jnp.stack([_block_diag_tiles(lru_wi[l]) for l in range(depth)]).astype(BF16)

    bp, tp, d = x_prompt.shape
    bs, ts, _ = x_sample.shape
    dt = x_prompt.dtype
    st_s = (state_a_h, state_a_conv, state_b_S, state_b_xprev, state_b_zprev, state_c_S)
    st_p = tuple(jnp.zeros((1, bp) + s.shape[2:], dt) for s in st_s)

    big = dict(w_in=w_in, w_out=w_out, ple_gate=ple_gate)
    wq = {name: [None] * depth for name in big}

    def mm_sample(name, x, l, out_dtype):
        if x.shape[0] <= MM_TILE:
            y, wq[name][l] = _matmul_wcast(x, big[name], l, out_dtype)
            return y
        wq[name][l] = big[name][l].astype(BF16)
        return _matmul(x, wq[name][l], out_dtype)

    def mm_prompt(name, x, l, out_dtype):
        return _matmul(x, wq[name][l], out_dtype)

    tpad = -(-ts // SUBLANES) * SUBLANES
    pad_t = lambda v, ax: jnp.pad(v, [(0, tpad - ts) if i == ax else (0, 0) for i in range(v.ndim)])
    y_s, out_s = _layer_stack(pad_t(x_sample, 1), pad_t(p_sample, 2), st_s, W, mm_sample,
                              _make_plan(bs, tpad, ts), False)
    y_p, out_p = _layer_stack(x_prompt, p_prompt, st_p, W, mm_prompt, _make_plan(bp, tp, tp), True)
    return (y_p, y_s[:, :ts]) + out_p + out_s
```

```python
import functools
import math
from typing import NamedTuple

import jax
import jax.numpy as jnp
from jax import lax
from jax.experimental import pallas as pl
from jax.experimental.pallas import tpu as pltpu

F32 = jnp.float32
BF16 = jnp.bfloat16

RMS_EPS = 1e-6
GN_EPS = 64e-5
LRU_C = 8.0
TINY = 1e-30
KK_EPS = 1e-12
CONV_W = 4
LOG2E = 1.4426950408889634
NORM_GROUP_ROWS = 128
HGRN_UNROLL_SEQS = 4

LANES = 128
SUBLANES = 8
BF16_ROWS = 16
VMEM_LIMIT = 52 * 1024 * 1024

NT = (((1,), (1,)), ((), ()))
TN = (((0,), (0,)), ((), ()))
BNT = (((2,), (2,)), ((0,), (0,)))
BNN = (((2,), (1,)), ((0,), (0,)))
BTN = (((1,), (1,)), ((0,), (0,)))


class Plan(NamedTuple):
    n_valid: int
    tt: int
    sb_rows: int
    tt_norm: int
    sb_norm: int
    chunk: int
    sb_rwkv: int
    sb_hgrn: int
    flat_acts: bool
    rwkv_steps: bool


def _make_plan(b, t, n_valid):
    tt = min(t, 128)
    chunk = min(t, 64)
    rows = 128
    sb = max(1, min(b, rows // tt))
    tt_norm = min(t, 2 * rows)
    sb_norm = max(1, min(b, 2 * rows // tt_norm))
    flat = chunk == t
    assert flat or (chunk % BF16_ROWS == 0 and tt % BF16_ROWS == 0)
    return Plan(n_valid=n_valid, tt=tt, sb_rows=sb, tt_norm=tt_norm, sb_norm=sb_norm, chunk=chunk,
                sb_rwkv=math.gcd(b, max(4, rows // chunk)),
                sb_hgrn=sb if chunk < 64 else math.gcd(b, HGRN_UNROLL_SEQS),
                flat_acts=flat, rwkv_steps=flat and b % LANES == 0 and t <= SUBLANES)


def _act_shape(plan, b, t, width):
    return jax.ShapeDtypeStruct((b * t, width) if plan.flat_acts else (b, t, width), BF16)


def _act_spec(plan, t, sb, rows, width, col, grid_rank):
    if grid_rank == 2:
        if plan.flat_acts:
            return pl.BlockSpec((sb * rows, width), lambda i, j: (i * (t // rows) + j, col(0)))
        return pl.BlockSpec((sb, rows, width), lambda i, j: (i, j, col(0)))
    if plan.flat_acts:
        return pl.BlockSpec((sb * rows, width), lambda i, h, j: (i * (t // rows) + j, col(h)))
    return pl.BlockSpec((sb, rows, width), lambda i, h, j: (i, j, col(h)))


def _cparams(n_axes):
    return pltpu.CompilerParams(dimension_semantics=("arbitrary",) * n_axes,
                                vmem_limit_bytes=VMEM_LIMIT)


def _softplus(x):
    return jnp.maximum(x, 0.0) + jnp.log1p(jnp.exp(-jnp.abs(x)))


def _silu(x):
    return x * jax.nn.sigmoid(x)


def _bdot(a, b, dims=None):
    a = a.astype(BF16)
    b = b.astype(BF16)
    if dims is None:
        return jnp.dot(a, b, preferred_element_type=F32)
    return lax.dot_general(a, b, dims, preferred_element_type=F32)


def _cumsum_time(x, axis):
    n = x.shape[axis]
    idx = lax.broadcasted_iota(jnp.int32, x.shape, axis)
    d = 1
    while d < n:
        x = x + jnp.where(idx >= d, pltpu.roll(x, d, axis), 0.0)
        d *= 2
    return x


_ANY = pl.BlockSpec(memory_space=pl.ANY)


MM_TILE = 1024


def _mm_body(x_ref, w_ref, o_ref):
    o_ref[...] = jnp.dot(x_ref[...], w_ref[...], preferred_element_type=F32).astype(o_ref.dtype)


def _mm_wcast_body(x_ref, w_ref, o_ref, wb_ref):
    wb = w_ref[...].astype(BF16)
    wb_ref[...] = wb
    o_ref[...] = jnp.dot(x_ref[...], wb, preferred_element_type=F32).astype(o_ref.dtype)


def _matmul(x, w, out_dtype):
    m, k = x.shape
    n = w.shape[1]
    tm = math.gcd(m, MM_TILE)
    tn = math.gcd(n, MM_TILE)
    return pl.pallas_call(
        _mm_body,
        grid=(m // tm, n // tn),
        in_specs=[pl.BlockSpec((tm, k), lambda i, j: (i, 0)),
                  pl.BlockSpec((k, tn), lambda i, j: (0, j))],
        out_specs=pl.BlockSpec((tm, tn), lambda i, j: (i, j)),
        out_shape=jax.ShapeDtypeStruct((m, n), out_dtype),
        compiler_params=_cparams(2),
        name="matmul",
    )(x, w)


def _matmul_wcast(x, w_all, l, out_dtype):
    m, k = x.shape
    n = w_all.shape[2]
    tn = math.gcd(n, MM_TILE // 2)
    return pl.pallas_call(
        _mm_wcast_body,
        grid=(n // tn,),
        in_specs=[pl.BlockSpec((m, k), lambda j: (0, 0)),
                  pl.BlockSpec((None, k, tn), lambda j: (l, 0, j))],
        out_specs=[pl.BlockSpec((m, tn), lambda j: (0, j)),
                   pl.BlockSpec((k, tn), lambda j: (0, j))],
        out_shape=[jax.ShapeDtypeStruct((m, n), out_dtype), jax.ShapeDtypeStruct((k, n), BF16)],
        compiler_params=_cparams(1),
        name="matmul_wcast",
    )(x, w_all)


def _prenorm_body(h_ref, g_ref, xprev_ref, wf_ref, w2_ref, a2_ref, w0_ref, a0_ref,
                  xn_ref, lw_ref, a_ref, xlast_ref, carry_ref, *, last_row):
    j = pl.program_id(1)
    sb, tt, d = h_ref.shape
    r2 = wf_ref.shape[1] // 2
    rank = r2 // 2

    @pl.when(j == 0)
    def _():
        xp = jnp.broadcast_to(xprev_ref[...], (sb, SUBLANES, d)).reshape(sb * SUBLANES, d)
        carry_ref[...] = _bdot(xp, wf_ref[:, r2:]).reshape(sb, SUBLANES, r2)[:, 0:1, :]

    gt = min(tt, NORM_GROUP_ROWS)
    gs = max(1, min(sb, NORM_GROUP_ROWS // gt))
    for s0 in range(0, sb, gs):
        for t0 in range(0, tt, gt):
            h = h_ref[s0:s0 + gs, t0:t0 + gt, :]
            xn = h * lax.rsqrt(jnp.mean(h * h, axis=-1, keepdims=True) + RMS_EPS) * g_ref[...]
            xn_b = xn.reshape(gs * gt, d).astype(BF16)
            if len(xn_ref.shape) == 2:
                xn_ref[s0 * tt + t0:s0 * tt + t0 + gs * gt, :] = xn_b
            else:
                xn_ref[s0:s0 + gs, t0:t0 + gt, :] = xn_b.reshape(gs, gt, d)
            if t0 <= last_row < t0 + gt:
                @pl.when(j == pl.num_programs(1) - 1)
                def _(xn=xn, s0=s0, t0=t0):
                    xlast_ref[s0:s0 + gs] = xn[:, last_row - t0:last_row - t0 + 1, :]

            prod = jnp.dot(xn_b, wf_ref[...], preferred_element_type=F32)
            shifted = prod[:, r2:].reshape(gs, gt, r2)
            t_idx = lax.broadcasted_iota(jnp.int32, (gs, gt, r2), 1)
            prev = jnp.where(t_idx == 0, carry_ref[s0:s0 + gs], pltpu.roll(shifted, 1, 1))
            carry_ref[s0:s0 + gs] = shifted[:, gt - 1:gt, :]
            pre = prod[:, :r2] + prev.reshape(gs * gt, r2)
            yw = w0_ref[...] + _bdot(jnp.tanh(pre[:, :rank]), w2_ref[...])
            wl = -_softplus(-yw) - 0.5
            lw_ref[s0:s0 + gs, t0:t0 + gt, :] = (-jnp.exp(wl)).reshape(gs, gt, -1)
            ya = a0_ref[...] + _bdot(pre[:, rank:], a2_ref[...])
            a_ref[s0:s0 + gs, t0:t0 + gt, :] = jax.nn.sigmoid(ya).reshape(gs, gt, -1)


def _prenorm(h3, g, xprev, wf, w2, a2, w0, a0, plan):
    b, t, d = h3.shape
    sb, tt = plan.sb_norm, plan.tt_norm
    wb = w2.shape[1]
    rank = w2.shape[0]
    row3 = lambda i, j: (i, j, 0)
    par3 = lambda i, j: (0, 0, 0)
    par2 = lambda i, j: (0, 0)
    seq3 = lambda i, j: (i, 0, 0)
    return pl.pallas_call(
        functools.partial(_prenorm_body, last_row=(plan.n_valid - 1) % tt),
        grid=(b // sb, t // tt),
        in_specs=[pl.BlockSpec((sb, tt, d), row3),
                  pl.BlockSpec((1, 1, d), par3),
                  pl.BlockSpec((sb, 1, d), seq3),
                  pl.BlockSpec((d, 4 * rank), par2),
                  pl.BlockSpec((rank, wb), par2),
                  pl.BlockSpec((rank, wb), par2),
                  pl.BlockSpec((1, wb), par2),
                  pl.BlockSpec((1, wb), par2)],
        out_specs=[_act_spec(plan, t, sb, tt, d, lambda h: 0, 2),
                   pl.BlockSpec((sb, tt, wb), row3),
                   pl.BlockSpec((sb, tt, wb), row3),
                   pl.BlockSpec((sb, 1, d), seq3)],
        out_shape=[_act_shape(plan, b, t, d),
                   jax.ShapeDtypeStruct((b, t, wb), F32),
                   jax.ShapeDtypeStruct((b, t, wb), F32),
                   jax.ShapeDtypeStruct((b, 1, d), F32)],
        scratch_shapes=[pltpu.VMEM((sb, 1, 2 * rank), F32)],
        compiler_params=_cparams(2),
        name="prenorm_lora",
    )(h3, g, xprev, wf, w2, a2, w0, a0)


def _rglru_body(x_ref, gt_ref, h0_ref, cb_ref, cw_ref, cbias_ref, wr_ref, br_ref, wi_ref, bi_ref,
                lam_ref, o_ref, hout_ref, tail_ref, hc_ref, tl_ref, *, nv_last):
    j = pl.program_id(1)

    @pl.when(j == 0)
    def _():
        hc_ref[...] = h0_ref[...]
        tl_ref[...] = cb_ref[...]

    x = x_ref[...]
    sb, tt, w = x.shape
    ext = jnp.concatenate([tl_ref[...], x], axis=1)
    tl_ref[...] = ext[:, tt:tt + SUBLANES, :]
    u = cbias_ref[...]
    for tap in range(CONV_W):
        sh = CONV_W - 1 - tap
        xs = x if sh == 0 else pltpu.roll(ext, sh, 1)[:, SUBLANES:, :]
        u = u + xs * cw_ref[:, tap:tap + 1, :]
    u2 = u.reshape(sb * tt, w)
    rp, ip = [], []
    for m in range(w // LANES):
        um = u2[:, m * LANES:(m + 1) * LANES]
        rp.append(_bdot(um, wr_ref[m]))
        ip.append(_bdot(um, wi_ref[m]))
    r = jax.nn.sigmoid(jnp.concatenate(rp, axis=1) + br_ref[...])
    i = jax.nn.sigmoid(jnp.concatenate(ip, axis=1) + bi_ref[...])
    log_a = -LRU_C * r * _softplus(-lam_ref[...])
    th = jnp.tanh(log_a)
    bcoef = jnp.sqrt(-2.0 * th / (1.0 - th)) * (i * u2)
    nt = tt // SUBLANES
    a_cum = jnp.exp(log_a).reshape(sb * nt, SUBLANES, w)
    b_cum = bcoef.reshape(sb * nt, SUBLANES, w)
    t_idx = lax.broadcasted_iota(jnp.int32, (sb * nt, SUBLANES, w), 1)
    d = 1
    while d < SUBLANES:
        keep = t_idx >= d
        a_prev = jnp.where(keep, pltpu.roll(a_cum, d, 1), 1.0)
        b_prev = jnp.where(keep, pltpu.roll(b_cum, d, 1), 0.0)
        b_cum = a_cum * b_prev + b_cum
        a_cum = a_cum * a_prev
        d *= 2
    a_end = a_cum.reshape(sb, nt, SUBLANES, w)[:, :, SUBLANES - 1:, :]
    b_end = b_cum.reshape(sb, nt, SUBLANES, w)[:, :, SUBLANES - 1:, :]
    h_in = [hc_ref[...]]
    for g in range(nt):
        h_in.append(a_end[:, g] * h_in[g] + b_end[:, g])
    hc_ref[...] = h_in[nt]
    h_tile = jnp.stack(h_in[:nt], axis=1).reshape(sb * nt, 1, w)
    hs = (a_cum * h_tile + b_cum).reshape(sb, tt, w)
    o_ref[...] = (hs * _silu(gt_ref[...])).reshape(o_ref.shape).astype(o_ref.dtype)

    @pl.when(j == pl.num_programs(1) - 1)
    def _():
        hout_ref[...] = hs[:, nv_last - 1:nv_last, :]
        if nv_last == tt:
            tail_ref[...] = ext[:, tt:tt + SUBLANES, :]
        else:
            tail_ref[...] = pltpu.roll(ext, tt + SUBLANES - nv_last, 1)[:, 0:SUBLANES, :]


def _rglru(z3, h0, cb8, cw, cbias, wr_bd, br, wi_bd, bi, lam, *, wa, mix_w, plan):
    b, t, _ = z3.shape
    sb, tt = plan.sb_rows, plan.tt
    nt = wa // LANES
    par3 = lambda i, j: (0, 0, 0)
    seq3 = lambda i, j: (i, 0, 0)
    return pl.pallas_call(
        functools.partial(_rglru_body, nv_last=(plan.n_valid - 1) % tt + 1),
        grid=(b // sb, t // tt),
        in_specs=[pl.BlockSpec((sb, tt, wa), lambda i, j: (i, j, 0)),
                  pl.BlockSpec((sb, tt, wa), lambda i, j: (i, j, 1)),
                  pl.BlockSpec((sb, 1, wa), seq3),
                  pl.BlockSpec((sb, SUBLANES, wa), seq3),
                  pl.BlockSpec((1, CONV_W, wa), par3),
                  pl.BlockSpec((1, 1, wa), par3),
                  pl.BlockSpec((nt, LANES, LANES), par3),
                  pl.BlockSpec((1, wa), lambda i, j: (0, 0)),
                  pl.BlockSpec((nt, LANES, LANES), par3),
                  pl.BlockSpec((1, wa), lambda i, j: (0, 0)),
                  pl.BlockSpec((1, wa), lambda i, j: (0, 0))],
        out_specs=[_act_spec(plan, t, sb, tt, wa, lambda h: 0, 2),
                   pl.BlockSpec((sb, 1, wa), seq3),
                   pl.BlockSpec((sb, SUBLANES, wa), seq3)],
        out_shape=[_act_shape(plan, b, t, mix_w),
                   jax.ShapeDtypeStruct((b, 1, wa), F32),
                   jax.ShapeDtypeStruct((b, SUBLANES, wa), F32)],
        scratch_shapes=[pltpu.VMEM((sb, 1, wa), F32), pltpu.VMEM((sb, SUBLANES, wa), F32)],
        compiler_params=_cparams(2),
        name="rglru",
    )(z3, z3, h0, cb8, cw, cbias, wr_bd, br, wi_bd, bi, lam)


def _split_heads(x, hd):
    sb, c, w = x.shape
    nh = w // hd
    st = jnp.stack([x[:, :, h * hd:(h + 1) * hd] for h in range(nh)], axis=1)
    return st.reshape(sb * nh, c, hd)


def _merge_heads(x, sb):
    n, c, hd = x.shape
    nh = n // sb
    x4 = x.reshape(sb, nh, c, hd)
    return jnp.concatenate([x4[:, h] for h in range(nh)], axis=-1)


def _pair_diag(x, half):
    lo = lax.broadcasted_iota(jnp.int32, x.shape, 2) < half
    return jnp.concatenate([jnp.where(lo, x, 0.0), jnp.where(lo, 0.0, x)], axis=1)


def _pair_sum(x, half):
    lo = lax.broadcasted_iota(jnp.int32, x.shape, 2) < half
    s_lo = jnp.sum(jnp.where(lo, x, 0.0), axis=-1, keepdims=True)
    s_hi = jnp.sum(jnp.where(lo, 0.0, x), axis=-1, keepdims=True)
    return jnp.where(lo, s_lo, s_hi)


_RWKV_INPUTS = 18


def _rwkv_prep(z_refs, carry_refs, mu_refs, lw, a, kk_p, ka_p, rk_p, n_valid, hd):
    sb, c, hbw = lw.shape
    pw = 2 * hd
    row = lax.broadcasted_iota(jnp.int32, (sb, c, hbw), 1)
    first = row == 0

    def mix(z_ref, carry_ref, mu_ref):
        z = z_ref[...]
        zm = z + (jnp.where(first, carry_ref[...], pltpu.roll(z, 1, 1)) - z) * mu_ref[...]
        carry_ref[...] = z[:, c - 1:c, :]
        return zm

    r, k, v = (mix(zr, cr, mu) for zr, cr, mu in zip(z_refs, carry_refs, mu_refs))
    kk_raw = k * kk_p
    kmod = k * (1.0 + (a - 1.0) * ka_p)
    if n_valid < c:
        valid = row < n_valid
        lw = jnp.where(valid, lw, 0.0)
        kmod = jnp.where(valid, kmod, 0.0)
        v = jnp.where(valid, v, 0.0)
        a = jnp.where(valid, a, 0.0)
    cl = _cumsum_time(lw, 1)

    pairs = lambda x: _split_heads(x, pw)
    rk_pair = jnp.concatenate([_split_heads(rk_p[None], pw)] * sb, axis=0)
    kkp, ap, kmp, vp, rp, clp, lwp = map(pairs, (kk_raw, a, kmod, v, r, cl, lw))
    kkp = kkp / jnp.maximum(jnp.sqrt(_pair_sum(kkp * kkp, hd)), KK_EPS)
    bp = kkp * ap
    cl_last = clp[:, c - 1:c, :]
    e_neg = jnp.exp(-clp)
    e_end = jnp.exp(cl_last - clp)
    return dict(
        lhs=jnp.concatenate([kkp * jnp.exp(clp - lwp), rp * jnp.exp(clp)], axis=1),
        bd_b=_pair_diag(bp * e_neg, hd), bd_k=_pair_diag(kmp * e_neg, hd), bd_v=_pair_diag(vp, hd),
        k_end=kmp * e_end, b_end=bp * e_end, vp=vp,
        bonus=_pair_sum(rp * kmp * rk_pair, hd) * vp,
        e_tot=jnp.exp(cl_last))


def _rwkv_solve(p, s0, lnw_pair, lnb_pair, hd, c):
    pw = 2 * hd
    ti = lax.broadcasted_iota(jnp.int32, (c, 2 * c), 0)
    si = lax.broadcasted_iota(jnp.int32, (c, 2 * c), 1)
    si = jnp.where(si >= c, si - c, si)
    strict = (ti > si)[None]
    lower = (ti >= si)[None]
    lhs = p["lhs"]
    ab = _bdot(lhs, p["bd_b"], BNT)
    ak = _bdot(lhs, p["bd_k"], BNT)
    su = _bdot(lhs, s0, BNT)
    x = su[:, :c] + _bdot(jnp.where(strict, ak[:, :c], 0.0), p["bd_v"], BNN)
    lp = jnp.where(strict, ab[:, :c], 0.0)
    x = x - _bdot(lp, _pair_diag(x, hd), BNN)
    pw2 = 2
    while pw2 < c:
        lp = _bdot(lp, _pair_diag(lp, c), BNN)
        x = x + _bdot(lp, _pair_diag(x, hd), BNN)
        pw2 *= 2
    y = (su[:, c:] + _bdot(jnp.where(lower, ak[:, c:], 0.0), p["bd_v"], BNN)
         - _bdot(jnp.where(lower, ab[:, c:], 0.0), _pair_diag(x, hd), BNN))
    full = s0 * p["e_tot"] + _bdot(p["vp"], p["k_end"], BTN) - _bdot(x, p["b_end"], BTN)
    same_head = ((lax.broadcasted_iota(jnp.int32, (pw, pw), 0) < hd)
                 == (lax.broadcasted_iota(jnp.int32, (pw, pw), 1) < hd))[None]
    s_new = jnp.where(same_head, full, 0.0)
    mean = _pair_sum(y, hd) * (1.0 / hd)
    var = _pair_sum(jnp.square(y - mean), hd) * (1.0 / hd)
    yn = (y - mean) * lax.rsqrt(var + GN_EPS) * lnw_pair + lnb_pair
    return yn + p["bonus"], s_new


def _rwkv_body(*refs, n_valid, hd, n_alias):
    (zr_ref, zk_ref, zv_ref, zg_ref, lw_ref, a_ref, pr_ref, pk_ref, pv_ref, s0_ref,
     mur_ref, muk_ref, muv_ref, kk_ref, ka_ref, rk_ref, lnw_ref, lnb_ref) = refs[:_RWKV_INPUTS]
    o_ref, sout_ref, s_scr, cr_scr, ck_scr, cv_scr = refs[_RWKV_INPUTS + n_alias:]
    j = pl.program_id(2)
    sb, c, hbw = zr_ref.shape
    pw = 2 * hd
    npair = hbw // pw
    n = sb * npair

    @pl.when(j == 0)
    def _():
        s0 = s0_ref[...].reshape(sb, npair, 2, hd, hd)
        zero = jnp.zeros((sb, npair, hd, hd), F32)
        top = jnp.concatenate([s0[:, :, 0], zero], axis=-1)
        bot = jnp.concatenate([zero, s0[:, :, 1]], axis=-1)
        s_scr[...] = jnp.concatenate([top, bot], axis=-2).reshape(n, pw, pw)
        cr_scr[...] = pr_ref[...]
        ck_scr[...] = pk_ref[...]
        cv_scr[...] = pv_ref[...]

    par_pair = lambda ref: jnp.concatenate([_split_heads(ref[...][None], pw)] * sb, axis=0)
    ops = _rwkv_prep((zr_ref, zk_ref, zv_ref), (cr_scr, ck_scr, cv_scr), (mur_ref, muk_ref, muv_ref),
                     lw_ref[...], a_ref[...], kk_ref[...], ka_ref[...], rk_ref[...], n_valid, hd)
    y, s_new = _rwkv_solve(ops, s_scr[...], par_pair(lnw_ref), par_pair(lnb_ref), hd, c)
    s_scr[...] = s_new
    out = _merge_heads(y, sb) * _silu(zg_ref[...])
    o_ref[...] = out.reshape(o_ref.shape).astype(o_ref.dtype)

    @pl.when(j == pl.num_programs(2) - 1)
    def _():
        s4 = s_new.reshape(sb, npair, pw, pw)
        both = jnp.stack([s4[:, :, :hd, :hd], s4[:, :, hd:, hd:]], axis=2)
        sout_ref[...] = both.reshape(sout_ref.shape)


def _rwkv(z3, lw, a, zprev, s0_all, l_in, mix, s_acc, l, depth, muz, kkp, kap, rkp, lnw, lnb,
          *, wa, wb, hbw, plan):
    b, t, _ = z3.shape
    sb, c = plan.sb_rwkv, plan.chunk
    hd = s0_all.shape[-1]
    nb = wb // hbw
    nh = hbw // hd
    off = 2 * wa // hbw
    zspec = lambda o: pl.BlockSpec((sb, c, hbw), lambda i, h, j, o=o: (i, j, o + h))
    pspec = lambda o: pl.BlockSpec((sb, 1, hbw), lambda i, h, j, o=o: (i, 0, o + h))
    mspec = lambda o: pl.BlockSpec((1, hbw), lambda i, h, j, o=o: (0, o + h))
    act = pl.BlockSpec((sb, c, hbw), lambda i, h, j: (i, j, h))
    st = lambda ll: pl.BlockSpec((None, sb, nh, hd, hd), lambda i, h, j, ll=ll: (ll, i, h, 0, 0))
    aliased = [mix] + ([] if s_acc is None else [s_acc])
    n_in = _RWKV_INPUTS
    aliases = {n_in: 0} if s_acc is None else {n_in: 0, n_in + 1: 1}
    out = pl.pallas_call(
        functools.partial(_rwkv_body, n_valid=min(plan.n_valid, c), hd=hd, n_alias=len(aliased)),
        grid=(b // sb, nb, t // c),
        in_specs=[zspec(off), zspec(off + nb), zspec(off + 2 * nb), zspec(off + 3 * nb), act, act,
                  pspec(0), pspec(nb), pspec(2 * nb), st(l_in),
                  mspec(0), mspec(nb), mspec(2 * nb), mspec(0), mspec(0), mspec(0), mspec(0), mspec(0)]
        + [_ANY] * len(aliased),
        out_specs=[_act_spec(plan, t, sb, c, hbw, lambda h: wa // hbw + h, 3), st(l)],
        out_shape=[jax.ShapeDtypeStruct(mix.shape, mix.dtype),
                   jax.ShapeDtypeStruct((depth,) + s0_all.shape[1:], F32)],
        input_output_aliases=aliases,
        scratch_shapes=[pltpu.VMEM((sb * nh // 2, 2 * hd, 2 * hd), F32), pltpu.VMEM((sb, 1, hbw), F32),
                        pltpu.VMEM((sb, 1, hbw), F32), pltpu.VMEM((sb, 1, hbw), F32)],
        compiler_params=_cparams(3),
        name="rwkv7",
    )(z3, z3, z3, z3, lw, a, zprev, zprev, zprev, s0_all, muz, muz, muz, kkp, kap, rkp, lnw, lnb,
      *aliased)
    return out


def _rwkv_steps_body(*refs, n_valid, tpad, hd, n_alias):
    (zr_ref, zk_ref, zv_ref, zg_ref, lw_ref, a_ref, pr_ref, pk_ref, pv_ref, s0_ref,
     mur_ref, muk_ref, muv_ref, kk_ref, ka_ref, rk_ref, lnw_ref, lnb_ref) = refs[:_RWKV_INPUTS]
    o_ref, sout_ref, o_scr, vt_scr, y_scr = refs[_RWKV_INPUTS + n_alias:]
    bsz, cols = pr_ref.shape
    nh = cols // hd
    sout_ref[...] = s0_ref[...]
    o_scr[...] = jnp.zeros_like(o_scr)
    per_col = lambda ref: jnp.broadcast_to(ref[...], (bsz, cols)).T
    lnw_c, lnb_c, rk_c = per_col(lnw_ref), per_col(lnb_ref), per_col(rk_ref)
    step_rows = lambda ref, t: ref[pl.ds(t, bsz, stride=tpad), :]
    prev = (pr_ref[...], pk_ref[...], pv_ref[...])
    for t in range(n_valid):
        z = (step_rows(zr_ref, t), step_rows(zk_ref, t), step_rows(zv_ref, t))
        r, k, v = (zz + (pp - zz) * mu[...] for zz, pp, mu in zip(z, prev, (mur_ref, muk_ref, muv_ref)))
        prev = z
        a = step_rows(a_ref, t)
        w_t = jnp.exp(step_rows(lw_ref, t)).T
        a_t = a.T
        r_t = r.T
        v_t = v.T
        kk_t = (k * kk_ref[...]).T
        km_t = (k * (1.0 + (a - 1.0) * ka_ref[...])).T
        vt_scr[...] = v_t
        for hh in range(nh):
            sl = slice(hh * hd, (hh + 1) * hd)
            kkh = kk_t[sl]
            kkh = kkh / jnp.maximum(jnp.sqrt(jnp.sum(kkh * kkh, axis=0, keepdims=True)), KK_EPS)
            bh, wh, kmh, rh = kkh * a_t[sl], w_t[sl], km_t[sl], r_t[sl]

            def value_row(vi, carry, hh=hh, kkh=kkh, bh=bh, wh=wh, kmh=kmh, rh=rh):
                s = sout_ref[hh, vi]
                sa = jnp.sum(s * kkh, axis=0, keepdims=True)
                s = s * wh - sa * bh + vt_scr[pl.ds(hh * hd + vi, 1), :] * kmh
                sout_ref[hh, vi] = s
                y_scr[pl.ds(hh * hd + vi, 1), :] = jnp.sum(s * rh, axis=0, keepdims=True)
                return carry

            lax.fori_loop(0, hd, value_row, 0, unroll=8)
        y = y_scr[...]
        outs = []
        for hh in range(nh):
            sl = slice(hh * hd, (hh + 1) * hd)
            yh = y[sl]
            mean = jnp.mean(yh, axis=0, keepdims=True)
            var = jnp.mean(jnp.square(yh - mean), axis=0, keepdims=True)
            yn = (yh - mean) * lax.rsqrt(var + GN_EPS) * lnw_c[sl] + lnb_c[sl]
            bonus = jnp.sum(r_t[sl] * km_t[sl] * rk_c[sl], axis=0, keepdims=True) * v_t[sl]
            outs.append(yn + bonus)
        out = jnp.concatenate(outs, axis=0).T * _silu(step_rows(zg_ref, t))
        o_scr[pl.ds(t, bsz, stride=tpad), :] = out
    o_ref[...] = o_scr[...].astype(o_ref.dtype)


def _rwkv_steps(z2, lw2, a2, zprev, s0_all, l_in, mix, s_acc, l, depth, muz, kkp, kap, rkp, lnw, lnb,
                *, wa, wb, plan, tpad):
    bsz = zprev.shape[0]
    rows = z2.shape[0]
    hd = s0_all.shape[-2]
    cols = LANES
    nb = wb // cols
    nh = cols // hd
    off = 2 * wa // cols
    zspec = lambda o: pl.BlockSpec((rows, cols), lambda h, o=o: (0, o + h))
    pspec = lambda o: pl.BlockSpec((bsz, cols), lambda h, o=o: (0, o + h))
    mspec = lambda o: pl.BlockSpec((1, cols), lambda h, o=o: (0, o + h))
    st = lambda ll: pl.BlockSpec((None, nh, hd, hd, bsz), lambda h, ll=ll: (ll, h, 0, 0, 0))
    aliased = [mix] + ([] if s_acc is None else [s_acc])
    n_in = _RWKV_INPUTS
    aliases = {n_in: 0} if s_acc is None else {n_in: 0, n_in + 1: 1}
    return pl.pallas_call(
        functools.partial(_rwkv_steps_body, n_valid=plan.n_valid, tpad=tpad, hd=hd, n_alias=len(aliased)),
        grid=(nb,),
        in_specs=[zspec(off), zspec(off + nb), zspec(off + 2 * nb), zspec(off + 3 * nb),
                  zspec(0), zspec(0), pspec(0), pspec(nb), pspec(2 * nb), st(l_in),
                  mspec(0), mspec(nb), mspec(2 * nb), mspec(0), mspec(0), mspec(0), mspec(0), mspec(0)]
        + [_ANY] * len(aliased),
        out_specs=[pl.BlockSpec((rows, cols), lambda h: (0, wa // cols + h)), st(l)],
        out_shape=[jax.ShapeDtypeStruct(mix.shape, mix.dtype),
                   jax.ShapeDtypeStruct((depth,) + s0_all.shape[1:], F32)],
        input_output_aliases=aliases,
        scratch_shapes=[pltpu.VMEM((rows, cols), F32), pltpu.VMEM((cols, bsz), F32),
                        pltpu.VMEM((cols, bsz), F32)],
        compiler_params=_cparams(1),
        name="rwkv7_steps",
    )(z2, z2, z2, z2, lw2, a2, zprev, zprev, zprev, s0_all, muz, muz, muz, kkp, kap, rkp, lnw, lnb,
      *aliased)


def _hgrn_head(q, f, v, g, s0, lb, ng, n_valid, sub, stage):
    b2_ref, kf_ref, v_ref = stage
    bcast_row = lambda ref, i: ref[pl.ds(i, 1), :]
    c, dk = q.shape
    sig = jax.nn.sigmoid(f)
    logg = jnp.log(jnp.maximum(lb + (1.0 - lb) * sig, TINY))
    kf = (1.0 - lb) * jax.nn.sigmoid(-f)
    if n_valid < c:
        valid = lax.broadcasted_iota(jnp.int32, (c, dk), 0) < n_valid
        logg = jnp.where(valid, logg, 0.0)
        kf = jnp.where(valid, kf, 0.0)
    bc = _cumsum_time(logg, 0)
    b2 = bc * LOG2E
    b2_ref[...] = b2
    kf_ref[...] = kf
    v_ref[...] = v
    o_inter = _bdot(q * jnp.exp(bc), s0)
    pairs, w_rows = [], []
    for lo in range(0, c, sub):
        for r0 in range(lo, lo + sub, SUBLANES):
            qg, bg = q[r0:r0 + SUBLANES], b2[r0:r0 + SUBLANES]
            for src in range(lo, min(r0 + SUBLANES, n_valid)):
                pairs.append((r0, src))
                w_rows.append(qg * jnp.exp2(bg - bcast_row(b2_ref, src)) * bcast_row(kf_ref, src))
    att_rows = _bdot(jnp.concatenate(w_rows, axis=0), jnp.ones((dk, v.shape[1]), BF16))
    t_row = lax.broadcasted_iota(jnp.int32, (SUBLANES, 1), 0)
    groups = {}
    for u, (r0, src) in enumerate(pairs):
        col = att_rows[u * SUBLANES:(u + 1) * SUBLANES]
        if src > r0:
            col = jnp.where(t_row >= src - r0, col, 0.0)
        term = col * bcast_row(v_ref, src)
        groups[r0] = groups[r0] + term if r0 in groups else term
    blocks = []
    for lo in range(0, c, sub):
        ob = o_inter[lo:lo + sub]
        if lo > 0:
            bs = bc[lo - 1:lo]
            att = _bdot(q[lo:lo + sub] * jnp.exp(bc[lo:lo + sub] - bs), kf[:lo] * jnp.exp(bs - bc[:lo]), NT)
            ob = ob + _bdot(att, v[:lo])
        none = jnp.zeros((SUBLANES, v.shape[1]), F32)
        blocks.append(ob + jnp.concatenate([groups.get(r0, none) for r0 in range(lo, lo + sub, SUBLANES)],
                                           axis=0))
    o = jnp.concatenate(blocks, axis=0) if len(blocks) > 1 else blocks[0]
    b_last = bc[c - 1:c]
    e_col = jnp.broadcast_to(jnp.exp(b_last), (v.shape[1], dk)).T
    s_new = e_col * s0 + _bdot(kf * jnp.exp(b_last - bc), v, TN)
    on = o * lax.rsqrt(jnp.mean(o * o, axis=-1, keepdims=True) + RMS_EPS) * ng
    return on * _silu(g), s_new


_HGRN_INPUTS = 7


def _hgrn_body(*refs, n_valid, dk, sub, n_alias):
    q_ref, f_ref, i_ref, g_ref, s0_ref, lb_ref, ng_ref = refs[:_HGRN_INPUTS]
    o_ref, sout_ref, s_scr, o_scr, b2_scr, kf_scr, v_scr = refs[_HGRN_INPUTS + n_alias:]
    j = pl.program_id(2)

    @pl.when(j == 0)
    def _():
        s_scr[...] = s0_ref[...]

    sb, c, hcw = q_ref.shape
    nh = hcw // dk
    lb = lb_ref[...]
    ng = ng_ref[...]

    def one(s, carry, slot0=0):
        q, f, v, g = q_ref[s], f_ref[s], i_ref[s], g_ref[s]
        outs = []
        for hh in range(nh):
            sl = slice(hh * dk, (hh + 1) * dk)
            slot = slot0 + hh
            o, s_new = _hgrn_head(q[:, sl], f[:, sl], v[:, sl], g[:, sl], s_scr[s, hh],
                                  lb[:, sl], ng[:, sl], n_valid, sub,
                                  (b2_scr.at[slot], kf_scr.at[slot], v_scr.at[slot]))
            s_scr[s, hh] = s_new
            outs.append(o)
        o_scr[s] = jnp.concatenate(outs, axis=1) if nh > 1 else outs[0]
        return carry

    unroll = math.gcd(sb, HGRN_UNROLL_SEQS)

    def group(gi, carry):
        for u in range(unroll):
            one(gi * unroll + u, carry, u * nh)
        return carry

    if sb == unroll:
        group(0, 0)
    else:
        lax.fori_loop(0, sb // unroll, group, 0)
    o_ref[...] = o_scr[...].reshape(o_ref.shape).astype(o_ref.dtype)

    @pl.when(j == pl.num_programs(2) - 1)
    def _():
        sout_ref[...] = s_scr[...]


def _hgrn(z3, s0_all, l_in, mix, s_acc, l, depth, lb, ng, *, col0, out_col0, wc, hcw, plan):
    b, t, _ = z3.shape
    sb, c = plan.sb_hgrn, plan.chunk
    dk, dv = s0_all.shape[-2], s0_all.shape[-1]
    nb = wc // hcw
    nh = hcw // dk
    off = col0 // hcw
    zspec = lambda o: pl.BlockSpec((sb, c, hcw), lambda i, h, j, o=o: (i, j, o + h))
    st = lambda ll: pl.BlockSpec((None, sb, nh, dk, dv), lambda i, h, j, ll=ll: (ll, i, h, 0, 0))
    par = pl.BlockSpec((1, hcw), lambda i, h, j: (0, h))
    aliased = [mix] + ([] if s_acc is None else [s_acc])
    n_in = _HGRN_INPUTS
    aliases = {n_in: 0} if s_acc is None else {n_in: 0, n_in + 1: 1}
    return pl.pallas_call(
        functools.partial(_hgrn_body, n_valid=min(plan.n_valid, c), dk=dk, sub=min(2 * SUBLANES, c),
                          n_alias=len(aliased)),
        grid=(b // sb, nb, t // c),
        in_specs=[zspec(off), zspec(off + nb), zspec(off + 2 * nb), zspec(off + 3 * nb), st(l_in),
                  par, par] + [_ANY] * len(aliased),
        out_specs=[_act_spec(plan, t, sb, c, hcw, lambda h: out_col0 // hcw + h, 3), st(l)],
        out_shape=[jax.ShapeDtypeStruct(mix.shape, mix.dtype),
                   jax.ShapeDtypeStruct((depth,) + s0_all.shape[1:], F32)],
        input_output_aliases=aliases,
        scratch_shapes=[pltpu.VMEM((sb, nh, dk, dv), F32), pltpu.VMEM((sb, c, hcw), F32)]
        + [pltpu.VMEM((nh * math.gcd(sb, HGRN_UNROLL_SEQS), c, dk), F32)] * 3,
        compiler_params=_cparams(3),
        name="hgrn2",
    )(z3, z3, z3, z3, s0_all, lb, ng, *aliased)


def _post1_body(h_ref, m_ref, g_ref, h1_ref, hn_ref):
    m = m_ref[...].astype(F32)
    h1 = h_ref[...] + m * lax.rsqrt(jnp.mean(m * m, axis=-1, keepdims=True) + RMS_EPS) * g_ref[...]
    h1_ref[...] = h1
    hn_ref[...] = (h1 * lax.rsqrt(jnp.mean(h1 * h1, axis=-1, keepdims=True) + RMS_EPS)).astype(hn_ref.dtype)


def _post2_body(h_ref, p_ref, proj_ref, gate_ref, g_ref, o_ref):
    u = jnp.dot(p_ref[...], proj_ref[...], preferred_element_type=F32)
    x = u * jax.nn.sigmoid(gate_ref[...].astype(F32))
    o_ref[...] = h_ref[...] + x * lax.rsqrt(jnp.mean(x * x, axis=-1, keepdims=True) + RMS_EPS) * g_ref[...]


def _post_ple(h1, p2, proj_all, l, gate, gain, tr=256):
    n, d = h1.shape
    kp = p2.shape[1]
    tr = min(tr, n)
    row = pl.BlockSpec((tr, d), lambda i: (i, 0))
    return pl.pallas_call(
        _post2_body,
        grid=(n // tr,),
        in_specs=[row, pl.BlockSpec((tr, kp), lambda i: (i, 0)),
                  pl.BlockSpec((None, kp, d), lambda i: (l, 0, 0)), row,
                  pl.BlockSpec((1, d), lambda i: (0, 0))],
        out_specs=row,
        out_shape=jax.ShapeDtypeStruct((n, d), F32),
        compiler_params=_cparams(1),
        name="post_ple",
    )(h1, p2, proj_all, gate, gain)


def _rowwise(body, name, arrays, gain, out_dtypes, tr=256):
    n, d = arrays[0].shape
    tr = min(tr, n)
    row = pl.BlockSpec((tr, d), lambda i: (i, 0))
    return pl.pallas_call(
        body,
        grid=(n // tr,),
        in_specs=[row] * len(arrays) + [pl.BlockSpec((1, d), lambda i: (0, 0))],
        out_specs=[row] * len(out_dtypes),
        out_shape=[jax.ShapeDtypeStruct((n, d), dt) for dt in out_dtypes],
        compiler_params=_cparams(1),
        name=name,
    )(*arrays, gain)


def _block_diag_tiles(w):
    nb, bs, _ = w.shape
    per = LANES // bs
    w = w.reshape(nb // per, per, bs, bs)
    eye = jnp.eye(per, dtype=w.dtype)
    return jnp.einsum('tpab,pq->tpaqb', w, eye).reshape(nb // per, LANES, LANES)


def _layer_stack(x, p, st, W, mm, plan, zero_state):
    st_a_h, st_a_conv, st_b_s, st_b_x, st_b_z, st_c_s = st
    b, t, d = x.shape
    depth = W['g_pre'].shape[0]
    wa = st_a_h.shape[-1]
    hd = st_b_s.shape[-1]
    wb = st_b_s.shape[-3] * hd
    dk = st_c_s.shape[-2]
    wc = st_c_s.shape[-3] * dk
    mix_w = wa + wb + wc
    n = b * t
    nv = plan.n_valid
    hbw = math.gcd(math.gcd(2 * wa, wb), math.gcd(wa, 512))
    hcw = math.gcd(math.gcd(2 * wa + 4 * wb, wc), math.gcd(wa + wb, 512))

    lb_soft = jax.nn.softmax(W['hgrn_lb'].astype(F32), axis=0)
    lb_all = jnp.cumsum(lb_soft, axis=0) - lb_soft[0]
    p_act = p.astype(BF16)

    if plan.rwkv_steps:
        st_b_s = jnp.transpose(st_b_s, (0, 2, 3, 4, 1))
    rwkv_params = lambda l: (
        W['rwkv_mu_z'][l][None], W['rwkv_kk'][l][None], W['rwkv_ka'][l][None],
        W['rwkv_rk'][l].reshape(1, wb), W['rwkv_lnx_w'][l][None], W['rwkv_lnx_b'][l][None])

    h = x
    small = [[] for _ in range(4)]
    sb_acc = sc_acc = None
    for l in range(depth):
        ls = 0 if zero_state else l
        xn, lw, a, xlast = _prenorm(
            h, W['g_pre'][l].reshape(1, 1, d), st_b_x[ls][:, None, :], W['rwkv_lora_in'][l],
            W['rwkv_w2'][l], W['rwkv_a2'][l], W['rwkv_w0'][l][None], W['rwkv_a0'][l][None], plan)
        z3 = mm('w_in', xn.reshape(n, d), l, F32).reshape(b, t, -1)
        cb8 = jnp.pad(st_a_conv[ls], ((0, 0), (SUBLANES - (CONV_W - 1), 0), (0, 0)))
        mix, ha, tail = _rglru(
            z3, st_a_h[ls][:, None, :], cb8, W['conv_a_w'][l][None], W['conv_a_b'][l].reshape(1, 1, wa),
            W['lru_wr_bd'][l], W['lru_br'][l][None], W['lru_wi_bd'][l], W['lru_bi'][l][None],
            W['lru_lambda'][l][None], wa=wa, mix_w=mix_w, plan=plan)
        if plan.rwkv_steps:
            mix, sb_acc = _rwkv_steps(
                z3.reshape(n, -1), lw.reshape(n, wb), a.reshape(n, wb), st_b_z[ls], st_b_s, ls,
                mix, sb_acc, l, depth, *rwkv_params(l), wa=wa, wb=wb, plan=plan, tpad=t)
        else:
            mix, sb_acc = _rwkv(
                z3, lw, a, st_b_z[ls][:, None, :], st_b_s, ls, mix, sb_acc, l, depth,
                *rwkv_params(l), wa=wa, wb=wb, hbw=hbw, plan=plan)
        mix, sc_acc = _hgrn(
            z3, st_c_s, ls, mix, sc_acc, l, depth, lb_all[l][None], W['hgrn_norm_g'][l][None],
            col0=2 * wa + 4 * wb, out_col0=wa + wb, wc=wc, hcw=hcw, plan=plan)
        mo = mm('w_out', mix.reshape(n, mix_w), l, BF16)
        h1, hn = _rowwise(_post1_body, "post_mix", [h.reshape(n, d), mo], W['g_post'][l][None],
                          [F32, BF16])
        gp = mm('ple_gate', hn, l, BF16)
        h = _post_ple(h1, p_act[l].reshape(n, -1), W['ple_proj'], l, gp, W['g_ple'][l][None]).reshape(b, t, d)
        small[0].append(ha[:, 0])
        small[1].append(tail[:, SUBLANES - (CONV_W - 1):])
        small[2].append(xlast[:, 0])
        small[3].append(z3[:, nv - 1, 2 * wa:2 * wa + 3 * wb])
    na_h, na_c, nb_x, nb_z = (jnp.stack(o) for o in small)
    if plan.rwkv_steps:
        sb_acc = jnp.transpose(sb_acc, (0, 4, 1, 2, 3))
    return h, (na_h, na_c, sb_acc, nb_x, nb_z, sc_acc)


def kernel(x_prompt, x_sample, p_prompt, p_sample, state_a_h, state_a_conv, state_b_S,
           state_b_xprev, state_b_zprev, state_c_S, g_pre, g_post, w_in, w_out,
           conv_a_w, conv_a_b, lru_wr, lru_br, lru_wi, lru_bi, lru_lambda,
           rwkv_mu_z, rwkv_mu_w, rwkv_mu_a, rwkv_w0, rwkv_w1, rwkv_w2, rwkv_a0, rwkv_a1,
           rwkv_a2, rwkv_kk, rwkv_ka, rwkv_rk, rwkv_lnx_w, rwkv_lnx_b, hgrn_lb, hgrn_norm_g,
           ple_proj, ple_gate, g_ple):
    depth = w_in.shape[0]
    W = dict(g_pre=g_pre, g_post=g_post, conv_a_w=conv_a_w, conv_a_b=conv_a_b,
             lru_br=lru_br, lru_bi=lru_bi, lru_lambda=lru_lambda, rwkv_mu_z=rwkv_mu_z,
             rwkv_mu_w=rwkv_mu_w, rwkv_mu_a=rwkv_mu_a, rwkv_w0=rwkv_w0, rwkv_a0=rwkv_a0,
             rwkv_kk=rwkv_kk, rwkv_ka=rwkv_ka, rwkv_rk=rwkv_rk, rwkv_lnx_w=rwkv_lnx_w,
             rwkv_lnx_b=rwkv_lnx_b, hgrn_lb=hgrn_lb, hgrn_norm_g=hgrn_norm_g, g_ple=g_ple)
    for name, w in (('rwkv_w2', rwkv_w2), ('rwkv_a2', rwkv_a2), ('ple_proj', ple_proj)):
        W[name] = w.astype(BF16)
    mw, ma = rwkv_mu_w[:, :, None], rwkv_mu_a[:, :, None]
    W['rwkv_lora_in'] = jnp.concatenate(
        [(1.0 - mw) * rwkv_w1, (1.0 - ma) * rwkv_a1, mw * rwkv_w1, ma * rwkv_a1], axis=-1).astype(BF16)
    W['lru_wr_bd'] = jnp.stack([_block_diag_tiles(lru_wr[l]) for l in range(depth)]).astype(BF16)
    W['lru_wi_bd'] = jnp.stack([_block_diag_tiles(lru_wi[l]) for l in range(depth)]).astype(BF16)

    bp, tp, d = x_prompt.shape
    bs, ts, _ = x_sample.shape
    dt = x_prompt.dtype
    st_s = (state_a_h, state_a_conv, state_b_S, state_b_xprev, state_b_zprev, state_c_S)
    st_p = tuple(jnp.zeros((1, bp) + s.shape[2:], dt) for s in st_s)

    big = dict(w_in=w_in, w_out=w_out, ple_gate=ple_gate)
    wq = {name: [None] * depth for name in big}

    def mm_sample(name, x, l, out_dtype):
        if x.shape[0] <= MM_TILE:
            y, wq[name][l] = _matmul_wcast(x, big[name], l, out_dtype)
            return y
        wq[name][l] = big[name][l].astype(BF16)
        return _matmul(x, wq[name][l], out_dtype)

    def mm_prompt(name, x, l, out_dtype):
        return _matmul(x, wq[name][l], out_dtype)

    tpad = -(-ts // SUBLANES) * SUBLANES
    pad_t = lambda v, ax: jnp.pad(v, [(0, tpad - ts) if i == ax else (0, 0) for i in range(v.ndim)])
    y_s, out_s = _layer_stack(pad_t(x_sample, 1), pad_t(p_sample, 2), st_s, W, mm_sample,
                              _make_plan(bs, tpad, ts), False)
    y_p, out_p = _layer_stack(x_prompt, p_prompt, st_p, W, mm_prompt, _make_plan(bp, tp, tp), True)
    return (y_p, y_s[:, :ts]) + out_p + out_s
```

```python
import functools
import math
from typing import NamedTuple

import jax
import jax.numpy as jnp
from jax import lax
from jax.experimental import pallas as pl
from jax.experimental.pallas import tpu as pltpu

F32 = jnp.float32
BF16 = jnp.bfloat16

RMS_EPS = 1e-6
GN_EPS = 64e-5
LRU_C = 8.0
TINY = 1e-30
KK_EPS = 1e-12
CONV_W = 4
LOG2E = 1.4426950408889634
NORM_GROUP_ROWS = 128
HGRN_UNROLL_SEQS = 4

LANES = 128
SUBLANES = 8
BF16_ROWS = 16
VMEM_LIMIT = 52 * 1024 * 1024

NT = (((1,), (1,)), ((), ()))
TN = (((0,), (0,)), ((), ()))
BNT = (((2,), (2,)), ((0,), (0,)))
BNN = (((2,), (1,)), ((0,), (0,)))
BTN = (((1,), (1,)), ((0,), (0,)))


class Plan(NamedTuple):
    n_valid: int
    tt: int
    sb_rows: int
    tt_norm: int
    sb_norm: int
    chunk: int
    sb_rwkv: int
    sb_hgrn: int
    flat_acts: bool
    rwkv_steps: bool


def _make_plan(b, t, n_valid):
    tt = min(t, 128)
    chunk = min(t, 64)
    rows = 128
    sb = max(1, min(b, rows // tt))
    tt_norm = min(t, 2 * rows)
    sb_norm = max(1, min(b, 2 * rows // tt_norm))
    flat = chunk == t
    assert flat or (chunk % BF16_ROWS == 0 and tt % BF16_ROWS == 0)
    return Plan(n_valid=n_valid, tt=tt, sb_rows=sb, tt_norm=tt_norm, sb_norm=sb_norm, chunk=chunk,
                sb_rwkv=math.gcd(b, max(4, rows // chunk)),
                sb_hgrn=sb if chunk < 64 else math.gcd(b, HGRN_UNROLL_SEQS),
                flat_acts=flat, rwkv_steps=flat and b % LANES == 0 and t <= SUBLANES)


def _act_shape(plan, b, t, width):
    return jax.ShapeDtypeStruct((b * t, width) if plan.flat_acts else (b, t, width), BF16)


def _act_spec(plan, t, sb, rows, width, col, grid_rank):
    if grid_rank == 2:
        if plan.flat_acts:
            return pl.BlockSpec((sb * rows, width), lambda i, j: (i * (t // rows) + j, col(0)))
        return pl.BlockSpec((sb, rows, width), lambda i, j: (i, j, col(0)))
    if plan.flat_acts:
        return pl.BlockSpec((sb * rows, width), lambda i, h, j: (i * (t // rows) + j, col(h)))
    return pl.BlockSpec((sb, rows, width), lambda i, h, j: (i, j, col(h)))


def _cparams(n_axes):
    return pltpu.CompilerParams(dimension_semantics=("arbitrary",) * n_axes,
                                vmem_limit_bytes=VMEM_LIMIT)


def _softplus(x):
    return jnp.maximum(x, 0.0) + jnp.log1p(jnp.exp(-jnp.abs(x)))


def _silu(x):
    return x * jax.nn.sigmoid(x)


def _bdot(a, b, dims=None):
    a = a.astype(BF16)
    b = b.astype(BF16)
    if dims is None:
        return jnp.dot(a, b, preferred_element_type=F32)
    return lax.dot_general(a, b, dims, preferred_element_type=F32)


def _cumsum_time(x, axis):
    n = x.shape[axis]
    idx = lax.broadcasted_iota(jnp.int32, x.shape, axis)
    d = 1
    while d < n:
        x = x + jnp.where(idx >= d, pltpu.roll(x, d, axis), 0.0)
        d *= 2
    return x


_ANY = pl.BlockSpec(memory_space=pl.ANY)


MM_TILE = 1024


def _mm_body(x_ref, w_ref, o_ref):
    o_ref[...] = jnp.dot(x_ref[...], w_ref[...], preferred_element_type=F32).astype(o_ref.dtype)


def _mm_wcast_body(x_ref, w_ref, o_ref, wb_ref):
    wb = w_ref[...].astype(BF16)
    wb_ref[...] = wb
    o_ref[...] = jnp.dot(x_ref[...], wb, preferred_element_type=F32).astype(o_ref.dtype)


def _matmul(x, w, out_dtype):
    m, k = x.shape
    n = w.shape[1]
    tm = math.gcd(m, MM_TILE)
    tn = math.gcd(n, MM_TILE)
    return pl.pallas_call(
        _mm_body,
        grid=(m // tm, n // tn),
        in_specs=[pl.BlockSpec((tm, k), lambda i, j: (i, 0)),
                  pl.BlockSpec((k, tn), lambda i, j: (0, j))],
        out_specs=pl.BlockSpec((tm, tn), lambda i, j: (i, j)),
        out_shape=jax.ShapeDtypeStruct((m, n), out_dtype),
        compiler_params=_cparams(2),
        name="matmul",
    )(x, w)


def _matmul_wcast(x, w_all, l, out_dtype):
    m, k = x.shape
    n = w_all.shape[2]
    tn = math.gcd(n, MM_TILE // 2)
    return pl.pallas_call(
        _mm_wcast_body,
        grid=(n // tn,),
        in_specs=[pl.BlockSpec((m, k), lambda j: (0, 0)),
                  pl.BlockSpec((None, k, tn), lambda j: (l, 0, j))],
        out_specs=[pl.BlockSpec((m, tn), lambda j: (0, j)),
                   pl.BlockSpec((k, tn), lambda j: (0, j))],
        out_shape=[jax.ShapeDtypeStruct((m, n), out_dtype), jax.ShapeDtypeStruct((k, n), BF16)],
        compiler_params=_cparams(1),
        name="matmul_wcast",
    )(x, w_all)


def _prenorm_body(h_ref, g_ref, xprev_ref, wf_ref, w2_ref, a2_ref, w0_ref, a0_ref,
                  xn_ref, lw_ref, a_ref, xlast_ref, carry_ref, *, last_row):
    j = pl.program_id(1)
    sb, tt, d = h_ref.shape
    r2 = wf_ref.shape[1] // 2
    rank = r2 // 2

    @pl.when(j == 0)
    def _():
        xp = jnp.broadcast_to(xprev_ref[...], (sb, SUBLANES, d)).reshape(sb * SUBLANES, d)
        carry_ref[...] = _bdot(xp, wf_ref[:, r2:]).reshape(sb, SUBLANES, r2)[:, 0:1, :]

    gt = min(tt, NORM_GROUP_ROWS)
    gs = max(1, min(sb, NORM_GROUP_ROWS // gt))
    for s0 in range(0, sb, gs):
        for t0 in range(0, tt, gt):
            h = h_ref[s0:s0 + gs, t0:t0 + gt, :]
            xn = h * lax.rsqrt(jnp.mean(h * h, axis=-1, keepdims=True) + RMS_EPS) * g_ref[...]
            xn_b = xn.reshape(gs * gt, d).astype(BF16)
            if len(xn_ref.shape) == 2:
                xn_ref[s0 * tt + t0:s0 * tt + t0 + gs * gt, :] = xn_b
            else:
                xn_ref[s0:s0 + gs, t0:t0 + gt, :] = xn_b.reshape(gs, gt, d)
            if t0 <= last_row < t0 + gt:
                @pl.when(j == pl.num_programs(1) - 1)
                def _(xn=xn, s0=s0, t0=t0):
                    xlast_ref[s0:s0 + gs] = xn[:, last_row - t0:last_row - t0 + 1, :]

            prod = jnp.dot(xn_b, wf_ref[...], preferred_element_type=F32)
            shifted = prod[:, r2:].reshape(gs, gt, r2)
            t_idx = lax.broadcasted_iota(jnp.int32, (gs, gt, r2), 1)
            prev = jnp.where(t_idx == 0, carry_ref[s0:s0 + gs], pltpu.roll(shifted, 1, 1))
            carry_ref[s0:s0 + gs] = shifted[:, gt - 1:gt, :]
            pre = prod[:, :r2] + prev.reshape(gs * gt, r2)
            yw = w0_ref[...] + _bdot(jnp.tanh(pre[:, :rank]), w2_ref[...])
            wl = -_softplus(-yw) - 0.5
            lw_ref[s0:s0 + gs, t0:t0 + gt, :] = (-jnp.exp(wl)).reshape(gs, gt, -1)
            ya = a0_ref[...] + _bdot(pre[:, rank:], a2_ref[...])
            a_ref[s0:s0 + gs, t0:t0 + gt, :] = jax.nn.sigmoid(ya).reshape(gs, gt, -1)


def _prenorm(h3, g, xprev, wf, w2, a2, w0, a0, plan):
    b, t, d = h3.shape
    sb, tt = plan.sb_norm, plan.tt_norm
    wb = w2.shape[1]
    rank = w2.shape[0]
    row3 = lambda i, j: (i, j, 0)
    par3 = lambda i, j: (0, 0, 0)
    par2 = lambda i, j: (0, 0)
    seq3 = lambda i, j: (i, 0, 0)
    return pl.pallas_call(
        functools.partial(_prenorm_body, last_row=(plan.n_valid - 1) % tt),
        grid=(b // sb, t // tt),
        in_specs=[pl.BlockSpec((sb, tt, d), row3),
                  pl.BlockSpec((1, 1, d), par3),
                  pl.BlockSpec((sb, 1, d), seq3),
                  pl.BlockSpec((d, 4 * rank), par2),
                  pl.BlockSpec((rank, wb), par2),
                  pl.BlockSpec((rank, wb), par2),
                  pl.BlockSpec((1, wb), par2),
                  pl.BlockSpec((1, wb), par2)],
        out_specs=[_act_spec(plan, t, sb, tt, d, lambda h: 0, 2),
                   pl.BlockSpec((sb, tt, wb), row3),
                   pl.BlockSpec((sb, tt, wb), row3),
                   pl.BlockSpec((sb, 1, d), seq3)],
        out_shape=[_act_shape(plan, b, t, d),
                   jax.ShapeDtypeStruct((b, t, wb), F32),
                   jax.ShapeDtypeStruct((b, t, wb), F32),
                   jax.ShapeDtypeStruct((b, 1, d), F32)],
        scratch_shapes=[pltpu.VMEM((sb, 1, 2 * rank), F32)],
        compiler_params=_cparams(2),
        name="prenorm_lora",
    )(h3, g, xprev, wf, w2, a2, w0, a0)


def _rglru_body(x_ref, gt_ref, h0_ref, cb_ref, cw_ref, cbias_ref, wr_ref, br_ref, wi_ref, bi_ref,
                lam_ref, o_ref, hout_ref, tail_ref, hc_ref, tl_ref, *, nv_last):
    j = pl.program_id(1)

    @pl.when(j == 0)
    def _():
        hc_ref[...] = h0_ref[...]
        tl_ref[...] = cb_ref[...]

    x = x_ref[...]
    sb, tt, w = x.shape
    ext = jnp.concatenate([tl_ref[...], x], axis=1)
    tl_ref[...] = ext[:, tt:tt + SUBLANES, :]
    u = cbias_ref[...]
    for tap in range(CONV_W):
        sh = CONV_W - 1 - tap
        xs = x if sh == 0 else pltpu.roll(ext, sh, 1)[:, SUBLANES:, :]
        u = u + xs * cw_ref[:, tap:tap + 1, :]
    u2 = u.reshape(sb * tt, w)
    rp, ip = [], []
    for m in range(w // LANES):
        um = u2[:, m * LANES:(m + 1) * LANES]
        rp.append(_bdot(um, wr_ref[m]))
        ip.append(_bdot(um, wi_ref[m]))
    r = jax.nn.sigmoid(jnp.concatenate(rp, axis=1) + br_ref[...])
    i = jax.nn.sigmoid(jnp.concatenate(ip, axis=1) + bi_ref[...])
    log_a = -LRU_C * r * _softplus(-lam_ref[...])
    th = jnp.tanh(log_a)
    bcoef = jnp.sqrt(-2.0 * th / (1.0 - th)) * (i * u2)
    nt = tt // SUBLANES
    a_cum = jnp.exp(log_a).reshape(sb * nt, SUBLANES, w)
    b_cum = bcoef.reshape(sb * nt, SUBLANES, w)
    t_idx = lax.broadcasted_iota(jnp.int32, (sb * nt, SUBLANES, w), 1)
    d = 1
    while d < SUBLANES:
        keep = t_idx >= d
        a_prev = jnp.where(keep, pltpu.roll(a_cum, d, 1), 1.0)
        b_prev = jnp.where(keep, pltpu.roll(b_cum, d, 1), 0.0)
        b_cum = a_cum * b_prev + b_cum
        a_cum = a_cum * a_prev
        d *= 2
    a_end = a_cum.reshape(sb, nt, SUBLANES, w)[:, :, SUBLANES - 1:, :]
    b_end = b_cum.reshape(sb, nt, SUBLANES, w)[:, :, SUBLANES - 1:, :]
    h_in = [hc_ref[...]]
    for g in range(nt):
        h_in.append(a_end[:, g] * h_in[g] + b_end[:, g])
    hc_ref[...] = h_in[nt]
    h_tile = jnp.stack(h_in[:nt], axis=1).reshape(sb * nt, 1, w)
    hs = (a_cum * h_tile + b_cum).reshape(sb, tt, w)
    out = (hs * _silu(gt_ref[...])).reshape(sb * tt, w)
    rest = jnp.zeros((sb * tt, o_ref.shape[-1] - w), F32)
    o_ref[...] = jnp.concatenate([out, rest], axis=-1).reshape(o_ref.shape).astype(o_ref.dtype)

    @pl.when(j == pl.num_programs(1) - 1)
    def _():
        hout_ref[...] = hs[:, nv_last - 1:nv_last, :]
        if nv_last == tt:
            tail_ref[...] = ext[:, tt:tt + SUBLANES, :]
        else:
            tail_ref[...] = pltpu.roll(ext, tt + SUBLANES - nv_last, 1)[:, 0:SUBLANES, :]


def _rglru(z3, h0, cb8, cw, cbias, wr_bd, br, wi_bd, bi, lam, *, wa, mix_w, plan):
    b, t, _ = z3.shape
    sb, tt = plan.sb_rows, plan.tt
    nt = wa // LANES
    par3 = lambda i, j: (0, 0, 0)
    seq3 = lambda i, j: (i, 0, 0)
    return pl.pallas_call(
        functools.partial(_rglru_body, nv_last=(plan.n_valid - 1) % tt + 1),
        grid=(b // sb, t // tt),
        in_specs=[pl.BlockSpec((sb, tt, wa), lambda i, j: (i, j, 0)),
                  pl.BlockSpec((sb, tt, wa), lambda i, j: (i, j, 1)),
                  pl.BlockSpec((sb, 1, wa), seq3),
                  pl.BlockSpec((sb, SUBLANES, wa), seq3),
                  pl.BlockSpec((1, CONV_W, wa), par3),
                  pl.BlockSpec((1, 1, wa), par3),
                  pl.BlockSpec((nt, LANES, LANES), par3),
                  pl.BlockSpec((1, wa), lambda i, j: (0, 0)),
                  pl.BlockSpec((nt, LANES, LANES), par3),
                  pl.BlockSpec((1, wa), lambda i, j: (0, 0)),
                  pl.BlockSpec((1, wa), lambda i, j: (0, 0))],
        out_specs=[_act_spec(plan, t, sb, tt, mix_w, lambda h: 0, 2),
                   pl.BlockSpec((sb, 1, wa), seq3),
                   pl.BlockSpec((sb, SUBLANES, wa), seq3)],
        out_shape=[_act_shape(plan, b, t, mix_w),
                   jax.ShapeDtypeStruct((b, 1, wa), F32),
                   jax.ShapeDtypeStruct((b, SUBLANES, wa), F32)],
        scratch_shapes=[pltpu.VMEM((sb, 1, wa), F32), pltpu.VMEM((sb, SUBLANES, wa), F32)],
        compiler_params=_cparams(2),
        name="rglru",
    )(z3, z3, h0, cb8, cw, cbias, wr_bd, br, wi_bd, bi, lam)


def _split_heads(x, hd):
    sb, c, w = x.shape
    nh = w // hd
    st = jnp.stack([x[:, :, h * hd:(h + 1) * hd] for h in range(nh)], axis=1)
    return st.reshape(sb * nh, c, hd)


def _merge_heads(x, sb):
    n, c, hd = x.shape
    nh = n // sb
    x4 = x.reshape(sb, nh, c, hd)
    return jnp.concatenate([x4[:, h] for h in range(nh)], axis=-1)


def _pair_diag(x, half):
    lo = lax.broadcasted_iota(jnp.int32, x.shape, 2) < half
    return jnp.concatenate([jnp.where(lo, x, 0.0), jnp.where(lo, 0.0, x)], axis=1)


def _pair_sum(x, half):
    lo = lax.broadcasted_iota(jnp.int32, x.shape, 2) < half
    s_lo = jnp.sum(jnp.where(lo, x, 0.0), axis=-1, keepdims=True)
    s_hi = jnp.sum(jnp.where(lo, 0.0, x), axis=-1, keepdims=True)
    return jnp.where(lo, s_lo, s_hi)


_RWKV_INPUTS = 18


def _rwkv_prep(z_refs, carry_refs, mu_refs, lw, a, kk_p, ka_p, rk_p, n_valid, hd):
    sb, c, hbw = lw.shape
    pw = 2 * hd
    row = lax.broadcasted_iota(jnp.int32, (sb, c, hbw), 1)
    first = row == 0

    def mix(z_ref, carry_ref, mu_ref):
        z = z_ref[...]
        zm = z + (jnp.where(first, carry_ref[...], pltpu.roll(z, 1, 1)) - z) * mu_ref[...]
        carry_ref[...] = z[:, c - 1:c, :]
        return zm

    r, k, v = (mix(zr, cr, mu) for zr, cr, mu in zip(z_refs, carry_refs, mu_refs))
    kk_raw = k * kk_p
    kmod = k * (1.0 + (a - 1.0) * ka_p)
    if n_valid < c:
        valid = row < n_valid
        lw = jnp.where(valid, lw, 0.0)
        kmod = jnp.where(valid, kmod, 0.0)
        v = jnp.where(valid, v, 0.0)
        a = jnp.where(valid, a, 0.0)
    cl = _cumsum_time(lw, 1)

    pairs = lambda x: _split_heads(x, pw)
    rk_pair = jnp.concatenate([_split_heads(rk_p[None], pw)] * sb, axis=0)
    kkp, ap, kmp, vp, rp, clp, lwp = map(pairs, (kk_raw, a, kmod, v, r, cl, lw))
    kkp = kkp / jnp.maximum(jnp.sqrt(_pair_sum(kkp * kkp, hd)), KK_EPS)
    bp = kkp * ap
    cl_last = clp[:, c - 1:c, :]
    e_neg = jnp.exp(-clp)
    e_end = jnp.exp(cl_last - clp)
    return dict(
        lhs=jnp.concatenate([kkp * jnp.exp(clp - lwp), rp * jnp.exp(clp)], axis=1),
        bd_b=_pair_diag(bp * e_neg, hd), bd_k=_pair_diag(kmp * e_neg, hd), bd_v=_pair_diag(vp, hd),
        k_end=kmp * e_end, b_end=bp * e_end, vp=vp,
        bonus=_pair_sum(rp * kmp * rk_pair, hd) * vp,
        e_tot=jnp.exp(cl_last))


def _rwkv_solve(p, s0, lnw_pair, lnb_pair, hd, c):
    pw = 2 * hd
    ti = lax.broadcasted_iota(jnp.int32, (c, 2 * c), 0)
    si = lax.broadcasted_iota(jnp.int32, (c, 2 * c), 1)
    si = jnp.where(si >= c, si - c, si)
    strict = (ti > si)[None]
    lower = (ti >= si)[None]
    lhs = p["lhs"]
    ab = _bdot(lhs, p["bd_b"], BNT)
    ak = _bdot(lhs, p["bd_k"], BNT)
    su = _bdot(lhs, s0, BNT)
    x = su[:, :c] + _bdot(jnp.where(strict, ak[:, :c], 0.0), p["bd_v"], BNN)
    lp = jnp.where(strict, ab[:, :c], 0.0)
    x = x - _bdot(lp, _pair_diag(x, hd), BNN)
    pw2 = 2
    while pw2 < c:
        lp = _bdot(lp, _pair_diag(lp, c), BNN)
        x = x + _bdot(lp, _pair_diag(x, hd), BNN)
        pw2 *= 2
    y = (su[:, c:] + _bdot(jnp.where(lower, ak[:, c:], 0.0), p["bd_v"], BNN)
         - _bdot(jnp.where(lower, ab[:, c:], 0.0), _pair_diag(x, hd), BNN))
    full = s0 * p["e_tot"] + _bdot(p["vp"], p["k_end"], BTN) - _bdot(x, p["b_end"], BTN)
    same_head = ((lax.broadcasted_iota(jnp.int32, (pw, pw), 0) < hd)
                 == (lax.broadcasted_iota(jnp.int32, (pw, pw), 1) < hd))[None]
    s_new = jnp.where(same_head, full, 0.0)
    mean = _pair_sum(y, hd) * (1.0 / hd)
    var = _pair_sum(jnp.square(y - mean), hd) * (1.0 / hd)
    yn = (y - mean) * lax.rsqrt(var + GN_EPS) * lnw_pair + lnb_pair
    return yn + p["bonus"], s_new


def _rwkv_body(*refs, n_valid, hd, n_alias):
    (zr_ref, zk_ref, zv_ref, zg_ref, lw_ref, a_ref, pr_ref, pk_ref, pv_ref, s0_ref,
     mur_ref, muk_ref, muv_ref, kk_ref, ka_ref, rk_ref, lnw_ref, lnb_ref) = refs[:_RWKV_INPUTS]
    o_ref, sout_ref, s_scr, cr_scr, ck_scr, cv_scr = refs[_RWKV_INPUTS + n_alias:]
    j = pl.program_id(2)
    sb, c, hbw = zr_ref.shape
    pw = 2 * hd
    npair = hbw // pw
    n = sb * npair

    @pl.when(j == 0)
    def _():
        s0 = s0_ref[...].reshape(sb, npair, 2, hd, hd)
        zero = jnp.zeros((sb, npair, hd, hd), F32)
        top = jnp.concatenate([s0[:, :, 0], zero], axis=-1)
        bot = jnp.concatenate([zero, s0[:, :, 1]], axis=-1)
        s_scr[...] = jnp.concatenate([top, bot], axis=-2).reshape(n, pw, pw)
        cr_scr[...] = pr_ref[...]
        ck_scr[...] = pk_ref[...]
        cv_scr[...] = pv_ref[...]

    par_pair = lambda ref: jnp.concatenate([_split_heads(ref[...][None], pw)] * sb, axis=0)
    ops = _rwkv_prep((zr_ref, zk_ref, zv_ref), (cr_scr, ck_scr, cv_scr), (mur_ref, muk_ref, muv_ref),
                     lw_ref[...], a_ref[...], kk_ref[...], ka_ref[...], rk_ref[...], n_valid, hd)
    y, s_new = _rwkv_solve(ops, s_scr[...], par_pair(lnw_ref), par_pair(lnb_ref), hd, c)
    s_scr[...] = s_new
    out = _merge_heads(y, sb) * _silu(zg_ref[...])
    o_ref[...] = out.reshape(o_ref.shape).astype(o_ref.dtype)

    @pl.when(j == pl.num_programs(2) - 1)
    def _():
        s4 = s_new.reshape(sb, npair, pw, pw)
        both = jnp.stack([s4[:, :, :hd, :hd], s4[:, :, hd:, hd:]], axis=2)
        sout_ref[...] = both.reshape(sout_ref.shape)


def _rwkv(z3, lw, a, zprev, s0_all, l_in, mix, s_acc, l, depth, muz, kkp, kap, rkp, lnw, lnb,
          *, wa, wb, hbw, plan):
    b, t, _ = z3.shape
    sb, c = plan.sb_rwkv, plan.chunk
    hd = s0_all.shape[-1]
    nb = wb // hbw
    nh = hbw // hd
    off = 2 * wa // hbw
    zspec = lambda o: pl.BlockSpec((sb, c, hbw), lambda i, h, j, o=o: (i, j, o + h))
    pspec = lambda o: pl.BlockSpec((sb, 1, hbw), lambda i, h, j, o=o: (i, 0, o + h))
    mspec = lambda o: pl.BlockSpec((1, hbw), lambda i, h, j, o=o: (0, o + h))
    act = pl.BlockSpec((sb, c, hbw), lambda i, h, j: (i, j, h))
    st = lambda ll: pl.BlockSpec((None, sb, nh, hd, hd), lambda i, h, j, ll=ll: (ll, i, h, 0, 0))
    aliased = [mix, s_acc]
    n_in = _RWKV_INPUTS
    aliases = {n_in: 0, n_in + 1: 1}
    out = pl.pallas_call(
        functools.partial(_rwkv_body, n_valid=min(plan.n_valid, c), hd=hd, n_alias=len(aliased)),
        grid=(b // sb, nb, t // c),
        in_specs=[zspec(off), zspec(off + nb), zspec(off + 2 * nb), zspec(off + 3 * nb), act, act,
                  pspec(0), pspec(nb), pspec(2 * nb), st(l_in),
                  mspec(0), mspec(nb), mspec(2 * nb), mspec(0), mspec(0), mspec(0), mspec(0), mspec(0)]
        + [_ANY] * len(aliased),
        out_specs=[_act_spec(plan, t, sb, c, hbw, lambda h: wa // hbw + h, 3), st(l)],
        out_shape=[jax.ShapeDtypeStruct(mix.shape, mix.dtype),
                   jax.ShapeDtypeStruct((depth,) + s0_all.shape[1:], F32)],
        input_output_aliases=aliases,
        scratch_shapes=[pltpu.VMEM((sb * nh // 2, 2 * hd, 2 * hd), F32), pltpu.VMEM((sb, 1, hbw), F32),
                        pltpu.VMEM((sb, 1, hbw), F32), pltpu.VMEM((sb, 1, hbw), F32)],
        compiler_params=_cparams(3),
        name="rwkv7",
    )(z3, z3, z3, z3, lw, a, zprev, zprev, zprev, s0_all, muz, muz, muz, kkp, kap, rkp, lnw, lnb,
      *aliased)
    return out


def _rwkv_steps_body(*refs, n_valid, tpad, hd, n_alias):
    (zr_ref, zk_ref, zv_ref, zg_ref, lw_ref, a_ref, pr_ref, pk_ref, pv_ref, s0_ref,
     mur_ref, muk_ref, muv_ref, kk_ref, ka_ref, rk_ref, lnw_ref, lnb_ref) = refs[:_RWKV_INPUTS]
    o_ref, sout_ref, o_scr, vt_scr, y_scr = refs[_RWKV_INPUTS + n_alias:]
    bsz, cols = pr_ref.shape
    nh = cols // hd
    sout_ref[...] = s0_ref[...]
    o_scr[...] = jnp.zeros_like(o_scr)
    per_col = lambda ref: jnp.broadcast_to(ref[...], (bsz, cols)).T
    lnw_c, lnb_c, rk_c = per_col(lnw_ref), per_col(lnb_ref), per_col(rk_ref)
    step_rows = lambda ref, t: ref[pl.ds(t, bsz, stride=tpad), :]
    prev = (pr_ref[...], pk_ref[...], pv_ref[...])
    for t in range(n_valid):
        z = (step_rows(zr_ref, t), step_rows(zk_ref, t), step_rows(zv_ref, t))
        r, k, v = (zz + (pp - zz) * mu[...] for zz, pp, mu in zip(z, prev, (mur_ref, muk_ref, muv_ref)))
        prev = z
        a = step_rows(a_ref, t)
        w_t = jnp.exp(step_rows(lw_ref, t)).T
        a_t = a.T
        r_t = r.T
        v_t = v.T
        kk_t = (k * kk_ref[...]).T
        km_t = (k * (1.0 + (a - 1.0) * ka_ref[...])).T
        vt_scr[...] = v_t
        for hh in range(nh):
            sl = slice(hh * hd, (hh + 1) * hd)
            kkh = kk_t[sl]
            kkh = kkh / jnp.maximum(jnp.sqrt(jnp.sum(kkh * kkh, axis=0, keepdims=True)), KK_EPS)
            bh, wh, kmh, rh = kkh * a_t[sl], w_t[sl], km_t[sl], r_t[sl]

            def value_row(vi, carry, hh=hh, kkh=kkh, bh=bh, wh=wh, kmh=kmh, rh=rh):
                s = sout_ref[hh, vi]
                sa = jnp.sum(s * kkh, axis=0, keepdims=True)
                s = s * wh - sa * bh + vt_scr[pl.ds(hh * hd + vi, 1), :] * kmh
                sout_ref[hh, vi] = s
                y_scr[pl.ds(hh * hd + vi, 1), :] = jnp.sum(s * rh, axis=0, keepdims=True)
                return carry

            lax.fori_loop(0, hd, value_row, 0, unroll=8)
        y = y_scr[...]
        outs = []
        for hh in range(nh):
            sl = slice(hh * hd, (hh + 1) * hd)
            yh = y[sl]
            mean = jnp.mean(yh, axis=0, keepdims=True)
            var = jnp.mean(jnp.square(yh - mean), axis=0, keepdims=True)
            yn = (yh - mean) * lax.rsqrt(var + GN_EPS) * lnw_c[sl] + lnb_c[sl]
            bonus = jnp.sum(r_t[sl] * km_t[sl] * rk_c[sl], axis=0, keepdims=True) * v_t[sl]
            outs.append(yn + bonus)
        out = jnp.concatenate(outs, axis=0).T * _silu(step_rows(zg_ref, t))
        o_scr[pl.ds(t, bsz, stride=tpad), :] = out
    o_ref[...] = o_scr[...].astype(o_ref.dtype)


def _rwkv_steps(z2, lw2, a2, zprev, s0_all, l_in, mix, s_acc, l, depth, muz, kkp, kap, rkp, lnw, lnb,
                *, wa, wb, plan, tpad):
    bsz = zprev.shape[0]
    rows = z2.shape[0]
    hd = s0_all.shape[-2]
    cols = LANES
    nb = wb // cols
    nh = cols // hd
    off = 2 * wa // cols
    zspec = lambda o: pl.BlockSpec((rows, cols), lambda h, o=o: (0, o + h))
    pspec = lambda o: pl.BlockSpec((bsz, cols), lambda h, o=o: (0, o + h))
    mspec = lambda o: pl.BlockSpec((1, cols), lambda h, o=o: (0, o + h))
    st = lambda ll: pl.BlockSpec((None, nh, hd, hd, bsz), lambda h, ll=ll: (ll, h, 0, 0, 0))
    aliased = [mix, s_acc]
    n_in = _RWKV_INPUTS
    aliases = {n_in: 0, n_in + 1: 1}
    return pl.pallas_call(
        functools.partial(_rwkv_steps_body, n_valid=plan.n_valid, tpad=tpad, hd=hd, n_alias=len(aliased)),
        grid=(nb,),
        in_specs=[zspec(off), zspec(off + nb), zspec(off + 2 * nb), zspec(off + 3 * nb),
                  zspec(0), zspec(0), pspec(0), pspec(nb), pspec(2 * nb), st(l_in),
                  mspec(0), mspec(nb), mspec(2 * nb), mspec(0), mspec(0), mspec(0), mspec(0), mspec(0)]
        + [_ANY] * len(aliased),
        out_specs=[pl.BlockSpec((rows, cols), lambda h: (0, wa // cols + h)), st(l)],
        out_shape=[jax.ShapeDtypeStruct(mix.shape, mix.dtype),
                   jax.ShapeDtypeStruct((depth,) + s0_all.shape[1:], F32)],
        input_output_aliases=aliases,
        scratch_shapes=[pltpu.VMEM((rows, cols), F32), pltpu.VMEM((cols, bsz), F32),
                        pltpu.VMEM((cols, bsz), F32)],
        compiler_params=_cparams(1),
        name="rwkv7_steps",
    )(z2, z2, z2, z2, lw2, a2, zprev, zprev, zprev, s0_all, muz, muz, muz, kkp, kap, rkp, lnw, lnb,
      *aliased)


def _hgrn_head(q, f, v, g, s0, lb, ng, n_valid, sub, stage):
    b2_ref, kf_ref, v_ref = stage
    bcast_row = lambda ref, i: ref[pl.ds(i, 1), :]
    c, dk = q.shape
    sig = jax.nn.sigmoid(f)
    logg = jnp.log(jnp.maximum(lb + (1.0 - lb) * sig, TINY))
    kf = (1.0 - lb) * jax.nn.sigmoid(-f)
    if n_valid < c:
        valid = lax.broadcasted_iota(jnp.int32, (c, dk), 0) < n_valid
        logg = jnp.where(valid, logg, 0.0)
        kf = jnp.where(valid, kf, 0.0)
    bc = _cumsum_time(logg, 0)
    b2 = bc * LOG2E
    b2_ref[...] = b2
    kf_ref[...] = kf
    v_ref[...] = v
    o_inter = _bdot(q * jnp.exp(bc), s0)
    pairs, w_rows = [], []
    for lo in range(0, c, sub):
        for r0 in range(lo, lo + sub, SUBLANES):
            qg, bg = q[r0:r0 + SUBLANES], b2[r0:r0 + SUBLANES]
            for src in range(lo, min(r0 + SUBLANES, n_valid)):
                pairs.append((r0, src))
                w_rows.append(qg * jnp.exp2(bg - bcast_row(b2_ref, src)) * bcast_row(kf_ref, src))
    att_rows = _bdot(jnp.concatenate(w_rows, axis=0), jnp.ones((dk, v.shape[1]), BF16))
    t_row = lax.broadcasted_iota(jnp.int32, (SUBLANES, 1), 0)
    groups = {}
    for u, (r0, src) in enumerate(pairs):
        col = att_rows[u * SUBLANES:(u + 1) * SUBLANES]
        if src > r0:
            col = jnp.where(t_row >= src - r0, col, 0.0)
        term = col * bcast_row(v_ref, src)
        groups[r0] = groups[r0] + term if r0 in groups else term
    blocks = []
    for lo in range(0, c, sub):
        ob = o_inter[lo:lo + sub]
        if lo > 0:
            bs = bc[lo - 1:lo]
            att = _bdot(q[lo:lo + sub] * jnp.exp(bc[lo:lo + sub] - bs), kf[:lo] * jnp.exp(bs - bc[:lo]), NT)
            ob = ob + _bdot(att, v[:lo])
        none = jnp.zeros((SUBLANES, v.shape[1]), F32)
        blocks.append(ob + jnp.concatenate([groups.get(r0, none) for r0 in range(lo, lo + sub, SUBLANES)],
                                           axis=0))
    o = jnp.concatenate(blocks, axis=0) if len(blocks) > 1 else blocks[0]
    b_last = bc[c - 1:c]
    e_col = jnp.broadcast_to(jnp.exp(b_last), (v.shape[1], dk)).T
    s_new = e_col * s0 + _bdot(kf * jnp.exp(b_last - bc), v, TN)
    on = o * lax.rsqrt(jnp.mean(o * o, axis=-1, keepdims=True) + RMS_EPS) * ng
    return on * _silu(g), s_new


_HGRN_INPUTS = 7


def _hgrn_body(*refs, n_valid, dk, sub, n_alias):
    q_ref, f_ref, i_ref, g_ref, s0_ref, lb_ref, ng_ref = refs[:_HGRN_INPUTS]
    o_ref, sout_ref, s_scr, o_scr, b2_scr, kf_scr, v_scr = refs[_HGRN_INPUTS + n_alias:]
    j = pl.program_id(2)

    @pl.when(j == 0)
    def _():
        s_scr[...] = s0_ref[...]

    sb, c, hcw = q_ref.shape
    nh = hcw // dk
    lb = lb_ref[...]
    ng = ng_ref[...]

    def one(s, carry, slot0=0):
        q, f, v, g = q_ref[s], f_ref[s], i_ref[s], g_ref[s]
        outs = []
        for hh in range(nh):
            sl = slice(hh * dk, (hh + 1) * dk)
            slot = slot0 + hh
            o, s_new = _hgrn_head(q[:, sl], f[:, sl], v[:, sl], g[:, sl], s_scr[s, hh],
                                  lb[:, sl], ng[:, sl], n_valid, sub,
                                  (b2_scr.at[slot], kf_scr.at[slot], v_scr.at[slot]))
            s_scr[s, hh] = s_new
            outs.append(o)
        o_scr[s] = jnp.concatenate(outs, axis=1) if nh > 1 else outs[0]
        return carry

    unroll = math.gcd(sb, HGRN_UNROLL_SEQS)

    def group(gi, carry):
        for u in range(unroll):
            one(gi * unroll + u, carry, u * nh)
        return carry

    if sb == unroll:
        group(0, 0)
    else:
        lax.fori_loop(0, sb // unroll, group, 0)
    o_ref[...] = o_scr[...].reshape(o_ref.shape).astype(o_ref.dtype)

    @pl.when(j == pl.num_programs(2) - 1)
    def _():
        sout_ref[...] = s_scr[...]


def _hgrn(z3, s0_all, l_in, mix, s_acc, l, depth, lb, ng, *, col0, out_col0, wc, hcw, plan):
    b, t, _ = z3.shape
    sb, c = plan.sb_hgrn, plan.chunk
    dk, dv = s0_all.shape[-2], s0_all.shape[-1]
    nb = wc // hcw
    nh = hcw // dk
    off = col0 // hcw
    zspec = lambda o: pl.BlockSpec((sb, c, hcw), lambda i, h, j, o=o: (i, j, o + h))
    st = lambda ll: pl.BlockSpec((None, sb, nh, dk, dv), lambda i, h, j, ll=ll: (ll, i, h, 0, 0))
    par = pl.BlockSpec((1, hcw), lambda i, h, j: (0, h))
    aliased = [mix, s_acc]
    n_in = _HGRN_INPUTS
    aliases = {n_in: 0, n_in + 1: 1}
    return pl.pallas_call(
        functools.partial(_hgrn_body, n_valid=min(plan.n_valid, c), dk=dk, sub=min(2 * SUBLANES, c),
                          n_alias=len(aliased)),
        grid=(b // sb, nb, t // c),
        in_specs=[zspec(off), zspec(off + nb), zspec(off + 2 * nb), zspec(off + 3 * nb), st(l_in),
                  par, par] + [_ANY] * len(aliased),
        out_specs=[_act_spec(plan, t, sb, c, hcw, lambda h: out_col0 // hcw + h, 3), st(l)],
        out_shape=[jax.ShapeDtypeStruct(mix.shape, mix.dtype),
                   jax.ShapeDtypeStruct((depth,) + s0_all.shape[1:], F32)],
        input_output_aliases=aliases,
        scratch_shapes=[pltpu.VMEM((sb, nh, dk, dv), F32), pltpu.VMEM((sb, c, hcw), F32)]
        + [pltpu.VMEM((nh * math.gcd(sb, HGRN_UNROLL_SEQS), c, dk), F32)] * 3,
        compiler_params=_cparams(3),
        name="hgrn2",
    )(z3, z3, z3, z3, s0_all, lb, ng, *aliased)


def _post1_body(h_ref, m_ref, g_ref, h1_ref, hn_ref):
    m = m_ref[...].astype(F32)
    h1 = h_ref[...] + m * lax.rsqrt(jnp.mean(m * m, axis=-1, keepdims=True) + RMS_EPS) * g_ref[...]
    h1_ref[...] = h1
    hn_ref[...] = (h1 * lax.rsqrt(jnp.mean(h1 * h1, axis=-1, keepdims=True) + RMS_EPS)).astype(hn_ref.dtype)


def _post2_body(h_ref, p_ref, proj_ref, gate_ref, g_ref, o_ref):
    u = jnp.dot(p_ref[...], proj_ref[...], preferred_element_type=F32)
    x = u * jax.nn.sigmoid(gate_ref[...].astype(F32))
    o_ref[...] = h_ref[...] + x * lax.rsqrt(jnp.mean(x * x, axis=-1, keepdims=True) + RMS_EPS) * g_ref[...]


def _post_ple(h1, p2, proj_all, l, gate, gain, tr=256):
    n, d = h1.shape
    kp = p2.shape[1]
    tr = min(tr, n)
    row = pl.BlockSpec((tr, d), lambda i: (i, 0))
    return pl.pallas_call(
        _post2_body,
        grid=(n // tr,),
        in_specs=[row, pl.BlockSpec((tr, kp), lambda i: (i, 0)),
                  pl.BlockSpec((None, kp, d), lambda i: (l, 0, 0)), row,
                  pl.BlockSpec((1, d), lambda i: (0, 0))],
        out_specs=row,
        out_shape=jax.ShapeDtypeStruct((n, d), F32),
        compiler_params=_cparams(1),
        name="post_ple",
    )(h1, p2, proj_all, gate, gain)


def _rowwise(body, name, arrays, gain, out_dtypes, tr=256):
    n, d = arrays[0].shape
    tr = min(tr, n)
    row = pl.BlockSpec((tr, d), lambda i: (i, 0))
    return pl.pallas_call(
        body,
        grid=(n // tr,),
        in_specs=[row] * len(arrays) + [pl.BlockSpec((1, d), lambda i: (0, 0))],
        out_specs=[row] * len(out_dtypes),
        out_shape=[jax.ShapeDtypeStruct((n, d), dt) for dt in out_dtypes],
        compiler_params=_cparams(1),
        name=name,
    )(*arrays, gain)


def _block_diag_tiles(w):
    nb, bs, _ = w.shape
    per = LANES // bs
    w = w.reshape(nb // per, per, bs, bs)
    eye = jnp.eye(per, dtype=w.dtype)
    return jnp.einsum('tpab,pq->tpaqb', w, eye).reshape(nb // per, LANES, LANES)


def _layer_stack(x, p, st, W, mm, plan, zero_state):
    st_a_h, st_a_conv, st_b_s, st_b_x, st_b_z, st_c_s = st
    b, t, d = x.shape
    depth = W['g_pre'].shape[0]
    wa = st_a_h.shape[-1]
    hd = st_b_s.shape[-1]
    wb = st_b_s.shape[-3] * hd
    dk = st_c_s.shape[-2]
    wc = st_c_s.shape[-3] * dk
    mix_w = wa + wb + wc
    n = b * t
    nv = plan.n_valid
    hbw = math.gcd(math.gcd(2 * wa, wb), math.gcd(wa, 512))
    hcw = math.gcd(math.gcd(2 * wa + 4 * wb, wc), math.gcd(wa + wb, 512))

    lb_soft = jax.nn.softmax(W['hgrn_lb'].astype(F32), axis=0)
    lb_all = jnp.cumsum(lb_soft, axis=0) - lb_soft[0]
    p_act = p.astype(BF16)

    if plan.rwkv_steps:
        st_b_s = jnp.transpose(st_b_s, (0, 2, 3, 4, 1))
    rwkv_params = lambda l: (
        W['rwkv_mu_z'][l][None], W['rwkv_kk'][l][None], W['rwkv_ka'][l][None],
        W['rwkv_rk'][l].reshape(1, wb), W['rwkv_lnx_w'][l][None], W['rwkv_lnx_b'][l][None])

    h = x
    small = [[] for _ in range(4)]
    sb_acc = jnp.zeros((depth,) + st_b_s.shape[1:], F32)
    sc_acc = jnp.zeros((depth,) + st_c_s.shape[1:], F32)
    for l in range(depth):
        ls = 0 if zero_state else l
        xn, lw, a, xlast = _prenorm(
            h, W['g_pre'][l].reshape(1, 1, d), st_b_x[ls][:, None, :], W['rwkv_lora_in'][l],
            W['rwkv_w2'][l], W['rwkv_a2'][l], W['rwkv_w0'][l][None], W['rwkv_a0'][l][None], plan)
        z3 = mm('w_in', xn.reshape(n, d), l, F32).reshape(b, t, -1)
        cb8 = jnp.pad(st_a_conv[ls], ((0, 0), (SUBLANES - (CONV_W - 1), 0), (0, 0)))
        mix, ha, tail = _rglru(
            z3, st_a_h[ls][:, None, :], cb8, W['conv_a_w'][l][None], W['conv_a_b'][l].reshape(1, 1, wa),
            W['lru_wr_bd'][l], W['lru_br'][l][None], W['lru_wi_bd'][l], W['lru_bi'][l][None],
            W['lru_lambda'][l][None], wa=wa, mix_w=mix_w, plan=plan)
        if plan.rwkv_steps:
            mix, sb_acc = _rwkv_steps(
                z3.reshape(n, -1), lw.reshape(n, wb), a.reshape(n, wb), st_b_z[ls], st_b_s, ls,
                mix, sb_acc, l, depth, *rwkv_params(l), wa=wa, wb=wb, plan=plan, tpad=t)
        else:
            mix, sb_acc = _rwkv(
                z3, lw, a, st_b_z[ls][:, None, :], st_b_s, ls, mix, sb_acc, l, depth,
                *rwkv_params(l), wa=wa, wb=wb, hbw=hbw, plan=plan)
        mix, sc_acc = _hgrn(
            z3, st_c_s, ls, mix, sc_acc, l, depth, lb_all[l][None], W['hgrn_norm_g'][l][None],
            col0=2 * wa + 4 * wb, out_col0=wa + wb, wc=wc, hcw=hcw, plan=plan)
        mo = mm('w_out', mix.reshape(n, mix_w), l, BF16)
        h1, hn = _rowwise(_post1_body, "post_mix", [h.reshape(n, d), mo], W['g_post'][l][None],
                          [F32, BF16])
        gp = mm('ple_gate', hn, l, BF16)
        h = _post_ple(h1, p_act[l].reshape(n, -1), W['ple_proj'], l, gp, W['g_ple'][l][None]).reshape(b, t, d)
        small[0].append(ha[:, 0])
        small[1].append(tail[:, SUBLANES - (CONV_W - 1):])
        small[2].append(xlast[:, 0])
        small[3].append(z3[:, nv - 1, 2 * wa:2 * wa + 3 * wb])
    na_h, na_c, nb_x, nb_z = (jnp.stack(o) for o in small)
    if plan.rwkv_steps:
        sb_acc = jnp.transpose(sb_acc, (0, 4, 1, 2, 3))
    return h, (na_h, na_c, sb_acc, nb_x, nb_z, sc_acc)


def kernel(x_prompt, x_sample, p_prompt, p_sample, state_a_h, state_a_conv, state_b_S,
           state_b_xprev, state_b_zprev, state_c_S, g_pre, g_post, w_in, w_out,
           conv_a_w, conv_a_b, lru_wr, lru_br, lru_wi, lru_bi, lru_lambda,
           rwkv_mu_z, rwkv_mu_w, rwkv_mu_a, rwkv_w0, rwkv_w1, rwkv_w2, rwkv_a0, rwkv_a1,
           rwkv_a2, rwkv_kk, rwkv_ka, rwkv_rk, rwkv_lnx_w, rwkv_lnx_b, hgrn_lb, hgrn_norm_g,
           ple_proj, ple_gate, g_ple):
    depth = w_in.shape[0]
    W = dict(g_pre=g_pre, g_post=g_post, conv_a_w=conv_a_w, conv_a_b=conv_a_b,
             lru_br=lru_br, lru_bi=lru_bi, lru_lambda=lru_lambda, rwkv_mu_z=rwkv_mu_z,
             rwkv_mu_w=rwkv_mu_w, rwkv_mu_a=rwkv_mu_a, rwkv_w0=rwkv_w0, rwkv_a0=rwkv_a0,
             rwkv_kk=rwkv_kk, rwkv_ka=rwkv_ka, rwkv_rk=rwkv_rk, rwkv_lnx_w=rwkv_lnx_w,
             rwkv_lnx_b=rwkv_lnx_b, hgrn_lb=hgrn_lb, hgrn_norm_g=hgrn_norm_g, g_ple=g_ple)
    for name, w in (('rwkv_w2', rwkv_w2), ('rwkv_a2', rwkv_a2), ('ple_proj', ple_proj)):
        W[name] = w.astype(BF16)
    mw, ma = rwkv_mu_w[:, :, None], rwkv_mu_a[:, :, None]
    W['rwkv_lora_in'] = jnp.concatenate(
        [(1.0 - mw) * rwkv_w1, (1.0 - ma) * rwkv_a1, mw * rwkv_w1, ma * rwkv_a1], axis=-1).astype(BF16)
    W['lru_wr_bd'] = jnp.stack([_block_diag_tiles(lru_wr[l]) for l in range(depth)]).astype(BF16)
    W['lru_wi_bd'] = jnp.stack([_block_diag_tiles(lru_wi[l]) for l in range(depth)]).astype(BF16)

    bp, tp, d = x_prompt.shape
    bs, ts, _ = x_sample.shape
    dt = x_prompt.dtype
    st_s = (state_a_h, state_a_conv, state_b_S, state_b_xprev, state_b_zprev, state_c_S)
    st_p = tuple(jnp.zeros((1, bp) + s.shape[2:], dt) for s in st_s)

    big = dict(w_in=w_in, w_out=w_out, ple_gate=ple_gate)
    wq = {name: [None] * depth for name in big}

    def mm_sample(name, x, l, out_dtype):
        if x.shape[0] <= MM_TILE:
            y, wq[name][l] = _matmul_wcast(x, big[name], l, out_dtype)
            return y
        wq[name][l] = big[name][l].astype(BF16)
        return _matmul(x, wq[name][l], out_dtype)

    def mm_prompt(name, x, l, out_dtype):
        return _matmul(x, wq[name][l], out_dtype)

    tpad = -(-ts // SUBLANES) * SUBLANES
    pad_t = lambda v, ax: jnp.pad(v, [(0, tpad - ts) if i == ax else (0, 0) for i in range(v.ndim)])
    y_s, out_s = _layer_stack(pad_t(x_sample, 1), pad_t(p_sample, 2), st_s, W, mm_sample,
                              _make_plan(bs, tpad, ts), False)
    y_p, out_p = _layer_stack(x_prompt, p_prompt, st_p, W, mm_prompt, _make_plan(bp, tp, tp), True)
    return (y_p, y_s[:, :ts]) + out_p + out_s
```

```python
import functools
import math
from typing import NamedTuple

import jax
import jax.numpy as jnp
from jax import lax
from jax.experimental import pallas as pl
from jax.experimental.pallas import tpu as pltpu

F32 = jnp.float32
BF16 = jnp.bfloat16

RMS_EPS = 1e-6
GN_EPS = 64e-5
LRU_C = 8.0
TINY = 1e-30
KK_EPS = 1e-12
CONV_W = 4
LOG2E = 1.4426950408889634
NORM_GROUP_ROWS = 128
HGRN_UNROLL_SEQS = 4

LANES = 128
SUBLANES = 8
BF16_ROWS = 16
VMEM_LIMIT = 52 * 1024 * 1024

NT = (((1,), (1,)), ((), ()))
TN = (((0,), (0,)), ((), ()))
BNT = (((2,), (2,)), ((0,), (0,)))
BNN = (((2,), (1,)), ((0,), (0,)))
BTN = (((1,), (1,)), ((0,), (0,)))


class Plan(NamedTuple):
    n_valid: int
    tt: int
    sb_rows: int
    tt_norm: int
    sb_norm: int
    chunk: int
    sb_rwkv: int
    sb_hgrn: int
    flat_acts: bool
    rwkv_steps: bool


def _make_plan(b, t, n_valid):
    tt = min(t, 128)
    chunk = min(t, 64)
    rows = 128
    sb = max(1, min(b, rows // tt))
    tt_norm = min(t, 2 * rows)
    sb_norm = max(1, min(b, 2 * rows // tt_norm))
    flat = chunk == t
    assert flat or (chunk % BF16_ROWS == 0 and tt % BF16_ROWS == 0)
    return Plan(n_valid=n_valid, tt=tt, sb_rows=sb, tt_norm=tt_norm, sb_norm=sb_norm, chunk=chunk,
                sb_rwkv=math.gcd(b, max(4, rows // chunk)),
                sb_hgrn=sb if chunk < 64 else math.gcd(b, HGRN_UNROLL_SEQS),
                flat_acts=flat, rwkv_steps=flat and b % LANES == 0 and t <= SUBLANES)


def _act_shape(plan, b, t, width):
    return jax.ShapeDtypeStruct((b * t, width) if plan.flat_acts else (b, t, width), BF16)


def _act_spec(plan, t, sb, rows, width, col, grid_rank):
    if grid_rank == 2:
        if plan.flat_acts:
            return pl.BlockSpec((sb * rows, width), lambda i, j: (i * (t // rows) + j, col(0)))
        return pl.BlockSpec((sb, rows, width), lambda i, j: (i, j, col(0)))
    if plan.flat_acts:
        return pl.BlockSpec((sb * rows, width), lambda i, h, j: (i * (t // rows) + j, col(h)))
    return pl.BlockSpec((sb, rows, width), lambda i, h, j: (i, j, col(h)))


def _cparams(n_axes):
    return pltpu.CompilerParams(dimension_semantics=("arbitrary",) * n_axes,
                                vmem_limit_bytes=VMEM_LIMIT)


def _softplus(x):
    return jnp.maximum(x, 0.0) + jnp.log1p(jnp.exp(-jnp.abs(x)))


def _silu(x):
    return x * jax.nn.sigmoid(x)


def _bdot(a, b, dims=None):
    a = a.astype(BF16)
    b = b.astype(BF16)
    if dims is None:
        return jnp.dot(a, b, preferred_element_type=F32)
    return lax.dot_general(a, b, dims, preferred_element_type=F32)


def _cumsum_time(x, axis):
    n = x.shape[axis]
    idx = lax.broadcasted_iota(jnp.int32, x.shape, axis)
    d = 1
    while d < n:
        x = x + jnp.where(idx >= d, pltpu.roll(x, d, axis), 0.0)
        d *= 2
    return x


_ANY = pl.BlockSpec(memory_space=pl.ANY)


MM_TILE = 1024


def _mm_body(x_ref, w_ref, o_ref):
    o_ref[...] = jnp.dot(x_ref[...], w_ref[...], preferred_element_type=F32).astype(o_ref.dtype)


W_RING = 3


def _mm_wcast_body(x_ref, w_hbm, o_ref, wb_ref, wbuf, sem, *, l, tn, n_steps):
    j = pl.program_id(0)

    def tile_copy(step):
        slot = step % W_RING
        cols = pl.ds(pl.multiple_of(step * tn, tn), tn)
        return pltpu.make_async_copy(w_hbm.at[l, :, cols], wbuf.at[slot], sem.at[slot])

    @pl.when(j == 0)
    def _():
        for step in range(min(W_RING - 1, n_steps)):
            tile_copy(step).start()

    @pl.when(j + (W_RING - 1) < n_steps)
    def _():
        tile_copy(j + (W_RING - 1)).start()

    tile_copy(j).wait()
    wb = wbuf[j % W_RING].astype(BF16)
    wb_ref[...] = wb
    o_ref[...] = jnp.dot(x_ref[...], wb, preferred_element_type=F32).astype(o_ref.dtype)


def _matmul(x, w, out_dtype):
    m, k = x.shape
    n = w.shape[1]
    tm = math.gcd(m, MM_TILE)
    tn = math.gcd(n, MM_TILE)
    return pl.pallas_call(
        _mm_body,
        grid=(m // tm, n // tn),
        in_specs=[pl.BlockSpec((tm, k), lambda i, j: (i, 0)),
                  pl.BlockSpec((k, tn), lambda i, j: (0, j))],
        out_specs=pl.BlockSpec((tm, tn), lambda i, j: (i, j)),
        out_shape=jax.ShapeDtypeStruct((m, n), out_dtype),
        compiler_params=_cparams(2),
        name="matmul",
    )(x, w)


def _matmul_wcast(x, w_all, l, out_dtype):
    m, k = x.shape
    n = w_all.shape[2]
    tn = math.gcd(n, MM_TILE // 2)
    n_steps = n // tn
    return pl.pallas_call(
        functools.partial(_mm_wcast_body, l=l, tn=tn, n_steps=n_steps),
        grid=(n_steps,),
        in_specs=[pl.BlockSpec((m, k), lambda j: (0, 0), pipeline_mode=pl.Buffered(1)),
                  pl.BlockSpec(memory_space=pl.ANY)],
        out_specs=[pl.BlockSpec((m, tn), lambda j: (0, j)),
                   pl.BlockSpec((k, tn), lambda j: (0, j))],
        out_shape=[jax.ShapeDtypeStruct((m, n), out_dtype), jax.ShapeDtypeStruct((k, n), BF16)],
        scratch_shapes=[pltpu.VMEM((W_RING, k, tn), F32), pltpu.SemaphoreType.DMA((W_RING,))],
        compiler_params=_cparams(1),
        name="matmul_wcast",
    )(x, w_all)


def _prenorm_body(h_ref, g_ref, xprev_ref, wf_ref, w2_ref, a2_ref, w0_ref, a0_ref,
                  xn_ref, lw_ref, a_ref, xlast_ref, carry_ref, *, last_row):
    j = pl.program_id(1)
    sb, tt, d = h_ref.shape
    r2 = wf_ref.shape[1] // 2
    rank = r2 // 2

    @pl.when(j == 0)
    def _():
        xp = jnp.broadcast_to(xprev_ref[...], (sb, SUBLANES, d)).reshape(sb * SUBLANES, d)
        carry_ref[...] = _bdot(xp, wf_ref[:, r2:]).reshape(sb, SUBLANES, r2)[:, 0:1, :]

    gt = min(tt, NORM_GROUP_ROWS)
    gs = max(1, min(sb, NORM_GROUP_ROWS // gt))
    for s0 in range(0, sb, gs):
        for t0 in range(0, tt, gt):
            h = h_ref[s0:s0 + gs, t0:t0 + gt, :]
            xn = h * lax.rsqrt(jnp.mean(h * h, axis=-1, keepdims=True) + RMS_EPS) * g_ref[...]
            xn_b = xn.reshape(gs * gt, d).astype(BF16)
            if len(xn_ref.shape) == 2:
                xn_ref[s0 * tt + t0:s0 * tt + t0 + gs * gt, :] = xn_b
            else:
                xn_ref[s0:s0 + gs, t0:t0 + gt, :] = xn_b.reshape(gs, gt, d)
            if t0 <= last_row < t0 + gt:
                @pl.when(j == pl.num_programs(1) - 1)
                def _(xn=xn, s0=s0, t0=t0):
                    xlast_ref[s0:s0 + gs] = xn[:, last_row - t0:last_row - t0 + 1, :]

            prod = jnp.dot(xn_b, wf_ref[...], preferred_element_type=F32)
            shifted = prod[:, r2:].reshape(gs, gt, r2)
            t_idx = lax.broadcasted_iota(jnp.int32, (gs, gt, r2), 1)
            prev = jnp.where(t_idx == 0, carry_ref[s0:s0 + gs], pltpu.roll(shifted, 1, 1))
            carry_ref[s0:s0 + gs] = shifted[:, gt - 1:gt, :]
            pre = prod[:, :r2] + prev.reshape(gs * gt, r2)
            yw = w0_ref[...] + _bdot(jnp.tanh(pre[:, :rank]), w2_ref[...])
            wl = -_softplus(-yw) - 0.5
            lw_ref[s0:s0 + gs, t0:t0 + gt, :] = (-jnp.exp(wl)).reshape(gs, gt, -1)
            ya = a0_ref[...] + _bdot(pre[:, rank:], a2_ref[...])
            a_ref[s0:s0 + gs, t0:t0 + gt, :] = jax.nn.sigmoid(ya).reshape(gs, gt, -1)


def _prenorm(h3, g, xprev, wf, w2, a2, w0, a0, plan):
    b, t, d = h3.shape
    sb, tt = plan.sb_norm, plan.tt_norm
    wb = w2.shape[1]
    rank = w2.shape[0]
    row3 = lambda i, j: (i, j, 0)
    par3 = lambda i, j: (0, 0, 0)
    par2 = lambda i, j: (0, 0)
    seq3 = lambda i, j: (i, 0, 0)
    return pl.pallas_call(
        functools.partial(_prenorm_body, last_row=(plan.n_valid - 1) % tt),
        grid=(b // sb, t // tt),
        in_specs=[pl.BlockSpec((sb, tt, d), row3),
                  pl.BlockSpec((1, 1, d), par3),
                  pl.BlockSpec((sb, 1, d), seq3),
                  pl.BlockSpec((d, 4 * rank), par2),
                  pl.BlockSpec((rank, wb), par2),
                  pl.BlockSpec((rank, wb), par2),
                  pl.BlockSpec((1, wb), par2),
                  pl.BlockSpec((1, wb), par2)],
        out_specs=[_act_spec(plan, t, sb, tt, d, lambda h: 0, 2),
                   pl.BlockSpec((sb, tt, wb), row3),
                   pl.BlockSpec((sb, tt, wb), row3),
                   pl.BlockSpec((sb, 1, d), seq3)],
        out_shape=[_act_shape(plan, b, t, d),
                   jax.ShapeDtypeStruct((b, t, wb), F32),
                   jax.ShapeDtypeStruct((b, t, wb), F32),
                   jax.ShapeDtypeStruct((b, 1, d), F32)],
        scratch_shapes=[pltpu.VMEM((sb, 1, 2 * rank), F32)],
        compiler_params=_cparams(2),
        name="prenorm_lora",
    )(h3, g, xprev, wf, w2, a2, w0, a0)


def _rglru_body(x_ref, gt_ref, h0_ref, cb_ref, cw_ref, cbias_ref, wr_ref, br_ref, wi_ref, bi_ref,
                lam_ref, o_ref, hout_ref, tail_ref, hc_ref, tl_ref, *, nv_last):
    j = pl.program_id(1)

    @pl.when(j == 0)
    def _():
        hc_ref[...] = h0_ref[...]
        tl_ref[...] = cb_ref[...]

    x = x_ref[...]
    sb, tt, w = x.shape
    ext = jnp.concatenate([tl_ref[...], x], axis=1)
    tl_ref[...] = ext[:, tt:tt + SUBLANES, :]
    u = cbias_ref[...]
    for tap in range(CONV_W):
        sh = CONV_W - 1 - tap
        xs = x if sh == 0 else pltpu.roll(ext, sh, 1)[:, SUBLANES:, :]
        u = u + xs * cw_ref[:, tap:tap + 1, :]
    u2 = u.reshape(sb * tt, w)
    rp, ip = [], []
    for m in range(w // LANES):
        um = u2[:, m * LANES:(m + 1) * LANES]
        rp.append(_bdot(um, wr_ref[m]))
        ip.append(_bdot(um, wi_ref[m]))
    r = jax.nn.sigmoid(jnp.concatenate(rp, axis=1) + br_ref[...])
    i = jax.nn.sigmoid(jnp.concatenate(ip, axis=1) + bi_ref[...])
    log_a = -LRU_C * r * _softplus(-lam_ref[...])
    th = jnp.tanh(log_a)
    bcoef = jnp.sqrt(-2.0 * th / (1.0 - th)) * (i * u2)
    nt = tt // SUBLANES
    a_cum = jnp.exp(log_a).reshape(sb * nt, SUBLANES, w)
    b_cum = bcoef.reshape(sb * nt, SUBLANES, w)
    t_idx = lax.broadcasted_iota(jnp.int32, (sb * nt, SUBLANES, w), 1)
    d = 1
    while d < SUBLANES:
        keep = t_idx >= d
        a_prev = jnp.where(keep, pltpu.roll(a_cum, d, 1), 1.0)
        b_prev = jnp.where(keep, pltpu.roll(b_cum, d, 1), 0.0)
        b_cum = a_cum * b_prev + b_cum
        a_cum = a_cum * a_prev
        d *= 2
    a_end = a_cum.reshape(sb, nt, SUBLANES, w)[:, :, SUBLANES - 1:, :]
    b_end = b_cum.reshape(sb, nt, SUBLANES, w)[:, :, SUBLANES - 1:, :]
    h_in = [hc_ref[...]]
    for g in range(nt):
        h_in.append(a_end[:, g] * h_in[g] + b_end[:, g])
    hc_ref[...] = h_in[nt]
    h_tile = jnp.stack(h_in[:nt], axis=1).reshape(sb * nt, 1, w)
    hs = (a_cum * h_tile + b_cum).reshape(sb, tt, w)
    out = (hs * _silu(gt_ref[...])).reshape(sb * tt, w)
    rest = jnp.zeros((sb * tt, o_ref.shape[-1] - w), F32)
    o_ref[...] = jnp.concatenate([out, rest], axis=-1).reshape(o_ref.shape).astype(o_ref.dtype)

    @pl.when(j == pl.num_programs(1) - 1)
    def _():
        hout_ref[...] = hs[:, nv_last - 1:nv_last, :]
        if nv_last == tt:
            tail_ref[...] = ext[:, tt:tt + SUBLANES, :]
        else:
            tail_ref[...] = pltpu.roll(ext, tt + SUBLANES - nv_last, 1)[:, 0:SUBLANES, :]


def _rglru(z3, h0, cb8, cw, cbias, wr_bd, br, wi_bd, bi, lam, *, wa, mix_w, plan):
    b, t, _ = z3.shape
    sb, tt = plan.sb_rows, plan.tt
    nt = wa // LANES
    par3 = lambda i, j: (0, 0, 0)
    seq3 = lambda i, j: (i, 0, 0)
    return pl.pallas_call(
        functools.partial(_rglru_body, nv_last=(plan.n_valid - 1) % tt + 1),
        grid=(b // sb, t // tt),
        in_specs=[pl.BlockSpec((sb, tt, wa), lambda i, j: (i, j, 0)),
                  pl.BlockSpec((sb, tt, wa), lambda i, j: (i, j, 1)),
                  pl.BlockSpec((sb, 1, wa), seq3),
                  pl.BlockSpec((sb, SUBLANES, wa), seq3),
                  pl.BlockSpec((1, CONV_W, wa), par3),
                  pl.BlockSpec((1, 1, wa), par3),
                  pl.BlockSpec((nt, LANES, LANES), par3),
                  pl.BlockSpec((1, wa), lambda i, j: (0, 0)),
                  pl.BlockSpec((nt, LANES, LANES), par3),
                  pl.BlockSpec((1, wa), lambda i, j: (0, 0)),
                  pl.BlockSpec((1, wa), lambda i, j: (0, 0))],
        out_specs=[_act_spec(plan, t, sb, tt, mix_w, lambda h: 0, 2),
                   pl.BlockSpec((sb, 1, wa), seq3),
                   pl.BlockSpec((sb, SUBLANES, wa), seq3)],
        out_shape=[_act_shape(plan, b, t, mix_w),
                   jax.ShapeDtypeStruct((b, 1, wa), F32),
                   jax.ShapeDtypeStruct((b, SUBLANES, wa), F32)],
        scratch_shapes=[pltpu.VMEM((sb, 1, wa), F32), pltpu.VMEM((sb, SUBLANES, wa), F32)],
        compiler_params=_cparams(2),
        name="rglru",
    )(z3, z3, h0, cb8, cw, cbias, wr_bd, br, wi_bd, bi, lam)


def _split_heads(x, hd):
    sb, c, w = x.shape
    nh = w // hd
    st = jnp.stack([x[:, :, h * hd:(h + 1) * hd] for h in range(nh)], axis=1)
    return st.reshape(sb * nh, c, hd)


def _merge_heads(x, sb):
    n, c, hd = x.shape
    nh = n // sb
    x4 = x.reshape(sb, nh, c, hd)
    return jnp.concatenate([x4[:, h] for h in range(nh)], axis=-1)


def _pair_diag(x, half):
    lo = lax.broadcasted_iota(jnp.int32, x.shape, 2) < half
    return jnp.concatenate([jnp.where(lo, x, 0.0), jnp.where(lo, 0.0, x)], axis=1)


def _pair_sum(x, half):
    lo = lax.broadcasted_iota(jnp.int32, x.shape, 2) < half
    s_lo = jnp.sum(jnp.where(lo, x, 0.0), axis=-1, keepdims=True)
    s_hi = jnp.sum(jnp.where(lo, 0.0, x), axis=-1, keepdims=True)
    return jnp.where(lo, s_lo, s_hi)


_RWKV_INPUTS = 18


def _rwkv_prep(z_refs, carry_refs, mu_refs, lw, a, kk_p, ka_p, rk_p, n_valid, hd):
    sb, c, hbw = lw.shape
    pw = 2 * hd
    row = lax.broadcasted_iota(jnp.int32, (sb, c, hbw), 1)
    first = row == 0

    def mix(z_ref, carry_ref, mu_ref):
        z = z_ref[...]
        zm = z + (jnp.where(first, carry_ref[...], pltpu.roll(z, 1, 1)) - z) * mu_ref[...]
        carry_ref[...] = z[:, c - 1:c, :]
        return zm

    r, k, v = (mix(zr, cr, mu) for zr, cr, mu in zip(z_refs, carry_refs, mu_refs))
    kk_raw = k * kk_p
    kmod = k * (1.0 + (a - 1.0) * ka_p)
    if n_valid < c:
        valid = row < n_valid
        lw = jnp.where(valid, lw, 0.0)
        kmod = jnp.where(valid, kmod, 0.0)
        v = jnp.where(valid, v, 0.0)
        a = jnp.where(valid, a, 0.0)
    cl = _cumsum_time(lw, 1)

    pairs = lambda x: _split_heads(x, pw)
    rk_pair = jnp.concatenate([_split_heads(rk_p[None], pw)] * sb, axis=0)
    kkp, ap, kmp, vp, rp, clp, lwp = map(pairs, (kk_raw, a, kmod, v, r, cl, lw))
    kkp = kkp / jnp.maximum(jnp.sqrt(_pair_sum(kkp * kkp, hd)), KK_EPS)
    bp = kkp * ap
    cl_last = clp[:, c - 1:c, :]
    e_neg = jnp.exp(-clp)
    e_end = jnp.exp(cl_last - clp)
    return dict(
        lhs=jnp.concatenate([kkp * jnp.exp(clp - lwp), rp * jnp.exp(clp)], axis=1),
        bd_b=_pair_diag(bp * e_neg, hd), bd_k=_pair_diag(kmp * e_neg, hd), bd_v=_pair_diag(vp, hd),
        k_end=kmp * e_end, b_end=bp * e_end, vp=vp,
        bonus=_pair_sum(rp * kmp * rk_pair, hd) * vp,
        e_tot=jnp.exp(cl_last))


def _rwkv_solve(p, s0, lnw_pair, lnb_pair, hd, c):
    pw = 2 * hd
    ti = lax.broadcasted_iota(jnp.int32, (c, 2 * c), 0)
    si = lax.broadcasted_iota(jnp.int32, (c, 2 * c), 1)
    si = jnp.where(si >= c, si - c, si)
    strict = (ti > si)[None]
    lower = (ti >= si)[None]
    lhs = p["lhs"]
    ab = _bdot(lhs, p["bd_b"], BNT)
    ak = _bdot(lhs, p["bd_k"], BNT)
    su = _bdot(lhs, s0, BNT)
    x = su[:, :c] + _bdot(jnp.where(strict, ak[:, :c], 0.0), p["bd_v"], BNN)
    lp = jnp.where(strict, ab[:, :c], 0.0)
    x = x - _bdot(lp, _pair_diag(x, hd), BNN)
    pw2 = 2
    while pw2 < c:
        lp = _bdot(lp, _pair_diag(lp, c), BNN)
        x = x + _bdot(lp, _pair_diag(x, hd), BNN)
        pw2 *= 2
    y = (su[:, c:] + _bdot(jnp.where(lower, ak[:, c:], 0.0), p["bd_v"], BNN)
         - _bdot(jnp.where(lower, ab[:, c:], 0.0), _pair_diag(x, hd), BNN))
    full = s0 * p["e_tot"] + _bdot(p["vp"], p["k_end"], BTN) - _bdot(x, p["b_end"], BTN)
    same_head = ((lax.broadcasted_iota(jnp.int32, (pw, pw), 0) < hd)
                 == (lax.broadcasted_iota(jnp.int32, (pw, pw), 1) < hd))[None]
    s_new = jnp.where(same_head, full, 0.0)
    mean = _pair_sum(y, hd) * (1.0 / hd)
    var = _pair_sum(jnp.square(y - mean), hd) * (1.0 / hd)
    yn = (y - mean) * lax.rsqrt(var + GN_EPS) * lnw_pair + lnb_pair
    return yn + p["bonus"], s_new


def _rwkv_body(*refs, n_valid, hd, n_alias):
    (zr_ref, zk_ref, zv_ref, zg_ref, lw_ref, a_ref, pr_ref, pk_ref, pv_ref, s0_ref,
     mur_ref, muk_ref, muv_ref, kk_ref, ka_ref, rk_ref, lnw_ref, lnb_ref) = refs[:_RWKV_INPUTS]
    o_ref, sout_ref, s_scr, cr_scr, ck_scr, cv_scr = refs[_RWKV_INPUTS + n_alias:]
    j = pl.program_id(2)
    sb, c, hbw = zr_ref.shape
    pw = 2 * hd
    npair = hbw // pw
    n = sb * npair

    @pl.when(j == 0)
    def _():
        s0 = s0_ref[...].reshape(sb, npair, 2, hd, hd)
        zero = jnp.zeros((sb, npair, hd, hd), F32)
        top = jnp.concatenate([s0[:, :, 0], zero], axis=-1)
        bot = jnp.concatenate([zero, s0[:, :, 1]], axis=-1)
        s_scr[...] = jnp.concatenate([top, bot], axis=-2).reshape(n, pw, pw)
        cr_scr[...] = pr_ref[...]
        ck_scr[...] = pk_ref[...]
        cv_scr[...] = pv_ref[...]

    par_pair = lambda ref: jnp.concatenate([_split_heads(ref[...][None], pw)] * sb, axis=0)
    ops = _rwkv_prep((zr_ref, zk_ref, zv_ref), (cr_scr, ck_scr, cv_scr), (mur_ref, muk_ref, muv_ref),
                     lw_ref[...], a_ref[...], kk_ref[...], ka_ref[...], rk_ref[...], n_valid, hd)
    y, s_new = _rwkv_solve(ops, s_scr[...], par_pair(lnw_ref), par_pair(lnb_ref), hd, c)
    s_scr[...] = s_new
    out = _merge_heads(y, sb) * _silu(zg_ref[...])
    o_ref[...] = out.reshape(o_ref.shape).astype(o_ref.dtype)

    @pl.when(j == pl.num_programs(2) - 1)
    def _():
        s4 = s_new.reshape(sb, npair, pw, pw)
        both = jnp.stack([s4[:, :, :hd, :hd], s4[:, :, hd:, hd:]], axis=2)
        sout_ref[...] = both.reshape(sout_ref.shape)


def _rwkv(z3, lw, a, zprev, s0_all, l_in, mix, s_acc, l, depth, muz, kkp, kap, rkp, lnw, lnb,
          *, wa, wb, hbw, plan):
    b, t, _ = z3.shape
    sb, c = plan.sb_rwkv, plan.chunk
    hd = s0_all.shape[-1]
    nb = wb // hbw
    nh = hbw // hd
    off = 2 * wa // hbw
    zspec = lambda o: pl.BlockSpec((sb, c, hbw), lambda i, h, j, o=o: (i, j, o + h))
    pspec = lambda o: pl.BlockSpec((sb, 1, hbw), lambda i, h, j, o=o: (i, 0, o + h))
    mspec = lambda o: pl.BlockSpec((1, hbw), lambda i, h, j, o=o: (0, o + h))
    act = pl.BlockSpec((sb, c, hbw), lambda i, h, j: (i, j, h))
    st = lambda ll: pl.BlockSpec((None, sb, nh, hd, hd), lambda i, h, j, ll=ll: (ll, i, h, 0, 0))
    aliased = [mix, s_acc]
    n_in = _RWKV_INPUTS
    aliases = {n_in: 0, n_in + 1: 1}
    out = pl.pallas_call(
        functools.partial(_rwkv_body, n_valid=min(plan.n_valid, c), hd=hd, n_alias=len(aliased)),
        grid=(b // sb, nb, t // c),
        in_specs=[zspec(off), zspec(off + nb), zspec(off + 2 * nb), zspec(off + 3 * nb), act, act,
                  pspec(0), pspec(nb), pspec(2 * nb), st(l_in),
                  mspec(0), mspec(nb), mspec(2 * nb), mspec(0), mspec(0), mspec(0), mspec(0), mspec(0)]
        + [_ANY] * len(aliased),
        out_specs=[_act_spec(plan, t, sb, c, hbw, lambda h: wa // hbw + h, 3), st(l)],
        out_shape=[jax.ShapeDtypeStruct(mix.shape, mix.dtype),
                   jax.ShapeDtypeStruct((depth,) + s0_all.shape[1:], F32)],
        input_output_aliases=aliases,
        scratch_shapes=[pltpu.VMEM((sb * nh // 2, 2 * hd, 2 * hd), F32), pltpu.VMEM((sb, 1, hbw), F32),
                        pltpu.VMEM((sb, 1, hbw), F32), pltpu.VMEM((sb, 1, hbw), F32)],
        compiler_params=_cparams(3),
        name="rwkv7",
    )(z3, z3, z3, z3, lw, a, zprev, zprev, zprev, s0_all, muz, muz, muz, kkp, kap, rkp, lnw, lnb,
      *aliased)
    return out


def _rwkv_steps_body(*refs, n_valid, tpad, hd, n_alias):
    (zr_ref, zk_ref, zv_ref, zg_ref, lw_ref, a_ref, pr_ref, pk_ref, pv_ref, s0_ref,
     mur_ref, muk_ref, muv_ref, kk_ref, ka_ref, rk_ref, lnw_ref, lnb_ref) = refs[:_RWKV_INPUTS]
    o_ref, sout_ref, o_scr, vt_scr, y_scr = refs[_RWKV_INPUTS + n_alias:]
    bsz, cols = pr_ref.shape
    nh = cols // hd
    sout_ref[...] = s0_ref[...]
    o_scr[...] = jnp.zeros_like(o_scr)
    per_col = lambda ref: jnp.broadcast_to(ref[...], (bsz, cols)).T
    lnw_c, lnb_c, rk_c = per_col(lnw_ref), per_col(lnb_ref), per_col(rk_ref)
    step_rows = lambda ref, t: ref[pl.ds(t, bsz, stride=tpad), :]
    prev = (pr_ref[...], pk_ref[...], pv_ref[...])
    for t in range(n_valid):
        z = (step_rows(zr_ref, t), step_rows(zk_ref, t), step_rows(zv_ref, t))
        r, k, v = (zz + (pp - zz) * mu[...] for zz, pp, mu in zip(z, prev, (mur_ref, muk_ref, muv_ref)))
        prev = z
        a = step_rows(a_ref, t)
        w_t = jnp.exp(step_rows(lw_ref, t)).T
        a_t = a.T
        r_t = r.T
        v_t = v.T
        kk_t = (k * kk_ref[...]).T
        km_t = (k * (1.0 + (a - 1.0) * ka_ref[...])).T
        vt_scr[...] = v_t
        for hh in range(nh):
            sl = slice(hh * hd, (hh + 1) * hd)
            kkh = kk_t[sl]
            kkh = kkh / jnp.maximum(jnp.sqrt(jnp.sum(kkh * kkh, axis=0, keepdims=True)), KK_EPS)
            bh, wh, kmh, rh = kkh * a_t[sl], w_t[sl], km_t[sl], r_t[sl]

            def value_row(vi, carry, hh=hh, kkh=kkh, bh=bh, wh=wh, kmh=kmh, rh=rh):
                s = sout_ref[hh, vi]
                sa = jnp.sum(s * kkh, axis=0, keepdims=True)
                s = s * wh - sa * bh + vt_scr[pl.ds(hh * hd + vi, 1), :] * kmh
                sout_ref[hh, vi] = s
                y_scr[pl.ds(hh * hd + vi, 1), :] = jnp.sum(s * rh, axis=0, keepdims=True)
                return carry

            lax.fori_loop(0, hd, value_row, 0, unroll=8)
        y = y_scr[...]
        outs = []
        for hh in range(nh):
            sl = slice(hh * hd, (hh + 1) * hd)
            yh = y[sl]
            mean = jnp.mean(yh, axis=0, keepdims=True)
            var = jnp.mean(jnp.square(yh - mean), axis=0, keepdims=True)
            yn = (yh - mean) * lax.rsqrt(var + GN_EPS) * lnw_c[sl] + lnb_c[sl]
            bonus = jnp.sum(r_t[sl] * km_t[sl] * rk_c[sl], axis=0, keepdims=True) * v_t[sl]
            outs.append(yn + bonus)
        out = jnp.concatenate(outs, axis=0).T * _silu(step_rows(zg_ref, t))
        o_scr[pl.ds(t, bsz, stride=tpad), :] = out
    o_ref[...] = o_scr[...].astype(o_ref.dtype)


def _rwkv_steps(z2, lw2, a2, zprev, s0_all, l_in, mix, s_acc, l, depth, muz, kkp, kap, rkp, lnw, lnb,
                *, wa, wb, plan, tpad):
    bsz = zprev.shape[0]
    rows = z2.shape[0]
    hd = s0_all.shape[-2]
    cols = LANES
    nb = wb // cols
    nh = cols // hd
    off = 2 * wa // cols
    zspec = lambda o: pl.BlockSpec((rows, cols), lambda h, o=o: (0, o + h))
    pspec = lambda o: pl.BlockSpec((bsz, cols), lambda h, o=o: (0, o + h))
    mspec = lambda o: pl.BlockSpec((1, cols), lambda h, o=o: (0, o + h))
    st = lambda ll: pl.BlockSpec((None, nh, hd, hd, bsz), lambda h, ll=ll: (ll, h, 0, 0, 0))
    aliased = [mix, s_acc]
    n_in = _RWKV_INPUTS
    aliases = {n_in: 0, n_in + 1: 1}
    return pl.pallas_call(
        functools.partial(_rwkv_steps_body, n_valid=plan.n_valid, tpad=tpad, hd=hd, n_alias=len(aliased)),
        grid=(nb,),
        in_specs=[zspec(off), zspec(off + nb), zspec(off + 2 * nb), zspec(off + 3 * nb),
                  zspec(0), zspec(0), pspec(0), pspec(nb), pspec(2 * nb), st(l_in),
                  mspec(0), mspec(nb), mspec(2 * nb), mspec(0), mspec(0), mspec(0), mspec(0), mspec(0)]
        + [_ANY] * len(aliased),
        out_specs=[pl.BlockSpec((rows, cols), lambda h: (0, wa // cols + h)), st(l)],
        out_shape=[jax.ShapeDtypeStruct(mix.shape, mix.dtype),
                   jax.ShapeDtypeStruct((depth,) + s0_all.shape[1:], F32)],
        input_output_aliases=aliases,
        scratch_shapes=[pltpu.VMEM((rows, cols), F32), pltpu.VMEM((cols, bsz), F32),
                        pltpu.VMEM((cols, bsz), F32)],
        compiler_params=_cparams(1),
        name="rwkv7_steps",
    )(z2, z2, z2, z2, lw2, a2, zprev, zprev, zprev, s0_all, muz, muz, muz, kkp, kap, rkp, lnw, lnb,
      *aliased)


def _hgrn_head(q, f, v, g, s0, lb, ng, n_valid, sub, stage):
    b2_ref, kf_ref, v_ref = stage
    bcast_row = lambda ref, i: ref[pl.ds(i, 1), :]
    c, dk = q.shape
    sig = jax.nn.sigmoid(f)
    logg = jnp.log(jnp.maximum(lb + (1.0 - lb) * sig, TINY))
    kf = (1.0 - lb) * jax.nn.sigmoid(-f)
    if n_valid < c:
        valid = lax.broadcasted_iota(jnp.int32, (c, dk), 0) < n_valid
        logg = jnp.where(valid, logg, 0.0)
        kf = jnp.where(valid, kf, 0.0)
    bc = _cumsum_time(logg, 0)
    b2 = bc * LOG2E
    b2_ref[...] = b2
    kf_ref[...] = kf
    v_ref[...] = v
    o_inter = _bdot(q * jnp.exp(bc), s0)
    pairs, w_rows = [], []
    for lo in range(0, c, sub):
        for r0 in range(lo, lo + sub, SUBLANES):
            qg, bg = q[r0:r0 + SUBLANES], b2[r0:r0 + SUBLANES]
            for src in range(lo, min(r0 + SUBLANES, n_valid)):
                pairs.append((r0, src))
                w_rows.append(qg * jnp.exp2(bg - bcast_row(b2_ref, src)) * bcast_row(kf_ref, src))
    att_rows = _bdot(jnp.concatenate(w_rows, axis=0), jnp.ones((dk, v.shape[1]), BF16))
    t_row = lax.broadcasted_iota(jnp.int32, (SUBLANES, 1), 0)
    groups = {}
    for u, (r0, src) in enumerate(pairs):
        col = att_rows[u * SUBLANES:(u + 1) * SUBLANES]
        if src > r0:
            col = jnp.where(t_row >= src - r0, col, 0.0)
        term = col * bcast_row(v_ref, src)
        groups[r0] = groups[r0] + term if r0 in groups else term
    blocks = []
    for lo in range(0, c, sub):
        ob = o_inter[lo:lo + sub]
        if lo > 0:
            bs = bc[lo - 1:lo]
            att = _bdot(q[lo:lo + sub] * jnp.exp(bc[lo:lo + sub] - bs), kf[:lo] * jnp.exp(bs - bc[:lo]), NT)
            ob = ob + _bdot(att, v[:lo])
        none = jnp.zeros((SUBLANES, v.shape[1]), F32)
        blocks.append(ob + jnp.concatenate([groups.get(r0, none) for r0 in range(lo, lo + sub, SUBLANES)],
                                           axis=0))
    o = jnp.concatenate(blocks, axis=0) if len(blocks) > 1 else blocks[0]
    b_last = bc[c - 1:c]
    e_col = jnp.broadcast_to(jnp.exp(b_last), (v.shape[1], dk)).T
    s_new = e_col * s0 + _bdot(kf * jnp.exp(b_last - bc), v, TN)
    on = o * lax.rsqrt(jnp.mean(o * o, axis=-1, keepdims=True) + RMS_EPS) * ng
    return on * _silu(g), s_new


_HGRN_INPUTS = 7


def _hgrn_body(*refs, n_valid, dk, sub, n_alias):
    q_ref, f_ref, i_ref, g_ref, s0_ref, lb_ref, ng_ref = refs[:_HGRN_INPUTS]
    o_ref, sout_ref, s_scr, o_scr, b2_scr, kf_scr, v_scr = refs[_HGRN_INPUTS + n_alias:]
    j = pl.program_id(2)

    @pl.when(j == 0)
    def _():
        s_scr[...] = s0_ref[...]

    sb, c, hcw = q_ref.shape
    nh = hcw // dk
    lb = lb_ref[...]
    ng = ng_ref[...]

    def one(s, carry, slot0=0):
        q, f, v, g = q_ref[s], f_ref[s], i_ref[s], g_ref[s]
        outs = []
        for hh in range(nh):
            sl = slice(hh * dk, (hh + 1) * dk)
            slot = slot0 + hh
            o, s_new = _hgrn_head(q[:, sl], f[:, sl], v[:, sl], g[:, sl], s_scr[s, hh],
                                  lb[:, sl], ng[:, sl], n_valid, sub,
                                  (b2_scr.at[slot], kf_scr.at[slot], v_scr.at[slot]))
            s_scr[s, hh] = s_new
            outs.append(o)
        o_scr[s] = jnp.concatenate(outs, axis=1) if nh > 1 else outs[0]
        return carry

    unroll = math.gcd(sb, HGRN_UNROLL_SEQS)

    def group(gi, carry):
        for u in range(unroll):
            one(gi * unroll + u, carry, u * nh)
        return carry

    if sb == unroll:
        group(0, 0)
    else:
        lax.fori_loop(0, sb // unroll, group, 0)
    o_ref[...] = o_scr[...].reshape(o_ref.shape).astype(o_ref.dtype)

    @pl.when(j == pl.num_programs(2) - 1)
    def _():
        sout_ref[...] = s_scr[...]


def _hgrn(z3, s0_all, l_in, mix, s_acc, l, depth, lb, ng, *, col0, out_col0, wc, hcw, plan):
    b, t, _ = z3.shape
    sb, c = plan.sb_hgrn, plan.chunk
    dk, dv = s0_all.shape[-2], s0_all.shape[-1]
    nb = wc // hcw
    nh = hcw // dk
    off = col0 // hcw
    zspec = lambda o: pl.BlockSpec((sb, c, hcw), lambda i, h, j, o=o: (i, j, o + h))
    st = lambda ll: pl.BlockSpec((None, sb, nh, dk, dv), lambda i, h, j, ll=ll: (ll, i, h, 0, 0))
    par = pl.BlockSpec((1, hcw), lambda i, h, j: (0, h))
    aliased = [mix, s_acc]
    n_in = _HGRN_INPUTS
    aliases = {n_in: 0, n_in + 1: 1}
    return pl.pallas_call(
        functools.partial(_hgrn_body, n_valid=min(plan.n_valid, c), dk=dk, sub=min(2 * SUBLANES, c),
                          n_alias=len(aliased)),
        grid=(b // sb, nb, t // c),
        in_specs=[zspec(off), zspec(off + nb), zspec(off + 2 * nb), zspec(off + 3 * nb), st(l_in),
                  par, par] + [_ANY] * len(aliased),
        out_specs=[_act_spec(plan, t, sb, c, hcw, lambda h: out_col0 // hcw + h, 3), st(l)],
        out_shape=[jax.ShapeDtypeStruct(mix.shape, mix.dtype),
                   jax.ShapeDtypeStruct((depth,) + s0_all.shape[1:], F32)],
        input_output_aliases=aliases,
        scratch_shapes=[pltpu.VMEM((sb, nh, dk, dv), F32), pltpu.VMEM((sb, c, hcw), F32)]
        + [pltpu.VMEM((nh * math.gcd(sb, HGRN_UNROLL_SEQS), c, dk), F32)] * 3,
        compiler_params=_cparams(3),
        name="hgrn2",
    )(z3, z3, z3, z3, s0_all, lb, ng, *aliased)


def _post1_body(h_ref, m_ref, g_ref, h1_ref, hn_ref):
    m = m_ref[...].astype(F32)
    h1 = h_ref[...] + m * lax.rsqrt(jnp.mean(m * m, axis=-1, keepdims=True) + RMS_EPS) * g_ref[...]
    h1_ref[...] = h1
    hn_ref[...] = (h1 * lax.rsqrt(jnp.mean(h1 * h1, axis=-1, keepdims=True) + RMS_EPS)).astype(hn_ref.dtype)


def _post2_body(h_ref, p_ref, proj_ref, gate_ref, g_ref, o_ref):
    u = jnp.dot(p_ref[...], proj_ref[...], preferred_element_type=F32)
    x = u * jax.nn.sigmoid(gate_ref[...].astype(F32))
    o_ref[...] = h_ref[...] + x * lax.rsqrt(jnp.mean(x * x, axis=-1, keepdims=True) + RMS_EPS) * g_ref[...]


def _post_ple(h1, p2, proj_all, l, gate, gain, tr=256):
    n, d = h1.shape
    kp = p2.shape[1]
    tr = min(tr, n)
    row = pl.BlockSpec((tr, d), lambda i: (i, 0))
    return pl.pallas_call(
        _post2_body,
        grid=(n // tr,),
        in_specs=[row, pl.BlockSpec((tr, kp), lambda i: (i, 0)),
                  pl.BlockSpec((None, kp, d), lambda i: (l, 0, 0)), row,
                  pl.BlockSpec((1, d), lambda i: (0, 0))],
        out_specs=row,
        out_shape=jax.ShapeDtypeStruct((n, d), F32),
        compiler_params=_cparams(1),
        name="post_ple",
    )(h1, p2, proj_all, gate, gain)


def _rowwise(body, name, arrays, gain, out_dtypes, tr=256):
    n, d = arrays[0].shape
    tr = min(tr, n)
    row = pl.BlockSpec((tr, d), lambda i: (i, 0))
    return pl.pallas_call(
        body,
        grid=(n // tr,),
        in_specs=[row] * len(arrays) + [pl.BlockSpec((1, d), lambda i: (0, 0))],
        out_specs=[row] * len(out_dtypes),
        out_shape=[jax.ShapeDtypeStruct((n, d), dt) for dt in out_dtypes],
        compiler_params=_cparams(1),
        name=name,
    )(*arrays, gain)


def _block_diag_tiles(w):
    nb, bs, _ = w.shape
    per = LANES // bs
    w = w.reshape(nb // per, per, bs, bs)
    eye = jnp.eye(per, dtype=w.dtype)
    return jnp.einsum('tpab,pq->tpaqb', w, eye).reshape(nb // per, LANES, LANES)


def _layer_stack(x, p, st, W, mm, plan, zero_state):
    st_a_h, st_a_conv, st_b_s, st_b_x, st_b_z, st_c_s = st
    b, t, d = x.shape
    depth = W['g_pre'].shape[0]
    wa = st_a_h.shape[-1]
    hd = st_b_s.shape[-1]
    wb = st_b_s.shape[-3] * hd
    dk = st_c_s.shape[-2]
    wc = st_c_s.shape[-3] * dk
    mix_w = wa + wb + wc
    n = b * t
    nv = plan.n_valid
    hbw = math.gcd(math.gcd(2 * wa, wb), math.gcd(wa, 512))
    hcw = math.gcd(math.gcd(2 * wa + 4 * wb, wc), math.gcd(wa + wb, 512))

    lb_soft = jax.nn.softmax(W['hgrn_lb'].astype(F32), axis=0)
    lb_all = jnp.cumsum(lb_soft, axis=0) - lb_soft[0]
    p_act = p.astype(BF16)

    if plan.rwkv_steps:
        st_b_s = jnp.transpose(st_b_s, (0, 2, 3, 4, 1))
    rwkv_params = lambda l: (
        W['rwkv_mu_z'][l][None], W['rwkv_kk'][l][None], W['rwkv_ka'][l][None],
        W['rwkv_rk'][l].reshape(1, wb), W['rwkv_lnx_w'][l][None], W['rwkv_lnx_b'][l][None])

    h = x
    small = [[] for _ in range(4)]
    sb_acc = jnp.zeros((depth,) + st_b_s.shape[1:], F32)
    sc_acc = jnp.zeros((depth,) + st_c_s.shape[1:], F32)
    for l in range(depth):
        ls = 0 if zero_state else l
        xn, lw, a, xlast = _prenorm(
            h, W['g_pre'][l].reshape(1, 1, d), st_b_x[ls][:, None, :], W['rwkv_lora_in'][l],
            W['rwkv_w2'][l], W['rwkv_a2'][l], W['rwkv_w0'][l][None], W['rwkv_a0'][l][None], plan)
        z3 = mm('w_in', xn.reshape(n, d), l, F32).reshape(b, t, -1)
        cb8 = jnp.pad(st_a_conv[ls], ((0, 0), (SUBLANES - (CONV_W - 1), 0), (0, 0)))
        mix, ha, tail = _rglru(
            z3, st_a_h[ls][:, None, :], cb8, W['conv_a_w'][l][None], W['conv_a_b'][l].reshape(1, 1, wa),
            W['lru_wr_bd'][l], W['lru_br'][l][None], W['lru_wi_bd'][l], W['lru_bi'][l][None],
            W['lru_lambda'][l][None], wa=wa, mix_w=mix_w, plan=plan)
        if plan.rwkv_steps:
            mix, sb_acc = _rwkv_steps(
                z3.reshape(n, -1), lw.reshape(n, wb), a.reshape(n, wb), st_b_z[ls], st_b_s, ls,
                mix, sb_acc, l, depth, *rwkv_params(l), wa=wa, wb=wb, plan=plan, tpad=t)
        else:
            mix, sb_acc = _rwkv(
                z3, lw, a, st_b_z[ls][:, None, :], st_b_s, ls, mix, sb_acc, l, depth,
                *rwkv_params(l), wa=wa, wb=wb, hbw=hbw, plan=plan)
        mix, sc_acc = _hgrn(
            z3, st_c_s, ls, mix, sc_acc, l, depth, lb_all[l][None], W['hgrn_norm_g'][l][None],
            col0=2 * wa + 4 * wb, out_col0=wa + wb, wc=wc, hcw=hcw, plan=plan)
        mo = mm('w_out', mix.reshape(n, mix_w), l, BF16)
        h1, hn = _rowwise(_post1_body, "post_mix", [h.reshape(n, d), mo], W['g_post'][l][None],
                          [F32, BF16])
        gp = mm('ple_gate', hn, l, BF16)
        h = _post_ple(h1, p_act[l].reshape(n, -1), W['ple_proj'], l, gp, W['g_ple'][l][None]).reshape(b, t, d)
        small[0].append(ha[:, 0])
        small[1].append(tail[:, SUBLANES - (CONV_W - 1):])
        small[2].append(xlast[:, 0])
        small[3].append(z3[:, nv - 1, 2 * wa:2 * wa + 3 * wb])
    na_h, na_c, nb_x, nb_z = (jnp.stack(o) for o in small)
    if plan.rwkv_steps:
        sb_acc = jnp.transpose(sb_acc, (0, 4, 1, 2, 3))
    return h, (na_h, na_c, sb_acc, nb_x, nb_z, sc_acc)


def kernel(x_prompt, x_sample, p_prompt, p_sample, state_a_h, state_a_conv, state_b_S,
           state_b_xprev, state_b_zprev, state_c_S, g_pre, g_post, w_in, w_out,
           conv_a_w, conv_a_b, lru_wr, lru_br, lru_wi, lru_bi, lru_lambda,
           rwkv_mu_z, rwkv_mu_w, rwkv_mu_a, rwkv_w0, rwkv_w1, rwkv_w2, rwkv_a0, rwkv_a1,
           rwkv_a2, rwkv_kk, rwkv_ka, rwkv_rk, rwkv_lnx_w, rwkv_lnx_b, hgrn_lb, hgrn_norm_g,
           ple_proj, ple_gate, g_ple):
    depth = w_in.shape[0]
    W = dict(g_pre=g_pre, g_post=g_post, conv_a_w=conv_a_w, conv_a_b=conv_a_b,
             lru_br=lru_br, lru_bi=lru_bi, lru_lambda=lru_lambda, rwkv_mu_z=rwkv_mu_z,
             rwkv_mu_w=rwkv_mu_w, rwkv_mu_a=rwkv_mu_a, rwkv_w0=rwkv_w0, rwkv_a0=rwkv_a0,
             rwkv_kk=rwkv_kk, rwkv_ka=rwkv_ka, rwkv_rk=rwkv_rk, rwkv_lnx_w=rwkv_lnx_w,
             rwkv_lnx_b=rwkv_lnx_b, hgrn_lb=hgrn_lb, hgrn_norm_g=hgrn_norm_g, g_ple=g_ple)
    for name, w in (('rwkv_w2', rwkv_w2), ('rwkv_a2', rwkv_a2), ('ple_proj', ple_proj)):
        W[name] = w.astype(BF16)
    mw, ma = rwkv_mu_w[:, :, None], rwkv_mu_a[:, :, None]
    W['rwkv_lora_in'] = jnp.concatenate(
        [(1.0 - mw) * rwkv_w1, (1.0 - ma) * rwkv_a1, mw * rwkv_w1, ma * rwkv_a1], axis=-1).astype(BF16)
    W['lru_wr_bd'] = jnp.stack([_block_diag_tiles(lru_wr[l]) for l in range(depth)]).astype(BF16)
    W['lru_wi_bd'] = jnp.stack([_block_diag_tiles(lru_wi[l]) for l in range(depth)]).astype(BF16)

    bp, tp, d = x_prompt.shape
    bs, ts, _ = x_sample.shape
    dt = x_prompt.dtype
    st_s = (state_a_h, state_a_conv, state_b_S, state_b_xprev, state_b_zprev, state_c_S)
    st_p = tuple(jnp.zeros((1, bp) + s.shape[2:], dt) for s in st_s)

    big = dict(w_in=w_in, w_out=w_out, ple_gate=ple_gate)
    wq = {name: [None] * depth for name in big}

    def mm_sample(name, x, l, out_dtype):
        if x.shape[0] <= MM_TILE:
            y, wq[name][l] = _matmul_wcast(x, big[name], l, out_dtype)
            return y
        wq[name][l] = big[name][l].astype(BF16)
        return _matmul(x, wq[name][l], out_dtype)

    def mm_prompt(name, x, l, out_dtype):
        return _matmul(x, wq[name][l], out_dtype)

    tpad = -(-ts // SUBLANES) * SUBLANES
    pad_t = lambda v, ax: jnp.pad(v, [(0, tpad - ts) if i == ax else (0, 0) for i in range(v.ndim)])
    y_s, out_s = _layer_stack(pad_t(x_sample, 1), pad_t(p_sample, 2), st_s, W, mm_sample,
                              _make_plan(bs, tpad, ts), False)
    y_p, out_p = _layer_stack(x_prompt, p_prompt, st_p, W, mm_prompt, _make_plan(bp, tp, tp), True)
    return (y_p, y_s[:, :ts]) + out_p + out_s
```

```python
import functools
import math
from typing import NamedTuple

import jax
import jax.numpy as jnp
from jax import lax
from jax.experimental import pallas as pl
from jax.experimental.pallas import tpu as pltpu

F32 = jnp.float32
BF16 = jnp.bfloat16

RMS_EPS = 1e-6
GN_EPS = 64e-5
LRU_C = 8.0
TINY = 1e-30
KK_EPS = 1e-12
CONV_W = 4
LOG2E = 1.4426950408889634
RWKV_CHUNKS_PER_STEP = 4
NORM_GROUP_ROWS = 128
HGRN_UNROLL_SEQS = 4

LANES = 128
SUBLANES = 8
BF16_ROWS = 16
VMEM_LIMIT = 52 * 1024 * 1024

NT = (((1,), (1,)), ((), ()))
TN = (((0,), (0,)), ((), ()))
BNT = (((2,), (2,)), ((0,), (0,)))
BNN = (((2,), (1,)), ((0,), (0,)))
BTN = (((1,), (1,)), ((0,), (0,)))


class Plan(NamedTuple):
    n_valid: int
    tt: int
    sb_rows: int
    tt_norm: int
    sb_norm: int
    chunk: int
    sb_rwkv: int
    sb_hgrn: int
    flat_acts: bool
    rwkv_steps: bool


def _make_plan(b, t, n_valid):
    tt = min(t, 128)
    chunk = min(t, 64)
    rows = 128
    sb = max(1, min(b, rows // tt))
    tt_norm = min(t, 2 * rows)
    sb_norm = max(1, min(b, 2 * rows // tt_norm))
    flat = chunk == t
    assert flat or (chunk % BF16_ROWS == 0 and tt % BF16_ROWS == 0)
    return Plan(n_valid=n_valid, tt=tt, sb_rows=sb, tt_norm=tt_norm, sb_norm=sb_norm, chunk=chunk,
                sb_rwkv=math.gcd(b, max(4, rows // chunk)),
                sb_hgrn=sb if chunk < 64 else math.gcd(b, HGRN_UNROLL_SEQS),
                flat_acts=flat, rwkv_steps=flat and b % LANES == 0 and t <= SUBLANES)


def _act_shape(plan, b, t, width):
    return jax.ShapeDtypeStruct((b * t, width) if plan.flat_acts else (b, t, width), BF16)


def _act_spec(plan, t, sb, rows, width, col, grid_rank):
    if grid_rank == 2:
        if plan.flat_acts:
            return pl.BlockSpec((sb * rows, width), lambda i, j: (i * (t // rows) + j, col(0)))
        return pl.BlockSpec((sb, rows, width), lambda i, j: (i, j, col(0)))
    if plan.flat_acts:
        return pl.BlockSpec((sb * rows, width), lambda i, h, j: (i * (t // rows) + j, col(h)))
    return pl.BlockSpec((sb, rows, width), lambda i, h, j: (i, j, col(h)))


def _cparams(n_axes):
    return pltpu.CompilerParams(dimension_semantics=("arbitrary",) * n_axes,
                                vmem_limit_bytes=VMEM_LIMIT)


def _softplus(x):
    return jnp.maximum(x, 0.0) + jnp.log1p(jnp.exp(-jnp.abs(x)))


def _silu(x):
    return x * jax.nn.sigmoid(x)


def _bdot(a, b, dims=None):
    a = a.astype(BF16)
    b = b.astype(BF16)
    if dims is None:
        return jnp.dot(a, b, preferred_element_type=F32)
    return lax.dot_general(a, b, dims, preferred_element_type=F32)


def _cumsum_time(x, axis):
    n = x.shape[axis]
    idx = lax.broadcasted_iota(jnp.int32, x.shape, axis)
    d = 1
    while d < n:
        x = x + jnp.where(idx >= d, pltpu.roll(x, d, axis), 0.0)
        d *= 2
    return x


_ANY = pl.BlockSpec(memory_space=pl.ANY)


MM_TILE = 1024


def _mm_body(x_ref, w_ref, o_ref):
    o_ref[...] = jnp.dot(x_ref[...], w_ref[...], preferred_element_type=F32).astype(o_ref.dtype)


def _mm_wcast_body(x_ref, w_ref, o_ref, wb_ref):
    wb = w_ref[...].astype(BF16)
    wb_ref[...] = wb
    o_ref[...] = jnp.dot(x_ref[...], wb, preferred_element_type=F32).astype(o_ref.dtype)


def _matmul(x, w, out_dtype):
    m, k = x.shape
    n = w.shape[1]
    tm = math.gcd(m, MM_TILE)
    tn = math.gcd(n, MM_TILE)
    return pl.pallas_call(
        _mm_body,
        grid=(m // tm, n // tn),
        in_specs=[pl.BlockSpec((tm, k), lambda i, j: (i, 0)),
                  pl.BlockSpec((k, tn), lambda i, j: (0, j))],
        out_specs=pl.BlockSpec((tm, tn), lambda i, j: (i, j)),
        out_shape=jax.ShapeDtypeStruct((m, n), out_dtype),
        compiler_params=_cparams(2),
        name="matmul",
    )(x, w)


def _matmul_wcast(x, w_all, l, out_dtype):
    m, k = x.shape
    n = w_all.shape[2]
    tn = math.gcd(n, MM_TILE // 2)
    return pl.pallas_call(
        _mm_wcast_body,
        grid=(n // tn,),
        in_specs=[pl.BlockSpec((m, k), lambda j: (0, 0)),
                  pl.BlockSpec((None, k, tn), lambda j: (l, 0, j))],
        out_specs=[pl.BlockSpec((m, tn), lambda j: (0, j)),
                   pl.BlockSpec((k, tn), lambda j: (0, j))],
        out_shape=[jax.ShapeDtypeStruct((m, n), out_dtype), jax.ShapeDtypeStruct((k, n), BF16)],
        compiler_params=_cparams(1),
        name="matmul_wcast",
    )(x, w_all)


def _prenorm_body(h_ref, g_ref, xprev_ref, wf_ref, w2_ref, a2_ref, w0_ref, a0_ref,
                  xn_ref, lw_ref, a_ref, xlast_ref, carry_ref, *, last_row):
    j = pl.program_id(1)
    sb, tt, d = h_ref.shape
    r2 = wf_ref.shape[1] // 2
    rank = r2 // 2

    @pl.when(j == 0)
    def _():
        xp = jnp.broadcast_to(xprev_ref[...], (sb, SUBLANES, d)).reshape(sb * SUBLANES, d)
        carry_ref[...] = _bdot(xp, wf_ref[:, r2:]).reshape(sb, SUBLANES, r2)[:, 0:1, :]

    gt = min(tt, NORM_GROUP_ROWS)
    gs = max(1, min(sb, NORM_GROUP_ROWS // gt))
    for s0 in range(0, sb, gs):
        for t0 in range(0, tt, gt):
            h = h_ref[s0:s0 + gs, t0:t0 + gt, :]
            xn = h * lax.rsqrt(jnp.mean(h * h, axis=-1, keepdims=True) + RMS_EPS) * g_ref[...]
            xn_b = xn.reshape(gs * gt, d).astype(BF16)
            if len(xn_ref.shape) == 2:
                xn_ref[s0 * tt + t0:s0 * tt + t0 + gs * gt, :] = xn_b
            else:
                xn_ref[s0:s0 + gs, t0:t0 + gt, :] = xn_b.reshape(gs, gt, d)
            if t0 <= last_row < t0 + gt:
                @pl.when(j == pl.num_programs(1) - 1)
                def _(xn=xn, s0=s0, t0=t0):
                    xlast_ref[s0:s0 + gs] = xn[:, last_row - t0:last_row - t0 + 1, :]

            prod = jnp.dot(xn_b, wf_ref[...], preferred_element_type=F32)
            shifted = prod[:, r2:].reshape(gs, gt, r2)
            t_idx = lax.broadcasted_iota(jnp.int32, (gs, gt, r2), 1)
            prev = jnp.where(t_idx == 0, carry_ref[s0:s0 + gs], pltpu.roll(shifted, 1, 1))
            carry_ref[s0:s0 + gs] = shifted[:, gt - 1:gt, :]
            pre = prod[:, :r2] + prev.reshape(gs * gt, r2)
            yw = w0_ref[...] + _bdot(jnp.tanh(pre[:, :rank]), w2_ref[...])
            wl = -_softplus(-yw) - 0.5
            lw_ref[s0:s0 + gs, t0:t0 + gt, :] = (-jnp.exp(wl)).reshape(gs, gt, -1)
            ya = a0_ref[...] + _bdot(pre[:, rank:], a2_ref[...])
            a_ref[s0:s0 + gs, t0:t0 + gt, :] = jax.nn.sigmoid(ya).reshape(gs, gt, -1)


def _prenorm(h3, g, xprev, wf, w2, a2, w0, a0, plan):
    b, t, d = h3.shape
    sb, tt = plan.sb_norm, plan.tt_norm
    wb = w2.shape[1]
    rank = w2.shape[0]
    row3 = lambda i, j: (i, j, 0)
    par3 = lambda i, j: (0, 0, 0)
    par2 = lambda i, j: (0, 0)
    seq3 = lambda i, j: (i, 0, 0)
    return pl.pallas_call(
        functools.partial(_prenorm_body, last_row=(plan.n_valid - 1) % tt),
        grid=(b // sb, t // tt),
        in_specs=[pl.BlockSpec((sb, tt, d), row3),
                  pl.BlockSpec((1, 1, d), par3),
                  pl.BlockSpec((sb, 1, d), seq3),
                  pl.BlockSpec((d, 4 * rank), par2),
                  pl.BlockSpec((rank, wb), par2),
                  pl.BlockSpec((rank, wb), par2),
                  pl.BlockSpec((1, wb), par2),
                  pl.BlockSpec((1, wb), par2)],
        out_specs=[_act_spec(plan, t, sb, tt, d, lambda h: 0, 2),
                   pl.BlockSpec((sb, tt, wb), row3),
                   pl.BlockSpec((sb, tt, wb), row3),
                   pl.BlockSpec((sb, 1, d), seq3)],
        out_shape=[_act_shape(plan, b, t, d),
                   jax.ShapeDtypeStruct((b, t, wb), F32),
                   jax.ShapeDtypeStruct((b, t, wb), F32),
                   jax.ShapeDtypeStruct((b, 1, d), F32)],
        scratch_shapes=[pltpu.VMEM((sb, 1, 2 * rank), F32)],
        compiler_params=_cparams(2),
        name="prenorm_lora",
    )(h3, g, xprev, wf, w2, a2, w0, a0)


def _rglru_body(x_ref, gt_ref, h0_ref, cb_ref, cw_ref, cbias_ref, wr_ref, br_ref, wi_ref, bi_ref,
                lam_ref, o_ref, hout_ref, tail_ref, hc_ref, tl_ref, *, nv_last):
    j = pl.program_id(1)

    @pl.when(j == 0)
    def _():
        hc_ref[...] = h0_ref[...]
        tl_ref[...] = cb_ref[...]

    x = x_ref[...]
    sb, tt, w = x.shape
    ext = jnp.concatenate([tl_ref[...], x], axis=1)
    tl_ref[...] = ext[:, tt:tt + SUBLANES, :]
    u = cbias_ref[...]
    for tap in range(CONV_W):
        sh = CONV_W - 1 - tap
        xs = x if sh == 0 else pltpu.roll(ext, sh, 1)[:, SUBLANES:, :]
        u = u + xs * cw_ref[:, tap:tap + 1, :]
    u2 = u.reshape(sb * tt, w)
    rp, ip = [], []
    for m in range(w // LANES):
        um = u2[:, m * LANES:(m + 1) * LANES]
        rp.append(_bdot(um, wr_ref[m]))
        ip.append(_bdot(um, wi_ref[m]))
    r = jax.nn.sigmoid(jnp.concatenate(rp, axis=1) + br_ref[...])
    i = jax.nn.sigmoid(jnp.concatenate(ip, axis=1) + bi_ref[...])
    log_a = -LRU_C * r * _softplus(-lam_ref[...])
    th = jnp.tanh(log_a)
    bcoef = jnp.sqrt(-2.0 * th / (1.0 - th)) * (i * u2)
    nt = tt // SUBLANES
    a_cum = jnp.exp(log_a).reshape(sb * nt, SUBLANES, w)
    b_cum = bcoef.reshape(sb * nt, SUBLANES, w)
    t_idx = lax.broadcasted_iota(jnp.int32, (sb * nt, SUBLANES, w), 1)
    d = 1
    while d < SUBLANES:
        keep = t_idx >= d
        a_prev = jnp.where(keep, pltpu.roll(a_cum, d, 1), 1.0)
        b_prev = jnp.where(keep, pltpu.roll(b_cum, d, 1), 0.0)
        b_cum = a_cum * b_prev + b_cum
        a_cum = a_cum * a_prev
        d *= 2
    a_end = a_cum.reshape(sb, nt, SUBLANES, w)[:, :, SUBLANES - 1:, :]
    b_end = b_cum.reshape(sb, nt, SUBLANES, w)[:, :, SUBLANES - 1:, :]
    h_in = [hc_ref[...]]
    for g in range(nt):
        h_in.append(a_end[:, g] * h_in[g] + b_end[:, g])
    hc_ref[...] = h_in[nt]
    h_tile = jnp.stack(h_in[:nt], axis=1).reshape(sb * nt, 1, w)
    hs = (a_cum * h_tile + b_cum).reshape(sb, tt, w)
    out = (hs * _silu(gt_ref[...])).reshape(sb * tt, w)
    rest = jnp.zeros((sb * tt, o_ref.shape[-1] - w), F32)
    o_ref[...] = jnp.concatenate([out, rest], axis=-1).reshape(o_ref.shape).astype(o_ref.dtype)

    @pl.when(j == pl.num_programs(1) - 1)
    def _():
        hout_ref[...] = hs[:, nv_last - 1:nv_last, :]
        if nv_last == tt:
            tail_ref[...] = ext[:, tt:tt + SUBLANES, :]
        else:
            tail_ref[...] = pltpu.roll(ext, tt + SUBLANES - nv_last, 1)[:, 0:SUBLANES, :]


def _rglru(z3, h0, cb8, cw, cbias, wr_bd, br, wi_bd, bi, lam, *, wa, mix_w, plan):
    b, t, _ = z3.shape
    sb, tt = plan.sb_rows, plan.tt
    nt = wa // LANES
    par3 = lambda i, j: (0, 0, 0)
    seq3 = lambda i, j: (i, 0, 0)
    return pl.pallas_call(
        functools.partial(_rglru_body, nv_last=(plan.n_valid - 1) % tt + 1),
        grid=(b // sb, t // tt),
        in_specs=[pl.BlockSpec((sb, tt, wa), lambda i, j: (i, j, 0)),
                  pl.BlockSpec((sb, tt, wa), lambda i, j: (i, j, 1)),
                  pl.BlockSpec((sb, 1, wa), seq3),
                  pl.BlockSpec((sb, SUBLANES, wa), seq3),
                  pl.BlockSpec((1, CONV_W, wa), par3),
                  pl.BlockSpec((1, 1, wa), par3),
                  pl.BlockSpec((nt, LANES, LANES), par3),
                  pl.BlockSpec((1, wa), lambda i, j: (0, 0)),
                  pl.BlockSpec((nt, LANES, LANES), par3),
                  pl.BlockSpec((1, wa), lambda i, j: (0, 0)),
                  pl.BlockSpec((1, wa), lambda i, j: (0, 0))],
        out_specs=[_act_spec(plan, t, sb, tt, mix_w, lambda h: 0, 2),
                   pl.BlockSpec((sb, 1, wa), seq3),
                   pl.BlockSpec((sb, SUBLANES, wa), seq3)],
        out_shape=[_act_shape(plan, b, t, mix_w),
                   jax.ShapeDtypeStruct((b, 1, wa), F32),
                   jax.ShapeDtypeStruct((b, SUBLANES, wa), F32)],
        scratch_shapes=[pltpu.VMEM((sb, 1, wa), F32), pltpu.VMEM((sb, SUBLANES, wa), F32)],
        compiler_params=_cparams(2),
        name="rglru",
    )(z3, z3, h0, cb8, cw, cbias, wr_bd, br, wi_bd, bi, lam)


def _split_heads(x, hd):
    sb, c, w = x.shape
    nh = w // hd
    st = jnp.stack([x[:, :, h * hd:(h + 1) * hd] for h in range(nh)], axis=1)
    return st.reshape(sb * nh, c, hd)


def _merge_heads(x, sb):
    n, c, hd = x.shape
    nh = n // sb
    x4 = x.reshape(sb, nh, c, hd)
    return jnp.concatenate([x4[:, h] for h in range(nh)], axis=-1)


def _pair_diag(x, half):
    lo = lax.broadcasted_iota(jnp.int32, x.shape, 2) < half
    return jnp.concatenate([jnp.where(lo, x, 0.0), jnp.where(lo, 0.0, x)], axis=1)


def _pair_sum(x, half):
    lo = lax.broadcasted_iota(jnp.int32, x.shape, 2) < half
    s_lo = jnp.sum(jnp.where(lo, x, 0.0), axis=-1, keepdims=True)
    s_hi = jnp.sum(jnp.where(lo, 0.0, x), axis=-1, keepdims=True)
    return jnp.where(lo, s_lo, s_hi)


_RWKV_INPUTS = 18


def _rwkv_prep(rows, z_refs, carry_refs, mu_refs, lw, a, kk_p, ka_p, rk_p, n_valid, hd):
    sb, c, hbw = lw.shape
    pw = 2 * hd
    row = lax.broadcasted_iota(jnp.int32, (sb, c, hbw), 1)
    first = row == 0

    def mix(z_ref, carry_ref, mu_ref):
        z = z_ref[:, rows, :]
        zm = z + (jnp.where(first, carry_ref[...], pltpu.roll(z, 1, 1)) - z) * mu_ref[...]
        carry_ref[...] = z[:, c - 1:c, :]
        return zm

    r, k, v = (mix(zr, cr, mu) for zr, cr, mu in zip(z_refs, carry_refs, mu_refs))
    kk_raw = k * kk_p
    kmod = k * (1.0 + (a - 1.0) * ka_p)
    if n_valid < c:
        valid = row < n_valid
        lw = jnp.where(valid, lw, 0.0)
        kmod = jnp.where(valid, kmod, 0.0)
        v = jnp.where(valid, v, 0.0)
        a = jnp.where(valid, a, 0.0)
    cl = _cumsum_time(lw, 1)

    pairs = lambda x: _split_heads(x, pw)
    rk_pair = jnp.concatenate([_split_heads(rk_p[None], pw)] * sb, axis=0)
    kkp, ap, kmp, vp, rp, clp, lwp = map(pairs, (kk_raw, a, kmod, v, r, cl, lw))
    kkp = kkp / jnp.maximum(jnp.sqrt(_pair_sum(kkp * kkp, hd)), KK_EPS)
    bp = kkp * ap
    cl_last = clp[:, c - 1:c, :]
    e_neg = jnp.exp(-clp)
    e_end = jnp.exp(cl_last - clp)
    return dict(
        lhs=jnp.concatenate([kkp * jnp.exp(clp - lwp), rp * jnp.exp(clp)], axis=1),
        bd_b=_pair_diag(bp * e_neg, hd), bd_k=_pair_diag(kmp * e_neg, hd), bd_v=_pair_diag(vp, hd),
        k_end=kmp * e_end, b_end=bp * e_end, vp=vp,
        bonus=_pair_sum(rp * kmp * rk_pair, hd) * vp,
        e_tot=jnp.exp(cl_last))


def _rwkv_solve(p, s0, lnw_pair, lnb_pair, hd, c):
    pw = 2 * hd
    ti = lax.broadcasted_iota(jnp.int32, (c, 2 * c), 0)
    si = lax.broadcasted_iota(jnp.int32, (c, 2 * c), 1)
    si = jnp.where(si >= c, si - c, si)
    strict = (ti > si)[None]
    lower = (ti >= si)[None]
    lhs = p["lhs"]
    ab = _bdot(lhs, p["bd_b"], BNT)
    ak = _bdot(lhs, p["bd_k"], BNT)
    su = _bdot(lhs, s0, BNT)
    x = su[:, :c] + _bdot(jnp.where(strict, ak[:, :c], 0.0), p["bd_v"], BNN)
    lp = jnp.where(strict, ab[:, :c], 0.0)
    x = x - _bdot(lp, _pair_diag(x, hd), BNN)
    pw2 = 2
    while pw2 < c:
        lp = _bdot(lp, _pair_diag(lp, c), BNN)
        x = x + _bdot(lp, _pair_diag(x, hd), BNN)
        pw2 *= 2
    y = (su[:, c:] + _bdot(jnp.where(lower, ak[:, c:], 0.0), p["bd_v"], BNN)
         - _bdot(jnp.where(lower, ab[:, c:], 0.0), _pair_diag(x, hd), BNN))
    full = s0 * p["e_tot"] + _bdot(p["vp"], p["k_end"], BTN) - _bdot(x, p["b_end"], BTN)
    same_head = ((lax.broadcasted_iota(jnp.int32, (pw, pw), 0) < hd)
                 == (lax.broadcasted_iota(jnp.int32, (pw, pw), 1) < hd))[None]
    s_new = jnp.where(same_head, full, 0.0)
    mean = _pair_sum(y, hd) * (1.0 / hd)
    var = _pair_sum(jnp.square(y - mean), hd) * (1.0 / hd)
    yn = (y - mean) * lax.rsqrt(var + GN_EPS) * lnw_pair + lnb_pair
    return yn + p["bonus"], s_new


def _rwkv_body(*refs, n_valid, hd, n_alias, n_chunks):
    (zr_ref, zk_ref, zv_ref, zg_ref, lw_ref, a_ref, pr_ref, pk_ref, pv_ref, s0_ref,
     mur_ref, muk_ref, muv_ref, kk_ref, ka_ref, rk_ref, lnw_ref, lnb_ref) = refs[:_RWKV_INPUTS]
    o_ref, sout_ref, s_scr, cr_scr, ck_scr, cv_scr = refs[_RWKV_INPUTS + n_alias:]
    j = pl.program_id(2)
    sb, cblk, hbw = zr_ref.shape
    c = cblk // n_chunks
    pw = 2 * hd
    npair = hbw // pw
    n = sb * npair

    @pl.when(j == 0)
    def _():
        s0 = s0_ref[...].reshape(sb, npair, 2, hd, hd)
        zero = jnp.zeros((sb, npair, hd, hd), F32)
        top = jnp.concatenate([s0[:, :, 0], zero], axis=-1)
        bot = jnp.concatenate([zero, s0[:, :, 1]], axis=-1)
        s_scr[...] = jnp.concatenate([top, bot], axis=-2).reshape(n, pw, pw)
        cr_scr[...] = pr_ref[...]
        ck_scr[...] = pk_ref[...]
        cv_scr[...] = pv_ref[...]

    par_pair = lambda ref: jnp.concatenate([_split_heads(ref[...][None], pw)] * sb, axis=0)
    chunk_rows = [slice(ci * c, (ci + 1) * c) for ci in range(n_chunks)]
    preps = [_rwkv_prep(rows, (zr_ref, zk_ref, zv_ref), (cr_scr, ck_scr, cv_scr),
                        (mur_ref, muk_ref, muv_ref), lw_ref[:, rows, :], a_ref[:, rows, :],
                        kk_ref[...], ka_ref[...], rk_ref[...], n_valid, hd) for rows in chunk_rows]
    s_new = s_scr[...]
    for rows, ops in zip(chunk_rows, preps):
        y, s_new = _rwkv_solve(ops, s_new, par_pair(lnw_ref), par_pair(lnb_ref), hd, c)
        out = (_merge_heads(y, sb) * _silu(zg_ref[:, rows, :])).astype(o_ref.dtype)
        if len(o_ref.shape) == 2:
            o_ref[...] = out.reshape(o_ref.shape)
        else:
            o_ref[:, rows, :] = out
    s_scr[...] = s_new

    @pl.when(j == pl.num_programs(2) - 1)
    def _():
        s4 = s_new.reshape(sb, npair, pw, pw)
        both = jnp.stack([s4[:, :, :hd, :hd], s4[:, :, hd:, hd:]], axis=2)
        sout_ref[...] = both.reshape(sout_ref.shape)


def _rwkv(z3, lw, a, zprev, s0_all, l_in, mix, s_acc, l, depth, muz, kkp, kap, rkp, lnw, lnb,
          *, wa, wb, hbw, plan):
    b, t, _ = z3.shape
    n_chunks = math.gcd(t // plan.chunk, RWKV_CHUNKS_PER_STEP)
    sb, c = plan.sb_rwkv, plan.chunk * n_chunks
    hd = s0_all.shape[-1]
    nb = wb // hbw
    nh = hbw // hd
    off = 2 * wa // hbw
    zspec = lambda o: pl.BlockSpec((sb, c, hbw), lambda i, h, j, o=o: (i, j, o + h))
    pspec = lambda o: pl.BlockSpec((sb, 1, hbw), lambda i, h, j, o=o: (i, 0, o + h))
    mspec = lambda o: pl.BlockSpec((1, hbw), lambda i, h, j, o=o: (0, o + h))
    act = pl.BlockSpec((sb, c, hbw), lambda i, h, j: (i, j, h))
    st = lambda ll: pl.BlockSpec((None, sb, nh, hd, hd), lambda i, h, j, ll=ll: (ll, i, h, 0, 0))
    aliased = [mix, s_acc]
    n_in = _RWKV_INPUTS
    aliases = {n_in: 0, n_in + 1: 1}
    out = pl.pallas_call(
        functools.partial(_rwkv_body, n_valid=min(plan.n_valid, plan.chunk), hd=hd, n_alias=len(aliased),
                          n_chunks=n_chunks),
        grid=(b // sb, nb, t // c),
        in_specs=[zspec(off), zspec(off + nb), zspec(off + 2 * nb), zspec(off + 3 * nb), act, act,
                  pspec(0), pspec(nb), pspec(2 * nb), st(l_in),
                  mspec(0), mspec(nb), mspec(2 * nb), mspec(0), mspec(0), mspec(0), mspec(0), mspec(0)]
        + [_ANY] * len(aliased),
        out_specs=[_act_spec(plan, t, sb, c, hbw, lambda h: wa // hbw + h, 3), st(l)],
        out_shape=[jax.ShapeDtypeStruct(mix.shape, mix.dtype),
                   jax.ShapeDtypeStruct((depth,) + s0_all.shape[1:], F32)],
        input_output_aliases=aliases,
        scratch_shapes=[pltpu.VMEM((sb * nh // 2, 2 * hd, 2 * hd), F32), pltpu.VMEM((sb, 1, hbw), F32),
                        pltpu.VMEM((sb, 1, hbw), F32), pltpu.VMEM((sb, 1, hbw), F32)],
        compiler_params=_cparams(3),
        name="rwkv7",
    )(z3, z3, z3, z3, lw, a, zprev, zprev, zprev, s0_all, muz, muz, muz, kkp, kap, rkp, lnw, lnb,
      *aliased)
    return out


def _rwkv_steps_body(*refs, n_valid, tpad, hd, n_alias):
    (zr_ref, zk_ref, zv_ref, zg_ref, lw_ref, a_ref, pr_ref, pk_ref, pv_ref, s0_ref,
     mur_ref, muk_ref, muv_ref, kk_ref, ka_ref, rk_ref, lnw_ref, lnb_ref) = refs[:_RWKV_INPUTS]
    o_ref, sout_ref, o_scr, vt_scr, y_scr = refs[_RWKV_INPUTS + n_alias:]
    bsz, cols = pr_ref.shape
    nh = cols // hd
    sout_ref[...] = s0_ref[...]
    o_scr[...] = jnp.zeros_like(o_scr)
    per_col = lambda ref: jnp.broadcast_to(ref[...], (bsz, cols)).T
    lnw_c, lnb_c, rk_c = per_col(lnw_ref), per_col(lnb_ref), per_col(rk_ref)
    step_rows = lambda ref, t: ref[pl.ds(t, bsz, stride=tpad), :]
    prev = (pr_ref[...], pk_ref[...], pv_ref[...])
    for t in range(n_valid):
        z = (step_rows(zr_ref, t), step_rows(zk_ref, t), step_rows(zv_ref, t))
        r, k, v = (zz + (pp - zz) * mu[...] for zz, pp, mu in zip(z, prev, (mur_ref, muk_ref, muv_ref)))
        prev = z
        a = step_rows(a_ref, t)
        w_t = jnp.exp(step_rows(lw_ref, t)).T
        a_t = a.T
        r_t = r.T
        v_t = v.T
        kk_t = (k * kk_ref[...]).T
        km_t = (k * (1.0 + (a - 1.0) * ka_ref[...])).T
        vt_scr[...] = v_t
        for hh in range(nh):
            sl = slice(hh * hd, (hh + 1) * hd)
            kkh = kk_t[sl]
            kkh = kkh / jnp.maximum(jnp.sqrt(jnp.sum(kkh * kkh, axis=0, keepdims=True)), KK_EPS)
            bh, wh, kmh, rh = kkh * a_t[sl], w_t[sl], km_t[sl], r_t[sl]

            def value_row(vi, carry, hh=hh, kkh=kkh, bh=bh, wh=wh, kmh=kmh, rh=rh):
                s = sout_ref[hh, vi]
                sa = jnp.sum(s * kkh, axis=0, keepdims=True)
                s = s * wh - sa * bh + vt_scr[pl.ds(hh * hd + vi, 1), :] * kmh
                sout_ref[hh, vi] = s
                y_scr[pl.ds(hh * hd + vi, 1), :] = jnp.sum(s * rh, axis=0, keepdims=True)
                return carry

            lax.fori_loop(0, hd, value_row, 0, unroll=8)
        y = y_scr[...]
        outs = []
        for hh in range(nh):
            sl = slice(hh * hd, (hh + 1) * hd)
            yh = y[sl]
            mean = jnp.mean(yh, axis=0, keepdims=True)
            var = jnp.mean(jnp.square(yh - mean), axis=0, keepdims=True)
            yn = (yh - mean) * lax.rsqrt(var + GN_EPS) * lnw_c[sl] + lnb_c[sl]
            bonus = jnp.sum(r_t[sl] * km_t[sl] * rk_c[sl], axis=0, keepdims=True) * v_t[sl]
            outs.append(yn + bonus)
        out = jnp.concatenate(outs, axis=0).T * _silu(step_rows(zg_ref, t))
        o_scr[pl.ds(t, bsz, stride=tpad), :] = out
    o_ref[...] = o_scr[...].astype(o_ref.dtype)


def _rwkv_steps(z2, lw2, a2, zprev, s0_all, l_in, mix, s_acc, l, depth, muz, kkp, kap, rkp, lnw, lnb,
                *, wa, wb, plan, tpad):
    bsz = zprev.shape[0]
    rows = z2.shape[0]
    hd = s0_all.shape[-2]
    cols = LANES
    nb = wb // cols
    nh = cols // hd
    off = 2 * wa // cols
    zspec = lambda o: pl.BlockSpec((rows, cols), lambda h, o=o: (0, o + h))
    pspec = lambda o: pl.BlockSpec((bsz, cols), lambda h, o=o: (0, o + h))
    mspec = lambda o: pl.BlockSpec((1, cols), lambda h, o=o: (0, o + h))
    st = lambda ll: pl.BlockSpec((None, nh, hd, hd, bsz), lambda h, ll=ll: (ll, h, 0, 0, 0))
    aliased = [mix, s_acc]
    n_in = _RWKV_INPUTS
    aliases = {n_in: 0, n_in + 1: 1}
    return pl.pallas_call(
        functools.partial(_rwkv_steps_body, n_valid=plan.n_valid, tpad=tpad, hd=hd, n_alias=len(aliased)),
        grid=(nb,),
        in_specs=[zspec(off), zspec(off + nb), zspec(off + 2 * nb), zspec(off + 3 * nb),
                  zspec(0), zspec(0), pspec(0), pspec(nb), pspec(2 * nb), st(l_in),
                  mspec(0), mspec(nb), mspec(2 * nb), mspec(0), mspec(0), mspec(0), mspec(0), mspec(0)]
        + [_ANY] * len(aliased),
        out_specs=[pl.BlockSpec((rows, cols), lambda h: (0, wa // cols + h)), st(l)],
        out_shape=[jax.ShapeDtypeStruct(mix.shape, mix.dtype),
                   jax.ShapeDtypeStruct((depth,) + s0_all.shape[1:], F32)],
        input_output_aliases=aliases,
        scratch_shapes=[pltpu.VMEM((rows, cols), F32), pltpu.VMEM((cols, bsz), F32),
                        pltpu.VMEM((cols, bsz), F32)],
        compiler_params=_cparams(1),
        name="rwkv7_steps",
    )(z2, z2, z2, z2, lw2, a2, zprev, zprev, zprev, s0_all, muz, muz, muz, kkp, kap, rkp, lnw, lnb,
      *aliased)


def _hgrn_head(q, f, v, g, s0, lb, ng, n_valid, sub, stage):
    b2_ref, kf_ref, v_ref = stage
    bcast_row = lambda ref, i: ref[pl.ds(i, 1), :]
    c, dk = q.shape
    sig = jax.nn.sigmoid(f)
    logg = jnp.log(jnp.maximum(lb + (1.0 - lb) * sig, TINY))
    kf = (1.0 - lb) * jax.nn.sigmoid(-f)
    if n_valid < c:
        valid = lax.broadcasted_iota(jnp.int32, (c, dk), 0) < n_valid
        logg = jnp.where(valid, logg, 0.0)
        kf = jnp.where(valid, kf, 0.0)
    bc = _cumsum_time(logg, 0)
    b2 = bc * LOG2E
    b2_ref[...] = b2
    kf_ref[...] = kf
    v_ref[...] = v
    o_inter = _bdot(q * jnp.exp(bc), s0)
    pairs, w_rows = [], []
    for lo in range(0, c, sub):
        for r0 in range(lo, lo + sub, SUBLANES):
            qg, bg = q[r0:r0 + SUBLANES], b2[r0:r0 + SUBLANES]
            for src in range(lo, min(r0 + SUBLANES, n_valid)):
                pairs.append((r0, src))
                w_rows.append(qg * jnp.exp2(bg - bcast_row(b2_ref, src)) * bcast_row(kf_ref, src))
    att_rows = _bdot(jnp.concatenate(w_rows, axis=0), jnp.ones((dk, v.shape[1]), BF16))
    t_row = lax.broadcasted_iota(jnp.int32, (SUBLANES, 1), 0)
    groups = {}
    for u, (r0, src) in enumerate(pairs):
        col = att_rows[u * SUBLANES:(u + 1) * SUBLANES]
        if src > r0:
            col = jnp.where(t_row >= src - r0, col, 0.0)
        term = col * bcast_row(v_ref, src)
        groups[r0] = groups[r0] + term if r0 in groups else term
    blocks = []
    for lo in range(0, c, sub):
        ob = o_inter[lo:lo + sub]
        if lo > 0:
            bs = bc[lo - 1:lo]
            att = _bdot(q[lo:lo + sub] * jnp.exp(bc[lo:lo + sub] - bs), kf[:lo] * jnp.exp(bs - bc[:lo]), NT)
            ob = ob + _bdot(att, v[:lo])
        none = jnp.zeros((SUBLANES, v.shape[1]), F32)
        blocks.append(ob + jnp.concatenate([groups.get(r0, none) for r0 in range(lo, lo + sub, SUBLANES)],
                                           axis=0))
    o = jnp.concatenate(blocks, axis=0) if len(blocks) > 1 else blocks[0]
    b_last = bc[c - 1:c]
    e_col = jnp.broadcast_to(jnp.exp(b_last), (v.shape[1], dk)).T
    s_new = e_col * s0 + _bdot(kf * jnp.exp(b_last - bc), v, TN)
    on = o * lax.rsqrt(jnp.mean(o * o, axis=-1, keepdims=True) + RMS_EPS) * ng
    return on * _silu(g), s_new


_HGRN_INPUTS = 7


def _hgrn_body(*refs, n_valid, dk, sub, n_alias):
    q_ref, f_ref, i_ref, g_ref, s0_ref, lb_ref, ng_ref = refs[:_HGRN_INPUTS]
    o_ref, sout_ref, s_scr, o_scr, b2_scr, kf_scr, v_scr = refs[_HGRN_INPUTS + n_alias:]
    j = pl.program_id(2)

    @pl.when(j == 0)
    def _():
        s_scr[...] = s0_ref[...]

    sb, c, hcw = q_ref.shape
    nh = hcw // dk
    lb = lb_ref[...]
    ng = ng_ref[...]

    def one(s, carry, slot0=0):
        q, f, v, g = q_ref[s], f_ref[s], i_ref[s], g_ref[s]
        outs = []
        for hh in range(nh):
            sl = slice(hh * dk, (hh + 1) * dk)
            slot = slot0 + hh
            o, s_new = _hgrn_head(q[:, sl], f[:, sl], v[:, sl], g[:, sl], s_scr[s, hh],
                                  lb[:, sl], ng[:, sl], n_valid, sub,
                                  (b2_scr.at[slot], kf_scr.at[slot], v_scr.at[slot]))
            s_scr[s, hh] = s_new
            outs.append(o)
        o_scr[s] = jnp.concatenate(outs, axis=1) if nh > 1 else outs[0]
        return carry

    unroll = math.gcd(sb, HGRN_UNROLL_SEQS)

    def group(gi, carry):
        for u in range(unroll):
            one(gi * unroll + u, carry, u * nh)
        return carry

    if sb == unroll:
        group(0, 0)
    else:
        lax.fori_loop(0, sb // unroll, group, 0)
    o_ref[...] = o_scr[...].reshape(o_ref.shape).astype(o_ref.dtype)

    @pl.when(j == pl.num_programs(2) - 1)
    def _():
        sout_ref[...] = s_scr[...]


def _hgrn(z3, s0_all, l_in, mix, s_acc, l, depth, lb, ng, *, col0, out_col0, wc, hcw, plan):
    b, t, _ = z3.shape
    sb, c = plan.sb_hgrn, plan.chunk
    dk, dv = s0_all.shape[-2], s0_all.shape[-1]
    nb = wc // hcw
    nh = hcw // dk
    off = col0 // hcw
    zspec = lambda o: pl.BlockSpec((sb, c, hcw), lambda i, h, j, o=o: (i, j, o + h))
    st = lambda ll: pl.BlockSpec((None, sb, nh, dk, dv), lambda i, h, j, ll=ll: (ll, i, h, 0, 0))
    par = pl.BlockSpec((1, hcw), lambda i, h, j: (0, h))
    aliased = [mix, s_acc]
    n_in = _HGRN_INPUTS
    aliases = {n_in: 0, n_in + 1: 1}
    return pl.pallas_call(
        functools.partial(_hgrn_body, n_valid=min(plan.n_valid, c), dk=dk, sub=min(2 * SUBLANES, c),
                          n_alias=len(aliased)),
        grid=(b // sb, nb, t // c),
        in_specs=[zspec(off), zspec(off + nb), zspec(off + 2 * nb), zspec(off + 3 * nb), st(l_in),
                  par, par] + [_ANY] * len(aliased),
        out_specs=[_act_spec(plan, t, sb, c, hcw, lambda h: out_col0 // hcw + h, 3), st(l)],
        out_shape=[jax.ShapeDtypeStruct(mix.shape, mix.dtype),
                   jax.ShapeDtypeStruct((depth,) + s0_all.shape[1:], F32)],
        input_output_aliases=aliases,
        scratch_shapes=[pltpu.VMEM((sb, nh, dk, dv), F32), pltpu.VMEM((sb, c, hcw), F32)]
        + [pltpu.VMEM((nh * math.gcd(sb, HGRN_UNROLL_SEQS), c, dk), F32)] * 3,
        compiler_params=_cparams(3),
        name="hgrn2",
    )(z3, z3, z3, z3, s0_all, lb, ng, *aliased)


def _post1_body(h_ref, m_ref, g_ref, h1_ref, hn_ref):
    m = m_ref[...].astype(F32)
    h1 = h_ref[...] + m * lax.rsqrt(jnp.mean(m * m, axis=-1, keepdims=True) + RMS_EPS) * g_ref[...]
    h1_ref[...] = h1
    hn_ref[...] = (h1 * lax.rsqrt(jnp.mean(h1 * h1, axis=-1, keepdims=True) + RMS_EPS)).astype(hn_ref.dtype)


def _post2_body(h_ref, p_ref, proj_ref, gate_ref, g_ref, o_ref):
    u = jnp.dot(p_ref[...], proj_ref[...], preferred_element_type=F32)
    x = u * jax.nn.sigmoid(gate_ref[...].astype(F32))
    o_ref[...] = h_ref[...] + x * lax.rsqrt(jnp.mean(x * x, axis=-1, keepdims=True) + RMS_EPS) * g_ref[...]


def _post_ple(h1, p2, proj_all, l, gate, gain, tr=256):
    n, d = h1.shape
    kp = p2.shape[1]
    tr = min(tr, n)
    row = pl.BlockSpec((tr, d), lambda i: (i, 0))
    return pl.pallas_call(
        _post2_body,
        grid=(n // tr,),
        in_specs=[row, pl.BlockSpec((tr, kp), lambda i: (i, 0)),
                  pl.BlockSpec((None, kp, d), lambda i: (l, 0, 0)), row,
                  pl.BlockSpec((1, d), lambda i: (0, 0))],
        out_specs=row,
        out_shape=jax.ShapeDtypeStruct((n, d), F32),
        compiler_params=_cparams(1),
        name="post_ple",
    )(h1, p2, proj_all, gate, gain)


def _rowwise(body, name, arrays, gain, out_dtypes, tr=256):
    n, d = arrays[0].shape
    tr = min(tr, n)
    row = pl.BlockSpec((tr, d), lambda i: (i, 0))
    return pl.pallas_call(
        body,
        grid=(n // tr,),
        in_specs=[row] * len(arrays) + [pl.BlockSpec((1, d), lambda i: (0, 0))],
        out_specs=[row] * len(out_dtypes),
        out_shape=[jax.ShapeDtypeStruct((n, d), dt) for dt in out_dtypes],
        compiler_params=_cparams(1),
        name=name,
    )(*arrays, gain)


def _block_diag_tiles(w):
    nb, bs, _ = w.shape
    per = LANES // bs
    w = w.reshape(nb // per, per, bs, bs)
    eye = jnp.eye(per, dtype=w.dtype)
    return jnp.einsum('tpab,pq->tpaqb', w, eye).reshape(nb // per, LANES, LANES)


def _layer_stack(x, p, st, W, mm, plan, zero_state):
    st_a_h, st_a_conv, st_b_s, st_b_x, st_b_z, st_c_s = st
    b, t, d = x.shape
    depth = W['g_pre'].shape[0]
    wa = st_a_h.shape[-1]
    hd = st_b_s.shape[-1]
    wb = st_b_s.shape[-3] * hd
    dk = st_c_s.shape[-2]
    wc = st_c_s.shape[-3] * dk
    mix_w = wa + wb + wc
    n = b * t
    nv = plan.n_valid
    hbw = math.gcd(math.gcd(2 * wa, wb), math.gcd(wa, 512))
    hcw = math.gcd(math.gcd(2 * wa + 4 * wb, wc), math.gcd(wa + wb, 512))

    lb_soft = jax.nn.softmax(W['hgrn_lb'].astype(F32), axis=0)
    lb_all = jnp.cumsum(lb_soft, axis=0) - lb_soft[0]
    p_act = p.astype(BF16)

    if plan.rwkv_steps:
        st_b_s = jnp.transpose(st_b_s, (0, 2, 3, 4, 1))
    rwkv_params = lambda l: (
        W['rwkv_mu_z'][l][None], W['rwkv_kk'][l][None], W['rwkv_ka'][l][None],
        W['rwkv_rk'][l].reshape(1, wb), W['rwkv_lnx_w'][l][None], W['rwkv_lnx_b'][l][None])

    h = x
    small = [[] for _ in range(4)]
    sb_acc = jnp.zeros((depth,) + st_b_s.shape[1:], F32)
    sc_acc = jnp.zeros((depth,) + st_c_s.shape[1:], F32)
    for l in range(depth):
        ls = 0 if zero_state else l
        xn, lw, a, xlast = _prenorm(
            h, W['g_pre'][l].reshape(1, 1, d), st_b_x[ls][:, None, :], W['rwkv_lora_in'][l],
            W['rwkv_w2'][l], W['rwkv_a2'][l], W['rwkv_w0'][l][None], W['rwkv_a0'][l][None], plan)
        z3 = mm('w_in', xn.reshape(n, d), l, F32).reshape(b, t, -1)
        cb8 = jnp.pad(st_a_conv[ls], ((0, 0), (SUBLANES - (CONV_W - 1), 0), (0, 0)))
        mix, ha, tail = _rglru(
            z3, st_a_h[ls][:, None, :], cb8, W['conv_a_w'][l][None], W['conv_a_b'][l].reshape(1, 1, wa),
            W['lru_wr_bd'][l], W['lru_br'][l][None], W['lru_wi_bd'][l], W['lru_bi'][l][None],
            W['lru_lambda'][l][None], wa=wa, mix_w=mix_w, plan=plan)
        if plan.rwkv_steps:
            mix, sb_acc = _rwkv_steps(
                z3.reshape(n, -1), lw.reshape(n, wb), a.reshape(n, wb), st_b_z[ls], st_b_s, ls,
                mix, sb_acc, l, depth, *rwkv_params(l), wa=wa, wb=wb, plan=plan, tpad=t)
        else:
            mix, sb_acc = _rwkv(
                z3, lw, a, st_b_z[ls][:, None, :], st_b_s, ls, mix, sb_acc, l, depth,
                *rwkv_params(l), wa=wa, wb=wb, hbw=hbw, plan=plan)
        mix, sc_acc = _hgrn(
            z3, st_c_s, ls, mix, sc_acc, l, depth, lb_all[l][None], W['hgrn_norm_g'][l][None],
            col0=2 * wa + 4 * wb, out_col0=wa + wb, wc=wc, hcw=hcw, plan=plan)
        mo = mm('w_out', mix.reshape(n, mix_w), l, BF16)
        h1, hn = _rowwise(_post1_body, "post_mix", [h.reshape(n, d), mo], W['g_post'][l][None],
                          [F32, BF16])
        gp = mm('ple_gate', hn, l, BF16)
        h = _post_ple(h1, p_act[l].reshape(n, -1), W['ple_proj'], l, gp, W['g_ple'][l][None]).reshape(b, t, d)
        small[0].append(ha[:, 0])
        small[1].append(tail[:, SUBLANES - (CONV_W - 1):])
        small[2].append(xlast[:, 0])
        small[3].append(z3[:, nv - 1, 2 * wa:2 * wa + 3 * wb])
    na_h, na_c, nb_x, nb_z = (jnp.stack(o) for o in small)
    if plan.rwkv_steps:
        sb_acc = jnp.transpose(sb_acc, (0, 4, 1, 2, 3))
    return h, (na_h, na_c, sb_acc, nb_x, nb_z, sc_acc)


def kernel(x_prompt, x_sample, p_prompt, p_sample, state_a_h, state_a_conv, state_b_S,
           state_b_xprev, state_b_zprev, state_c_S, g_pre, g_post, w_in, w_out,
           conv_a_w, conv_a_b, lru_wr, lru_br, lru_wi, lru_bi, lru_lambda,
           rwkv_mu_z, rwkv_mu_w, rwkv_mu_a, rwkv_w0, rwkv_w1, rwkv_w2, rwkv_a0, rwkv_a1,
           rwkv_a2, rwkv_kk, rwkv_ka, rwkv_rk, rwkv_lnx_w, rwkv_lnx_b, hgrn_lb, hgrn_norm_g,
           ple_proj, ple_gate, g_ple):
    depth = w_in.shape[0]
    W = dict(g_pre=g_pre, g_post=g_post, conv_a_w=conv_a_w, conv_a_b=conv_a_b,
             lru_br=lru_br, lru_bi=lru_bi, lru_lambda=lru_lambda, rwkv_mu_z=rwkv_mu_z,
             rwkv_mu_w=rwkv_mu_w, rwkv_mu_a=rwkv_mu_a, rwkv_w0=rwkv_w0, rwkv_a0=rwkv_a0,
             rwkv_kk=rwkv_kk, rwkv_ka=rwkv_ka, rwkv_rk=rwkv_rk, rwkv_lnx_w=rwkv_lnx_w,
             rwkv_lnx_b=rwkv_lnx_b, hgrn_lb=hgrn_lb, hgrn_norm_g=hgrn_norm_g, g_ple=g_ple)
    for name, w in (('rwkv_w2', rwkv_w2), ('rwkv_a2', rwkv_a2), ('ple_proj', ple_proj)):
        W[name] = w.astype(BF16)
    mw, ma = rwkv_mu_w[:, :, None], rwkv_mu_a[:, :, None]
    W['rwkv_lora_in'] = jnp.concatenate(
        [(1.0 - mw) * rwkv_w1, (1.0 - ma) * rwkv_a1, mw * rwkv_w1, ma * rwkv_a1], axis=-1).astype(BF16)
    W['lru_wr_bd'] = jnp.stack([_block_diag_tiles(lru_wr[l]) for l in range(depth)]).astype(BF16)
    W['lru_wi_bd'] = jnp.stack([_block_diag_tiles(lru_wi[l]) for l in range(depth)]).astype(BF16)

    bp, tp, d = x_prompt.shape
    bs, ts, _ = x_sample.shape
    dt = x_prompt.dtype
    st_s = (state_a_h, state_a_conv, state_b_S, state_b_xprev, state_b_zprev, state_c_S)
    st_p = tuple(jnp.zeros((1, bp) + s.shape[2:], dt) for s in st_s)

    big = dict(w_in=w_in, w_out=w_out, ple_gate=ple_gate)
    wq = {name: [None] * depth for name in big}

    def mm_sample(name, x, l, out_dtype):
        if x.shape[0] <= MM_TILE:
            y, wq[name][l] = _matmul_wcast(x, big[name], l, out_dtype)
            return y
        wq[name][l] = big[name][l].astype(BF16)
        return _matmul(x, wq[name][l], out_dtype)

    def mm_prompt(name, x, l, out_dtype):
        return _matmul(x, wq[name][l], out_dtype)

    tpad = -(-ts // SUBLANES) * SUBLANES
    pad_t = lambda v, ax: jnp.pad(v, [(0, tpad - ts) if i == ax else (0, 0) for i in range(v.ndim)])
    y_s, out_s = _layer_stack(pad_t(x_sample, 1), pad_t(p_sample, 2), st_s, W, mm_sample,
                              _make_plan(bs, tpad, ts), False)
    y_p, out_p = _layer_stack(x_prompt, p_prompt, st_p, W, mm_prompt, _make_plan(bp, tp, tp), True)
    return (y_p, y_s[:, :ts]) + out_p + out_s
```
